```python
import jax, jax.numpy as jnp
from jax import lax
import numpy as np

D_MODEL = 2048
BATCH = 2
SEQ = 4096
DEPTH = 2
DEC_BATCH = 128
DEC_SEQ = 8
PAST_LEN = 8192
PAGE_SIZE = 128

N_A_LAYERS = DEPTH // 2
N_B_LAYERS = DEPTH - N_A_LAYERS
D_CONV = D_MODEL
CONV_WIDTH = 31
HEAD_DIM = 64
N_HEADS = D_MODEL // HEAD_DIM
N_KV_HEADS = max(N_HEADS // 8, 1)
GROUP = N_HEADS // N_KV_HEADS
Q_WIDTH = N_HEADS * HEAD_DIM
KV_WIDTH = N_KV_HEADS * HEAD_DIM
WINDOW = 128
ATTN_BLOCK = WINDOW
RMS_EPS = 1e-6
LN_EPS = 1e-5

kernel_name = 'conformer_swa_sink_yoco_step'


def rms_norm(x, g):
    xf = x.astype(jnp.float32)
    y = xf * lax.rsqrt(jnp.mean(xf * xf, axis=-1, keepdims=True) + RMS_EPS)
    return (y * g.astype(jnp.float32)).astype(x.dtype)


def layer_norm(x, g, b):
    xf = x.astype(jnp.float32)
    mu = jnp.mean(xf, axis=-1, keepdims=True)
    xc = xf - mu
    var = jnp.mean(xc * xc, axis=-1, keepdims=True)
    y = xc * lax.rsqrt(var + LN_EPS) * g.astype(jnp.float32) + b.astype(jnp.float32)
    return y.astype(x.dtype)


def conv_mixer(u, buf, w_in, conv_w, conv_b, ln_g, ln_b, w_out):
    z = u @ w_in
    a, b, gate = jnp.split(z, 3, axis=-1)
    c = a * jax.nn.sigmoid(b)
    full = jnp.concatenate([buf.astype(c.dtype), c], axis=1)
    y = lax.conv_general_dilated(
        full, conv_w[:, None, :].astype(c.dtype), window_strides=(1,), padding='VALID',
        dimension_numbers=('NWC', 'WIO', 'NWC'), feature_group_count=D_CONV)
    y = y + conv_b
    y = jax.nn.silu(layer_norm(y, ln_g, ln_b))
    out = (y * jax.nn.silu(gate)) @ w_out
    return out, full[:, -(CONV_WIDTH - 1):]


def sink_softmax(s, mask, sink):
    s = jnp.where(mask, s, -jnp.inf)
    m = jnp.maximum(jnp.max(s, axis=-1, keepdims=True), sink)
    p = jnp.exp(s - m)
    return p / (jnp.sum(p, axis=-1, keepdims=True) + jnp.exp(sink - m))


def swa_prompt(q, k, v, sinks):
    n, t = q.shape[:2]
    nb = t // ATTN_BLOCK
    qb = q.reshape(n, nb, ATTN_BLOCK, N_KV_HEADS, GROUP, HEAD_DIM)

    def band(x):
        xb = x.reshape(n, nb, ATTN_BLOCK, N_KV_HEADS, HEAD_DIM)
        prev = jnp.pad(xb, ((0, 0), (1, 0), (0, 0), (0, 0), (0, 0)))[:, :-1]
        return jnp.concatenate([prev, xb], axis=2)

    kb, vb = band(k), band(v)
    s = jnp.einsum('bnqkgd,bnjkd->bnkgqj', qb, kb,
                   preferred_element_type=jnp.float32) * (HEAD_DIM ** -0.5)
    blk = jnp.arange(nb)[:, None, None]
    qpos = blk * ATTN_BLOCK + jnp.arange(ATTN_BLOCK)[None, :, None]
    kpos = (blk - 1) * ATTN_BLOCK + jnp.arange(2 * ATTN_BLOCK)[None, None, :]
    diff = qpos - kpos
    mask = ((diff >= 0) & (diff < WINDOW) & (kpos >= 0))[None, :, None, None]
    p = sink_softmax(s, mask, sinks.astype(jnp.float32).reshape(N_KV_HEADS, GROUP, 1, 1))
    o = jnp.einsum('bnkgqj,bnjkd->bnqkgd', p.astype(v.dtype), vb)
    return o.reshape(n, t, Q_WIDTH)


def swa_sample(q, k_all, v_all, qpos, kpos, sinks):
    n, t = q.shape[:2]
    qg = q.reshape(n, t, N_KV_HEADS, GROUP, HEAD_DIM)
    s = jnp.einsum('bqkgd,bjkd->bkgqj', qg, k_all,
                   preferred_element_type=jnp.float32) * (HEAD_DIM ** -0.5)
    diff = qpos[:, None] - kpos[None, :]
    mask = ((diff >= 0) & (diff < WINDOW) & (kpos[None, :] >= 0))[None, None, None]
    p = sink_softmax(s, mask, sinks.astype(jnp.float32).reshape(N_KV_HEADS, GROUP, 1, 1))
    o = jnp.einsum('bkgqj,bjkd->bqkgd', p.astype(v_all.dtype), v_all)
    return o.reshape(n, t, Q_WIDTH)


def trunk(x, conv_state, cache_k, cache_v, past_len, norm_pre, norm_post, w_in_a, conv_w,
          conv_b, ln_g, ln_b, w_out_a, kv_norm, w_kv, w_in_b, sinks, w_out_b):
    n, t = x.shape[:2]
    h = x
    new_conv = []
    new_k = new_v = None
    attend = None
    for layer in range(DEPTH):
        u = rms_norm(h, norm_pre[layer])
        if layer < N_A_LAYERS:
            o, buf = conv_mixer(u, conv_state[layer], w_in_a[layer], conv_w[layer], conv_b[layer],
                                ln_g[layer], ln_b[layer], w_out_a[layer])
            new_conv.append(buf)
        else:
            if layer == N_A_LAYERS:
                kv = rms_norm(h, kv_norm) @ w_kv
                k, v = jnp.split(kv, 2, axis=-1)
                k = k.reshape(n, t, N_KV_HEADS, HEAD_DIM)
                v = v.reshape(n, t, N_KV_HEADS, HEAD_DIM)
                if cache_k is None:
                    keep = min(WINDOW, t)
                    new_k, new_v = k[:, -keep:], v[:, -keep:]
                    attend = lambda q, sk, k=k, v=v: swa_prompt(q, k, v, sk)
                else:
                    w_buf = cache_k.shape[1]
                    k_all = jnp.concatenate([cache_k.astype(k.dtype), k], axis=1)
                    v_all = jnp.concatenate([cache_v.astype(v.dtype), v], axis=1)
                    kpos = jnp.concatenate([past_len - w_buf + jnp.arange(w_buf),
                                            past_len + jnp.arange(t)])
                    qpos = past_len + jnp.arange(t)
                    new_k, new_v = k_all[:, -w_buf:], v_all[:, -w_buf:]
                    attend = lambda q, sk, k_all=k_all, v_all=v_all, qpos=qpos, kpos=kpos: \
                        swa_sample(q, k_all, v_all, qpos, kpos, sk)
            j = layer - N_A_LAYERS
            z = u @ w_in_b[j]
            q, gate = jnp.split(z, 2, axis=-1)
            q = q.reshape(n, t, N_HEADS, HEAD_DIM)
            o = (attend(q, sinks[j]) * jax.nn.silu(gate)) @ w_out_b[j]
        h = h + rms_norm(o, norm_post[layer])
    return h, jnp.stack(new_conv), new_k, new_v


def setup_inputs(seed: int = 0) -> dict:
    key = jax.random.key(seed)
    ks = jax.random.split(key, 20)
    f32 = jnp.float32
    w_buf = min(WINDOW, PAST_LEN)
    nrm = lambda k, shape, scale: jax.random.normal(k, shape, f32) * scale
    return {
        'x_prompt': nrm(ks[0], (BATCH, SEQ, D_MODEL), 1.0),
        'x_sample': nrm(ks[1], (DEC_BATCH, DEC_SEQ, D_MODEL), 1.0),
        'state_conv': nrm(ks[2], (N_A_LAYERS, DEC_BATCH, CONV_WIDTH - 1, D_CONV), 0.5),
        'cache_k': nrm(ks[3], (DEC_BATCH, w_buf, N_KV_HEADS, HEAD_DIM), 1.0),
        'cache_v': nrm(ks[4], (DEC_BATCH, w_buf, N_KV_HEADS, HEAD_DIM), 1.0),
        'norm_pre': 1.0 + nrm(ks[5], (DEPTH, D_MODEL), 0.05),
        'norm_post': 1.0 + nrm(ks[6], (DEPTH, D_MODEL), 0.05),
        'w_in_a': nrm(ks[7], (N_A_LAYERS, D_MODEL, 3 * D_CONV), D_MODEL ** -0.5),
        'conv_w': nrm(ks[8], (N_A_LAYERS, CONV_WIDTH, D_CONV), CONV_WIDTH ** -0.5),
        'conv_b': nrm(ks[9], (N_A_LAYERS, D_CONV), 0.02),
        'ln_g': 1.0 + nrm(ks[10], (N_A_LAYERS, D_CONV), 0.05),
        'ln_b': nrm(ks[11], (N_A_LAYERS, D_CONV), 0.02),
        'w_out_a': nrm(ks[12], (N_A_LAYERS, D_CONV, D_MODEL), D_CONV ** -0.5),
        'kv_norm': 1.0 + nrm(ks[13], (D_MODEL,), 0.05),
        'w_kv': nrm(ks[14], (D_MODEL, 2 * KV_WIDTH), D_MODEL ** -0.5),
        'w_in_b': nrm(ks[15], (N_B_LAYERS, D_MODEL, 2 * Q_WIDTH), D_MODEL ** -0.5),
        'sinks': nrm(ks[16], (N_B_LAYERS, N_HEADS), 1.0),
        'w_out_b': nrm(ks[17], (N_B_LAYERS, Q_WIDTH, D_MODEL), Q_WIDTH ** -0.5),
    }


def reference(x_prompt, x_sample, state_conv, cache_k, cache_v, norm_pre, norm_post, w_in_a,
              conv_w, conv_b, ln_g, ln_b, w_out_a, kv_norm, w_kv, w_in_b, sinks, w_out_b):
    conv0 = jnp.zeros((N_A_LAYERS, x_prompt.shape[0], CONV_WIDTH - 1, D_CONV), x_prompt.dtype)
    y_prompt, state_conv_prompt, cache_k_prompt, cache_v_prompt = trunk(
        x_prompt, conv0, None, None, 0, norm_pre, norm_post, w_in_a, conv_w, conv_b, ln_g, ln_b,
        w_out_a, kv_norm, w_kv, w_in_b, sinks, w_out_b)
    y_sample, state_conv_sample, cache_k_sample, cache_v_sample = trunk(
        x_sample, state_conv, cache_k, cache_v, PAST_LEN, norm_pre, norm_post, w_in_a, conv_w,
        conv_b, ln_g, ln_b, w_out_a, kv_norm, w_kv, w_in_b, sinks, w_out_b)
    return (y_prompt, y_sample, state_conv_prompt, cache_k_prompt, cache_v_prompt,
            state_conv_sample, cache_k_sample, cache_v_sample)
```

```python
import functools

import jax
import jax.numpy as jnp
from jax import lax
from jax.experimental import pallas as pl
from jax.experimental.pallas import tpu as pltpu

RMS_EPS = 1e-6
LN_EPS = 1e-5
HEAD_DIM = 64
GROUP = 8
WINDOW = 128
SUBLANES = 8
LANES = 128
HALO = 32
VMEM_LIMIT = 56 * 1024 * 1024
ROW_BLK = 64

BF16 = jnp.bfloat16
F32 = jnp.float32


def _rows(i, n):
    return pl.ds(pl.multiple_of(i * n, n), n)


def _rms_scale(x):
    return lax.rsqrt(jnp.mean(x * x, axis=-1, keepdims=True) + RMS_EPS)


def _dot(a, b):
    return jnp.dot(a, b, preferred_element_type=F32)


def _dot_t(a, b):
    return lax.dot_general(a, b, (((1,), (1,)), ((), ())), preferred_element_type=F32)


def _pre_norm_to_bf16(x_ref, g_ref, u_ref, tm):
    def body(r, c):
        rows = _rows(r, ROW_BLK)
        x = x_ref[rows, :]
        u_ref[rows, :] = (x * _rms_scale(x) * g_ref[...]).astype(BF16)
        return c
    lax.fori_loop(0, tm // ROW_BLK, body, 0)


def _glu_chunk(u_ref, wa_ref, wb_ref, wg_ref):
    u = u_ref[...]
    c = _dot(u, wa_ref[...]) * jax.nn.sigmoid(_dot(u, wb_ref[...]))
    sg = jax.nn.silu(_dot(u, wg_ref[...])).astype(BF16)
    return c, sg


def _broadcast_taps(cw_ref, wbc_ref, taps):
    for k in range(taps):
        wbc_ref[k] = jnp.broadcast_to(cw_ref[k:k + 1, :], wbc_ref.shape[1:])


def _conv_finalize(j_chunks, tm, cn, y_ref, sg_ref, act_ref, x_ref, h_ref, cb_ref, lng_ref, lnb_ref,
                   gpost_ref, wout_ref):
    d = j_chunks * cn

    def ln_body(r, c):
        rows = _rows(r, ROW_BLK)
        ys = [y_ref[jj, rows, :] + cb_ref[:, jj * cn:(jj + 1) * cn] for jj in range(j_chunks)]
        mu = sum(jnp.sum(y, axis=-1, keepdims=True) for y in ys) * (1.0 / d)
        yc = [y - mu for y in ys]
        var = sum(jnp.sum(y * y, axis=-1, keepdims=True) for y in yc) * (1.0 / d)
        rstd = lax.rsqrt(var + LN_EPS)
        for jj in range(j_chunks):
            cols = slice(jj * cn, (jj + 1) * cn)
            t = jax.nn.silu(yc[jj] * rstd * lng_ref[:, cols] + lnb_ref[:, cols])
            act_ref[rows, cols] = (t * sg_ref[jj, rows, :].astype(F32)).astype(BF16)
        return c
    lax.fori_loop(0, tm // ROW_BLK, ln_body, 0)

    h_ref[...] = _dot(act_ref[...], wout_ref[...])

    def post_body(r, c):
        rows = _rows(r, ROW_BLK)
        o = h_ref[rows, :]
        h_ref[rows, :] = x_ref[rows, :] + o * _rms_scale(o) * gpost_ref[...]
        return c
    lax.fori_loop(0, tm // ROW_BLK, post_body, 0)


def _conv_prompt_kernel(x_ref, wa_ref, wb_ref, wg_ref, cw_ref, cb_ref, lng_ref, lnb_ref, gpre_ref,
                        gpost_ref, wout_ref, h_ref, st_ref,
                        u_ref, cext_ref, wbc_ref, y_ref, sg_ref, act_ref, *, tm, cn, taps, grp):
    i = pl.program_id(1)
    j = pl.program_id(2)
    n_i = pl.num_programs(1)
    n_j = pl.num_programs(2)

    @pl.when(j == 0)
    def _():
        _pre_norm_to_bf16(x_ref, gpre_ref, u_ref, tm)

    @pl.when(i == 0)
    def _():
        cext_ref[j, 0:HALO, :] = jnp.zeros((HALO, cn), F32)

    @pl.when(i > 0)
    def _():
        cext_ref[j, 0:HALO, :] = cext_ref[j, tm:tm + HALO, :]

    c, sg = _glu_chunk(u_ref, wa_ref, wb_ref, wg_ref)
    cext_ref[j, HALO:HALO + tm, :] = c
    sg_ref[j] = sg
    _broadcast_taps(cw_ref, wbc_ref, taps)

    sub = lax.broadcasted_iota(jnp.int32, (SUBLANES, LANES), 0)
    n_q = (taps + SUBLANES - 1) // SUBLANES

    def conv_body(g, carry):
        for lb in range(cn // LANES):
            ls = slice(lb * LANES, (lb + 1) * LANES)
            base = g * (grp * SUBLANES)
            blks = [cext_ref[j, pl.ds(pl.multiple_of(base + SUBLANES * m, SUBLANES), SUBLANES), ls]
                    for m in range(grp + n_q)]
            acc = [None] * grp
            for r in range(SUBLANES):
                if r == 0:
                    mer = blks
                else:
                    keep = sub < (SUBLANES - r)
                    mer = [None] + [jnp.where(keep, blks[m], blks[m - 1]) for m in range(1, grp + n_q)]
                part = [None] * grp
                for q in range(n_q):
                    d = SUBLANES * q + r
                    if d > taps - 1:
                        continue
                    wv = wbc_ref[taps - 1 - d, :, ls]
                    for gi in range(grp):
                        t = wv * mer[gi + n_q - q]
                        part[gi] = t if part[gi] is None else part[gi] + t
                for gi in range(grp):
                    p = part[gi] if r == 0 else pltpu.roll(part[gi], r, axis=0)
                    acc[gi] = p if acc[gi] is None else acc[gi] + p
            for gi in range(grp):
                y_ref[j, pl.ds(pl.multiple_of(base + SUBLANES * gi, SUBLANES), SUBLANES), ls] = acc[gi]
        return carry
    lax.fori_loop(0, tm // (grp * SUBLANES), conv_body, 0)

    @pl.when(j == n_j - 1)
    def _():
        _conv_finalize(cext_ref.shape[0], tm, cn, y_ref, sg_ref, act_ref, x_ref, h_ref, cb_ref,
                       lng_ref, lnb_ref, gpost_ref, wout_ref)

        @pl.when(i == n_i - 1)
        def _():
            for jj in range(cext_ref.shape[0]):
                st_ref[:, jj * cn:(jj + 1) * cn] = cext_ref[jj, HALO + tm - (taps - 1):HALO + tm, :]


def _conv_sample_kernel(x_ref, st_in_ref, wa_ref, wb_ref, wg_ref, cw_ref, cb_ref, lng_ref, lnb_ref,
                        gpre_ref, gpost_ref, wout_ref, h_ref, st_ref,
                        u_ref, c_ref, full_ref, wbc_ref, y_ref, sg_ref, act_ref, *, tm, cn, taps, t_new):
    j = pl.program_id(1)
    n_j = pl.num_programs(1)
    n_seq = tm // t_new
    hist = taps - 1

    @pl.when(j == 0)
    def _():
        _pre_norm_to_bf16(x_ref, gpre_ref, u_ref, tm)

    c, sg = _glu_chunk(u_ref, wa_ref, wb_ref, wg_ref)
    c_ref[...] = c
    sg_ref[j] = sg
    _broadcast_taps(cw_ref, wbc_ref, taps)

    def seq_body(s, carry):
        for lb in range(cn // LANES):
            ls = slice(lb * LANES, (lb + 1) * LANES)
            full_ref[0:hist, ls] = st_in_ref[s, :, ls]
            full_ref[hist:hist + t_new, ls] = c_ref[pl.ds(pl.multiple_of(s * t_new, t_new), t_new), ls]
            acc = None
            for k in range(taps):
                t = wbc_ref[k, :, ls] * full_ref[k:k + t_new, ls]
                acc = t if acc is None else acc + t
            y_ref[j, pl.ds(pl.multiple_of(s * t_new, t_new), t_new), ls] = acc
            st_ref[s, :, ls] = full_ref[t_new:t_new + hist, ls]
        return carry
    lax.fori_loop(0, n_seq, seq_body, 0)

    @pl.when(j == n_j - 1)
    def _():
        _conv_finalize(y_ref.shape[0], tm, cn, y_ref, sg_ref, act_ref, x_ref, h_ref, cb_ref,
                       lng_ref, lnb_ref, gpost_ref, wout_ref)


def _const_spec(shape, n_grid):
    zeros = (0,) * len(shape)
    return pl.BlockSpec(shape, lambda *_: zeros, pipeline_mode=pl.Buffered(1))


def _conv_layer_prompt(x, w_in, cw, cb, lng, lnb, gpre, gpost, w_out, *, tm, cn):
    b, t, d = x.shape
    taps = cw.shape[0]
    n_j = d // cn
    grp = 4
    kern = functools.partial(_conv_prompt_kernel, tm=tm, cn=cn, taps=taps, grp=grp)
    vec = lambda: _const_spec((1, d), 3)
    return pl.pallas_call(
        kern,
        grid=(b, t // tm, n_j),
        in_specs=[
            pl.BlockSpec((None, tm, d), lambda bb, i, j: (bb, i, 0)),
            pl.BlockSpec((d, cn), lambda bb, i, j: (0, j)),
            pl.BlockSpec((d, cn), lambda bb, i, j: (0, n_j + j)),
            pl.BlockSpec((d, cn), lambda bb, i, j: (0, 2 * n_j + j)),
            pl.BlockSpec((taps, cn), lambda bb, i, j: (0, j)),
            vec(), vec(), vec(), vec(), vec(),
            _const_spec((d, d), 3),
        ],
        out_specs=[
            pl.BlockSpec((None, tm, d), lambda bb, i, j: (bb, i, 0)),
            pl.BlockSpec((None, taps - 1, d), lambda bb, i, j: (bb, 0, 0)),
        ],
        out_shape=[
            jax.ShapeDtypeStruct((b, t, d), F32),
            jax.ShapeDtypeStruct((b, taps - 1, d), F32),
        ],
        scratch_shapes=[
            pltpu.VMEM((tm, d), BF16),
            pltpu.VMEM((n_j, HALO + tm, cn), F32),
            pltpu.VMEM((taps, SUBLANES, cn), F32),
            pltpu.VMEM((n_j, tm, cn), F32),
            pltpu.VMEM((n_j, tm, cn), BF16),
            pltpu.VMEM((tm, d), BF16),
        ],
        compiler_params=pltpu.CompilerParams(
            dimension_semantics=("arbitrary", "arbitrary", "arbitrary"),
            vmem_limit_bytes=VMEM_LIMIT),
        name="conv_layer_prompt",
    )(x, w_in, w_in, w_in, cw, cb, lng, lnb, gpre, gpost, w_out)


def _conv_layer_sample(x, state, w_in, cw, cb, lng, lnb, gpre, gpost, w_out, *, n_seq_tile, cn):
    n_seq, t_new, d = x.shape
    taps = cw.shape[0]
    n_j = d // cn
    tm = n_seq_tile * t_new
    x2 = x.reshape(n_seq * t_new, d)
    kern = functools.partial(_conv_sample_kernel, tm=tm, cn=cn, taps=taps, t_new=t_new)
    vec = lambda: _const_spec((1, d), 2)
    h, st = pl.pallas_call(
        kern,
        grid=(n_seq // n_seq_tile, n_j),
        in_specs=[
            pl.BlockSpec((tm, d), lambda i, j: (i, 0)),
            pl.BlockSpec((n_seq_tile, taps - 1, cn), lambda i, j: (i, 0, j)),
            pl.BlockSpec((d, cn), lambda i, j: (0, j)),
            pl.BlockSpec((d, cn), lambda i, j: (0, n_j + j)),
            pl.BlockSpec((d, cn), lambda i, j: (0, 2 * n_j + j)),
            pl.BlockSpec((taps, cn), lambda i, j: (0, j)),
            vec(), vec(), vec(), vec(), vec(),
            _const_spec((d, d), 2),
        ],
        out_specs=[
            pl.BlockSpec((tm, d), lambda i, j: (i, 0)),
            pl.BlockSpec((n_seq_tile, taps - 1, cn), lambda i, j: (i, 0, j)),
        ],
        out_shape=[
            jax.ShapeDtypeStruct((n_seq * t_new, d), F32),
            jax.ShapeDtypeStruct((n_seq, taps - 1, d), F32),
        ],
        scratch_shapes=[
            pltpu.VMEM((tm, d), BF16),
            pltpu.VMEM((tm, cn), F32),
            pltpu.VMEM((HALO + t_new, cn), F32),
            pltpu.VMEM((taps, SUBLANES, cn), F32),
            pltpu.VMEM((n_j, tm, cn), F32),
            pltpu.VMEM((n_j, tm, cn), BF16),
            pltpu.VMEM((tm, d), BF16),
        ],
        compiler_params=pltpu.CompilerParams(
            dimension_semantics=("arbitrary", "arbitrary"),
            vmem_limit_bytes=VMEM_LIMIT),
        name="conv_layer_sample",
    )(x2, state, w_in, w_in, w_in, cw, cb, lng, lnb, gpre, gpost, w_out)
    return h.reshape(n_seq, t_new, d), st


def _attn_projections(h_ref, gkv_ref, gpre_ref, wkv_ref, win_ref, ukv_ref, u_ref, q_ref, sg_ref, tm,
                      n_chunk):
    d = h_ref.shape[-1]

    def body(r, c):
        rows = _rows(r, ROW_BLK)
        x = h_ref[rows, :]
        xn = x * _rms_scale(x)
        ukv_ref[rows, :] = (xn * gkv_ref[...]).astype(BF16)
        u_ref[rows, :] = (xn * gpre_ref[...]).astype(BF16)
        return c
    lax.fori_loop(0, tm // ROW_BLK, body, 0)

    kv = _dot(ukv_ref[...], wkv_ref[...])
    u = u_ref[...]
    cw = d // n_chunk
    scale = HEAD_DIM ** -0.5
    for n in range(n_chunk):
        cols = slice(n * cw, (n + 1) * cw)
        q_ref[:, cols] = (_dot(u, win_ref[:, cols]) * scale).astype(BF16)
    for n in range(n_chunk):
        cols = slice(n * cw, (n + 1) * cw)
        sg_ref[:, cols] = jax.nn.silu(_dot(u, win_ref[:, d + n * cw:d + (n + 1) * cw])).astype(BF16)
    return kv


def _attn_output(act_ref, h_ref, gpost_ref, wout_ref, y_ref, tm):
    y_ref[...] = _dot(act_ref[...], wout_ref[...])

    def post_body(r, c):
        rows = _rows(r, ROW_BLK)
        o = y_ref[rows, :]
        y_ref[rows, :] = h_ref[rows, :] + o * _rms_scale(o) * gpost_ref[...]
        return c
    lax.fori_loop(0, tm // ROW_BLK, post_body, 0)


def _attn_prompt_kernel(sink_ref, h_ref, gkv_ref, gpre_ref, gpost_ref, wkv_ref, win_ref, wout_ref,
                        y_ref, ck_ref, cv_ref,
                        ukv_ref, u_ref, q_ref, sg_ref, kbuf_ref, vbuf_ref, act_ref, *, tm, n_kv):
    i = pl.program_id(1)
    n_i = pl.num_programs(1)
    kvw = n_kv * HEAD_DIM
    blk = WINDOW

    kv = _attn_projections(h_ref, gkv_ref, gpre_ref, wkv_ref, win_ref, ukv_ref, u_ref, q_ref, sg_ref,
                           tm, 4)

    @pl.when(i == 0)
    def _():
        kbuf_ref[0:blk, :] = jnp.zeros((blk, kvw), BF16)
        vbuf_ref[0:blk, :] = jnp.zeros((blk, kvw), BF16)

    @pl.when(i > 0)
    def _():
        kbuf_ref[0:blk, :] = kbuf_ref[tm:tm + blk, :]
        vbuf_ref[0:blk, :] = vbuf_ref[tm:tm + blk, :]

    kbuf_ref[blk:blk + tm, :] = kv[:, :kvw].astype(BF16)
    vbuf_ref[blk:blk + tm, :] = kv[:, kvw:].astype(BF16)

    @pl.when(i == n_i - 1)
    def _():
        ck_ref[...] = kv[tm - blk:, :kvw]
        cv_ref[...] = kv[tm - blk:, kvw:]

    qi = lax.broadcasted_iota(jnp.int32, (blk, 2 * blk), 0)
    kj = lax.broadcasted_iota(jnp.int32, (blk, 2 * blk), 1)
    own = (kj >= blk) & (kj - blk <= qi)
    prev = (kj < blk) & (kj > qi)

    def blk_body(bi, carry):
        rows = _rows(bi, blk)
        first_key = jnp.where((i > 0) | (bi > 0), 0, blk)
        mask = own | (prev & (kj >= first_key))
        for g in range(n_kv):
            kcols = slice(g * HEAD_DIM, (g + 1) * HEAD_DIM)
            k = kbuf_ref[pl.ds(pl.multiple_of(bi * blk, blk), 2 * blk), kcols]
            v = vbuf_ref[pl.ds(pl.multiple_of(bi * blk, blk), 2 * blk), kcols]
            qg = jnp.concatenate(
                [q_ref[rows, (g * GROUP + hh) * HEAD_DIM:(g * GROUP + hh + 1) * HEAD_DIM]
                 for hh in range(GROUP)], axis=0)
            s = _dot_t(qg, k)
            ps, rden = [], []
            for hh in range(GROUP):
                sink = sink_ref[g * GROUP + hh]
                sh = jnp.where(mask, s[hh * blk:(hh + 1) * blk], -jnp.inf)
                m = jnp.maximum(jnp.max(sh, axis=-1, keepdims=True), sink)
                p = jnp.exp(sh - m)
                den = jnp.sum(p, axis=-1, keepdims=True) + jnp.exp(sink - m)
                ps.append(p.astype(BF16))
                rden.append(1.0 / den)
            o = _dot(jnp.concatenate(ps, axis=0), v)
            for hp in range(GROUP // 2):
                pair = jnp.concatenate(
                    [o[(2 * hp + e) * blk:(2 * hp + e + 1) * blk] * rden[2 * hp + e] for e in range(2)],
                    axis=1)
                c0 = (g * GROUP + 2 * hp) * HEAD_DIM
                act_ref[rows, c0:c0 + 2 * HEAD_DIM] = (
                    pair * sg_ref[rows, c0:c0 + 2 * HEAD_DIM].astype(F32)).astype(BF16)
        return carry
    lax.fori_loop(0, tm // blk, blk_body, 0)

    _attn_output(act_ref, h_ref, gpost_ref, wout_ref, y_ref, tm)


def _attn_layer_prompt(h, sinks, gkv, gpre, gpost, w_kv, w_in, w_out, *, tm):
    b, t, d = h.shape
    kvw = w_kv.shape[1] // 2
    n_kv = kvw // HEAD_DIM
    kern = functools.partial(_attn_prompt_kernel, tm=tm, n_kv=n_kv)
    vec = lambda: _const_spec((1, d), 2)
    return pl.pallas_call(
        kern,
        grid=(b, t // tm),
        in_specs=[
            pl.BlockSpec(memory_space=pltpu.SMEM),
            pl.BlockSpec((None, tm, d), lambda bb, i: (bb, i, 0)),
            vec(), vec(), vec(),
            _const_spec(w_kv.shape, 2),
            _const_spec(w_in.shape, 2),
            _const_spec(w_out.shape, 2),
        ],
        out_specs=[
            pl.BlockSpec((None, tm, d), lambda bb, i: (bb, i, 0)),
            pl.BlockSpec((None, WINDOW, kvw), lambda bb, i: (bb, 0, 0)),
            pl.BlockSpec((None, WINDOW, kvw), lambda bb, i: (bb, 0, 0)),
        ],
        out_shape=[
            jax.ShapeDtypeStruct((b, t, d), F32),
            jax.ShapeDtypeStruct((b, WINDOW, kvw), F32),
            jax.ShapeDtypeStruct((b, WINDOW, kvw), F32),
        ],
        scratch_shapes=[
            pltpu.VMEM((tm, d), BF16),
            pltpu.VMEM((tm, d), BF16),
            pltpu.VMEM((tm, d), BF16),
            pltpu.VMEM((tm, d), BF16),
            pltpu.VMEM((WINDOW + tm, kvw), BF16),
            pltpu.VMEM((WINDOW + tm, kvw), BF16),
            pltpu.VMEM((tm, d), BF16),
        ],
        compiler_params=pltpu.CompilerParams(
            dimension_semantics=("arbitrary", "arbitrary"),
            vmem_limit_bytes=VMEM_LIMIT),
        name="attn_layer_prompt",
    )(sinks, h, gkv, gpre, gpost, w_kv, w_in, w_out)


def _attn_sample_proj_kernel(h_ref, gkv_ref, gpre_ref, wkv_ref, win_ref, kv_ref, q_ref, sg_ref,
                             ukv_ref, u_ref, *, tm):
    kv_ref[...] = _attn_projections(h_ref, gkv_ref, gpre_ref, wkv_ref, win_ref, ukv_ref, u_ref, q_ref,
                                    sg_ref, tm, 4)


def _attn_sample_core_kernel(sink_ref, q_ref, sg_ref, kv_ref, ck_in_ref, cv_in_ref,
                             act_ref, ck_ref, cv_ref, *, n_seq, t_new, n_kv):
    kvw = n_kv * HEAD_DIM
    w_buf = ck_in_ref.shape[1]
    rows_g = GROUP * t_new
    n_keys = w_buf + 2 * t_new
    tq = lax.broadcasted_iota(jnp.int32, (rows_g, n_keys), 0) % t_new
    kj = lax.broadcasted_iota(jnp.int32, (rows_g, n_keys), 1)
    mask = ((kj < w_buf) & (kj > tq)) | ((kj >= w_buf) & (kj - w_buf <= tq))
    hrow = lax.broadcasted_iota(jnp.int32, (rows_g, 1), 0) // t_new

    def seq_body(s, carry):
        rows = pl.ds(pl.multiple_of(s * t_new, t_new), t_new)
        kc = ck_in_ref[s]
        vc = cv_in_ref[s]
        kvn = kv_ref[rows, :]
        ck_ref[s, 0:w_buf - t_new, :] = kc[t_new:]
        cv_ref[s, 0:w_buf - t_new, :] = vc[t_new:]
        ck_ref[s, w_buf - t_new:w_buf, :] = kvn[:, :kvw]
        cv_ref[s, w_buf - t_new:w_buf, :] = kvn[:, kvw:]
        pad = jnp.zeros((t_new, kvw), F32)
        k_all = jnp.concatenate([kc, kvn[:, :kvw], pad], axis=0).astype(BF16)
        v_all = jnp.concatenate([vc, kvn[:, kvw:], pad], axis=0).astype(BF16)
        for g in range(n_kv):
            kcols = slice(g * HEAD_DIM, (g + 1) * HEAD_DIM)
            qg = jnp.concatenate(
                [q_ref[rows, (g * GROUP + hh) * HEAD_DIM:(g * GROUP + hh + 1) * HEAD_DIM]
                 for hh in range(GROUP)], axis=0)
            sinkv = jnp.zeros((rows_g, 1), F32)
            for hh in range(GROUP):
                sinkv = jnp.where(hrow == hh, sink_ref[g * GROUP + hh], sinkv)
            sc = jnp.where(mask, _dot_t(qg, k_all[:, kcols]), -jnp.inf)
            m = jnp.maximum(jnp.max(sc, axis=-1, keepdims=True), sinkv)
            p = jnp.exp(sc - m)
            den = jnp.sum(p, axis=-1, keepdims=True) + jnp.exp(sinkv - m)
            o = _dot(p.astype(BF16), v_all[:, kcols]) * (1.0 / den)
            for hp in range(GROUP // 2):
                pair = jnp.concatenate(
                    [o[(2 * hp + e) * t_new:(2 * hp + e + 1) * t_new] for e in range(2)], axis=1)
                c0 = (g * GROUP + 2 * hp) * HEAD_DIM
                act_ref[rows, c0:c0 + 2 * HEAD_DIM] = (
                    pair * sg_ref[rows, c0:c0 + 2 * HEAD_DIM].astype(F32)).astype(BF16)
        return carry
    lax.fori_loop(0, n_seq, seq_body, 0)


def _attn_sample_out_kernel(act_ref, h_ref, gpost_ref, wout_ref, y_ref, *, tm):
    _attn_output(act_ref, h_ref, gpost_ref, wout_ref, y_ref, tm)


def _attn_layer_sample(h, cache_k, cache_v, sinks, gkv, gpre, gpost, w_kv, w_in, w_out, *, tm,
                       n_seq_tile):
    n_seq, t_new, d = h.shape
    kvw = w_kv.shape[1] // 2
    n_kv = kvw // HEAD_DIM
    w_buf = cache_k.shape[1]
    n_tok = n_seq * t_new
    h2 = h.reshape(n_tok, d)
    vec = lambda: _const_spec((1, d), 1)
    params = pltpu.CompilerParams(dimension_semantics=("arbitrary",), vmem_limit_bytes=VMEM_LIMIT)

    kv, q, sg = pl.pallas_call(
        functools.partial(_attn_sample_proj_kernel, tm=tm),
        grid=(n_tok // tm,),
        in_specs=[
            pl.BlockSpec((tm, d), lambda i: (i, 0)),
            vec(), vec(),
            _const_spec(w_kv.shape, 1),
            _const_spec(w_in.shape, 1),
        ],
        out_specs=[
            pl.BlockSpec((tm, 2 * kvw), lambda i: (i, 0)),
            pl.BlockSpec((tm, d), lambda i: (i, 0)),
            pl.BlockSpec((tm, d), lambda i: (i, 0)),
        ],
        out_shape=[
            jax.ShapeDtypeStruct((n_tok, 2 * kvw), F32),
            jax.ShapeDtypeStruct((n_tok, d), BF16),
            jax.ShapeDtypeStruct((n_tok, d), BF16),
        ],
        scratch_shapes=[pltpu.VMEM((tm, d), BF16), pltpu.VMEM((tm, d), BF16)],
        compiler_params=params,
        name="attn_sample_proj",
    )(h2, gkv, gpre, w_kv, w_in)

    rows = n_seq_tile * t_new
    act, ck, cv = pl.pallas_call(
        functools.partial(_attn_sample_core_kernel, n_seq=n_seq_tile, t_new=t_new, n_kv=n_kv),
        grid=(n_seq // n_seq_tile,),
        in_specs=[
            pl.BlockSpec(memory_space=pltpu.SMEM),
            pl.BlockSpec((rows, d), lambda i: (i, 0)),
            pl.BlockSpec((rows, d), lambda i: (i, 0)),
            pl.BlockSpec((rows, 2 * kvw), lambda i: (i, 0)),
            pl.BlockSpec((n_seq_tile, w_buf, kvw), lambda i: (i, 0, 0)),
            pl.BlockSpec((n_seq_tile, w_buf, kvw), lambda i: (i, 0, 0)),
        ],
        out_specs=[
            pl.BlockSpec((rows, d), lambda i: (i, 0)),
            pl.BlockSpec((n_seq_tile, w_buf, kvw), lambda i: (i, 0, 0)),
            pl.BlockSpec((n_seq_tile, w_buf, kvw), lambda i: (i, 0, 0)),
        ],
        out_shape=[
            jax.ShapeDtypeStruct((n_tok, d), BF16),
            jax.ShapeDtypeStruct((n_seq, w_buf, kvw), F32),
            jax.ShapeDtypeStruct((n_seq, w_buf, kvw), F32),
        ],
        compiler_params=params,
        name="attn_sample_core",
    )(sinks, q, sg, kv, cache_k, cache_v)

    y = pl.pallas_call(
        functools.partial(_attn_sample_out_kernel, tm=tm),
        grid=(n_tok // tm,),
        in_specs=[
            pl.BlockSpec((tm, d), lambda i: (i, 0)),
            pl.BlockSpec((tm, d), lambda i: (i, 0)),
            vec(),
            _const_spec(w_out.shape, 1),
        ],
        out_specs=pl.BlockSpec((tm, d), lambda i: (i, 0)),
        out_shape=jax.ShapeDtypeStruct((n_tok, d), F32),
        compiler_params=params,
        name="attn_sample_out",
    )(act, h2, gpost, w_out)
    return y.reshape(n_seq, t_new, d), ck, cv


def kernel(x_prompt, x_sample, state_conv, cache_k, cache_v, norm_pre, norm_post, w_in_a, conv_w, conv_b, ln_g, ln_b, w_out_a, kv_norm, w_kv, w_in_b, sinks, w_out_b):
    n_a = w_in_a.shape[0]
    assert n_a == 1 and w_in_b.shape[0] == 1 and norm_pre.shape[0] == 2
    d = x_prompt.shape[-1]
    n_seq, w_buf, n_kv, hd = cache_k.shape
    assert hd == HEAD_DIM and w_buf == WINDOW

    row = lambda v: v.reshape(1, -1)
    w_in_a_bf = w_in_a[0].astype(BF16)
    w_out_a_bf = w_out_a[0].astype(BF16)
    w_kv_bf = w_kv.astype(BF16)
    w_in_b_bf = w_in_b[0].astype(BF16)
    w_out_b_bf = w_out_b[0].astype(BF16)
    conv_args = (w_in_a_bf, conv_w[0], row(conv_b[0]), row(ln_g[0]), row(ln_b[0]), row(norm_pre[0]),
                 row(norm_post[0]), w_out_a_bf)
    attn_args = (sinks[0], row(kv_norm), row(norm_pre[1]), row(norm_post[1]), w_kv_bf, w_in_b_bf,
                 w_out_b_bf)

    h_p, st_p = _conv_layer_prompt(x_prompt, *conv_args, tm=512, cn=256)
    y_p, ck_p, cv_p = _attn_layer_prompt(h_p, *attn_args, tm=256)

    h_s, st_s = _conv_layer_sample(x_sample, state_conv[0], *conv_args, n_seq_tile=64, cn=256)
    y_s, ck_s, cv_s = _attn_layer_sample(
        h_s, cache_k.reshape(n_seq, w_buf, n_kv * hd), cache_v.reshape(n_seq, w_buf, n_kv * hd),
        *attn_args, tm=512, n_seq_tile=16)

    b = x_prompt.shape[0]
    return (y_p, y_s, st_p[None], ck_p.reshape(b, w_buf, n_kv, hd), cv_p.reshape(b, w_buf, n_kv, hd),
            st_s[None], ck_s.reshape(n_seq, w_buf, n_kv, hd), cv_s.reshape(n_seq, w_buf, n_kv, hd))
```

```python
import functools

import jax
import jax.numpy as jnp
from jax import lax
from jax.experimental import pallas as pl
from jax.experimental.pallas import tpu as pltpu

RMS_EPS = 1e-6
LN_EPS = 1e-5
HEAD_DIM = 64
GROUP = 8
WINDOW = 128
SUBLANES = 8
LANES = 128
HALO = 32
VMEM_LIMIT = 56 * 1024 * 1024
ROW_BLK = 64

BF16 = jnp.bfloat16
F32 = jnp.float32


def _rows(i, n):
    return pl.ds(pl.multiple_of(i * n, n), n)


def _rms_scale(x):
    return lax.rsqrt(jnp.mean(x * x, axis=-1, keepdims=True) + RMS_EPS)


def _dot(a, b):
    return jnp.dot(a, b, preferred_element_type=F32)


def _dot_t(a, b):
    return lax.dot_general(a, b, (((1,), (1,)), ((), ())), preferred_element_type=F32)


def _pre_norm_to_bf16(x_ref, g_ref, u_ref, tm):
    def body(r, c):
        rows = _rows(r, ROW_BLK)
        x = x_ref[rows, :]
        u_ref[rows, :] = (x * _rms_scale(x) * g_ref[...]).astype(BF16)
        return c
    lax.fori_loop(0, tm // ROW_BLK, body, 0)


def _glu_chunk(u_ref, wa_ref, wb_ref, wg_ref):
    u = u_ref[...]
    c = _dot(u, wa_ref[0]) * jax.nn.sigmoid(_dot(u, wb_ref[0]))
    sg = jax.nn.silu(_dot(u, wg_ref[0])).astype(BF16)
    return c, sg


def _broadcast_taps(cw_ref, wbc_ref, taps):
    for k in range(taps):
        wbc_ref[k] = jnp.broadcast_to(cw_ref[k:k + 1, :], wbc_ref.shape[1:])


def _conv_finalize(j_chunks, tm, cn, y_ref, sg_ref, act_ref, x_ref, h_ref, cb_ref, lng_ref, lnb_ref,
                   gpost_ref, wout_ref):
    d = j_chunks * cn

    def ln_body(r, c):
        rows = _rows(r, ROW_BLK)
        ys = [y_ref[jj, rows, :] + cb_ref[:, jj * cn:(jj + 1) * cn] for jj in range(j_chunks)]
        mu = sum(jnp.sum(y, axis=-1, keepdims=True) for y in ys) * (1.0 / d)
        yc = [y - mu for y in ys]
        var = sum(jnp.sum(y * y, axis=-1, keepdims=True) for y in yc) * (1.0 / d)
        rstd = lax.rsqrt(var + LN_EPS)
        for jj in range(j_chunks):
            cols = slice(jj * cn, (jj + 1) * cn)
            t = jax.nn.silu(yc[jj] * rstd * lng_ref[:, cols] + lnb_ref[:, cols])
            act_ref[rows, cols] = (t * sg_ref[jj, rows, :].astype(F32)).astype(BF16)
        return c
    lax.fori_loop(0, tm // ROW_BLK, ln_body, 0)

    h_ref[...] = _dot(act_ref[...], wout_ref[...])

    def post_body(r, c):
        rows = _rows(r, ROW_BLK)
        o = h_ref[rows, :]
        h_ref[rows, :] = x_ref[rows, :] + o * _rms_scale(o) * gpost_ref[...]
        return c
    lax.fori_loop(0, tm // ROW_BLK, post_body, 0)


def _conv_chunk(cext_ref, wbc_ref, y_ref, jc, *, tm, cn, taps, grp):
    lbs = cn // LANES
    assert lbs >= 2, "the row-interleaved layout needs at least two lane blocks per chunk"
    lead = HALO - (taps - 1)
    for g in range(tm // (grp * SUBLANES)):
        base = g * grp * SUBLANES
        for lb in range(lbs):
            ls = slice(lb * LANES, (lb + 1) * LANES)
            acc = [None] * grp
            for k in range(taps):
                wv = wbc_ref[jc, k, :, ls]
                for gi in range(grp):
                    r0 = base + SUBLANES * gi + k + lead
                    t = wv * cext_ref[jc, pl.ds(lbs * r0 + lb, SUBLANES, stride=lbs), :]
                    acc[gi] = t if acc[gi] is None else acc[gi] + t
            for gi in range(grp):
                y_ref[jc, base + SUBLANES * gi:base + SUBLANES * (gi + 1), ls] = acc[gi]


def _conv_prompt_kernel(x_ref, wa_ref, wb_ref, wg_ref, cw_ref, cb_ref, lng_ref, lnb_ref, gpre_ref,
                        gpost_ref, wout_ref, h_ref, st_ref,
                        u_ref, cext_ref, wbc_ref, y_ref, sg_ref, act_ref, *, tm, cn, taps, grp):
    i = pl.program_id(1)
    j = pl.program_id(2)
    n_i = pl.num_programs(1)
    n_j = pl.num_programs(2)
    n_chunks = cext_ref.shape[0]
    lbs = cn // LANES

    @pl.when(j == 0)
    def _():
        _pre_norm_to_bf16(x_ref, gpre_ref, u_ref, tm)

    @pl.when(i == 0)
    def _():
        cext_ref[j, 0:lbs * HALO, :] = jnp.zeros((lbs * HALO, LANES), F32)

    @pl.when(i > 0)
    def _():
        cext_ref[j, 0:lbs * HALO, :] = cext_ref[j, lbs * tm:lbs * (tm + HALO), :]

    @pl.when((pl.program_id(0) == 0) & (i == 0) & (j == 0))
    def _():
        for jj in range(n_chunks):
            for k in range(taps):
                wbc_ref[jj, k] = jnp.broadcast_to(cw_ref[k:k + 1, jj * cn:(jj + 1) * cn], (SUBLANES, cn))

    c, sg = _glu_chunk(u_ref, wa_ref, wb_ref, wg_ref)
    for lb in range(lbs):
        cext_ref[j, pl.ds(lbs * HALO + lb, tm, stride=lbs), :] = c[:, lb * LANES:(lb + 1) * LANES]
    sg_ref[j] = sg
    _conv_chunk(cext_ref, wbc_ref, y_ref, j, tm=tm, cn=cn, taps=taps, grp=grp)

    @pl.when(j == n_j - 1)
    def _():
        _conv_finalize(cext_ref.shape[0], tm, cn, y_ref, sg_ref, act_ref, x_ref, h_ref, cb_ref,
                       lng_ref, lnb_ref, gpost_ref, wout_ref)

        @pl.when(i == n_i - 1)
        def _():
            tail = lbs * (HALO + tm - (taps - 1))
            for jj in range(n_chunks):
                for lb in range(lbs):
                    c0 = jj * cn + lb * LANES
                    st_ref[:, c0:c0 + LANES] = cext_ref[jj, pl.ds(tail + lb, taps - 1, stride=lbs), :]


def _conv_sample_kernel(x_ref, st_in_ref, wa_ref, wb_ref, wg_ref, cw_ref, cb_ref, lng_ref, lnb_ref,
                        gpre_ref, gpost_ref, wout_ref, h_ref, st_ref,
                        u_ref, c_ref, full_ref, wbc_ref, y_ref, sg_ref, act_ref, *, tm, cn, taps, t_new):
    j = pl.program_id(1)
    n_j = pl.num_programs(1)
    n_seq = tm // t_new
    hist = taps - 1

    @pl.when(j == 0)
    def _():
        _pre_norm_to_bf16(x_ref, gpre_ref, u_ref, tm)

    c, sg = _glu_chunk(u_ref, wa_ref, wb_ref, wg_ref)
    c_ref[...] = c
    sg_ref[j] = sg
    _broadcast_taps(cw_ref, wbc_ref, taps)

    lbs = cn // LANES
    n_par = full_ref.shape[0]

    def seq_body(sb, carry):
        for q in range(n_par):
            s = sb * n_par + q
            rows = pl.ds(pl.multiple_of(s * t_new, t_new), t_new)
            for lb in range(lbs):
                ls = slice(lb * LANES, (lb + 1) * LANES)
                full_ref[q, pl.ds(lb, hist, stride=lbs), :] = st_in_ref[s, :, ls]
                full_ref[q, pl.ds(lbs * hist + lb, t_new, stride=lbs), :] = c_ref[rows, ls]
        for q in range(n_par):
            s = sb * n_par + q
            rows = pl.ds(pl.multiple_of(s * t_new, t_new), t_new)
            for lb in range(lbs):
                ls = slice(lb * LANES, (lb + 1) * LANES)
                acc = None
                for k in range(taps):
                    t = wbc_ref[k, :, ls] * full_ref[q, pl.ds(lbs * k + lb, t_new, stride=lbs), :]
                    acc = t if acc is None else acc + t
                y_ref[j, rows, ls] = acc
                st_ref[s, :, ls] = full_ref[q, pl.ds(lbs * t_new + lb, hist, stride=lbs), :]
        return carry
    lax.fori_loop(0, n_seq // n_par, seq_body, 0)

    @pl.when(j == n_j - 1)
    def _():
        _conv_finalize(y_ref.shape[0], tm, cn, y_ref, sg_ref, act_ref, x_ref, h_ref, cb_ref,
                       lng_ref, lnb_ref, gpost_ref, wout_ref)


def _const_spec(shape, n_grid):
    zeros = (0,) * len(shape)
    return pl.BlockSpec(shape, lambda *_: zeros, pipeline_mode=pl.Buffered(1))


def _conv_layer_prompt(x, w_in, cw, cb, lng, lnb, gpre, gpost, w_out, *, tm, cn):
    b, t, d = x.shape
    taps = cw.shape[0]
    n_j = d // cn
    grp = 4
    kern = functools.partial(_conv_prompt_kernel, tm=tm, cn=cn, taps=taps, grp=grp)
    vec = lambda: _const_spec((1, d), 3)
    return pl.pallas_call(
        kern,
        grid=(b, t // tm, n_j),
        in_specs=[
            pl.BlockSpec((None, tm, d), lambda bb, i, j: (bb, i, 0)),
            pl.BlockSpec((1, d, cn), lambda bb, i, j: (0, 0, j)),
            pl.BlockSpec((1, d, cn), lambda bb, i, j: (0, 0, n_j + j)),
            pl.BlockSpec((1, d, cn), lambda bb, i, j: (0, 0, 2 * n_j + j)),
            _const_spec((taps, d), 3),
            vec(), vec(), vec(), vec(), vec(),
            _const_spec((d, d), 3),
        ],
        out_specs=[
            pl.BlockSpec((None, tm, d), lambda bb, i, j: (bb, i, 0)),
            pl.BlockSpec((None, taps - 1, d), lambda bb, i, j: (bb, 0, 0)),
        ],
        out_shape=[
            jax.ShapeDtypeStruct((b, t, d), F32),
            jax.ShapeDtypeStruct((b, taps - 1, d), F32),
        ],
        scratch_shapes=[
            pltpu.VMEM((tm, d), BF16),
            pltpu.VMEM((n_j, (cn // LANES) * (HALO + tm), LANES), F32),
            pltpu.VMEM((n_j, taps, SUBLANES, cn), F32),
            pltpu.VMEM((n_j, tm, cn), F32),
            pltpu.VMEM((n_j, tm, cn), BF16),
            pltpu.VMEM((tm, d), BF16),
        ],
        compiler_params=pltpu.CompilerParams(
            dimension_semantics=("arbitrary", "arbitrary", "arbitrary"),
            vmem_limit_bytes=VMEM_LIMIT),
        name="conv_layer_prompt",
    )(x, w_in, w_in, w_in, cw, cb, lng, lnb, gpre, gpost, w_out)


def _conv_layer_sample(x, state, w_in, cw, cb, lng, lnb, gpre, gpost, w_out, *, n_seq_tile, cn):
    n_seq, t_new, d = x.shape
    taps = cw.shape[0]
    n_j = d // cn
    tm = n_seq_tile * t_new
    x2 = x.reshape(n_seq * t_new, d)
    kern = functools.partial(_conv_sample_kernel, tm=tm, cn=cn, taps=taps, t_new=t_new)
    vec = lambda: _const_spec((1, d), 2)
    h, st = pl.pallas_call(
        kern,
        grid=(n_seq // n_seq_tile, n_j),
        in_specs=[
            pl.BlockSpec((tm, d), lambda i, j: (i, 0)),
            pl.BlockSpec((n_seq_tile, taps - 1, cn), lambda i, j: (i, 0, j)),
            pl.BlockSpec((1, d, cn), lambda i, j: (0, 0, j)),
            pl.BlockSpec((1, d, cn), lambda i, j: (0, 0, n_j + j)),
            pl.BlockSpec((1, d, cn), lambda i, j: (0, 0, 2 * n_j + j)),
            pl.BlockSpec((taps, cn), lambda i, j: (0, j)),
            vec(), vec(), vec(), vec(), vec(),
            _const_spec((d, d), 2),
        ],
        out_specs=[
            pl.BlockSpec((tm, d), lambda i, j: (i, 0)),
            pl.BlockSpec((n_seq_tile, taps - 1, cn), lambda i, j: (i, 0, j)),
        ],
        out_shape=[
            jax.ShapeDtypeStruct((n_seq * t_new, d), F32),
            jax.ShapeDtypeStruct((n_seq, taps - 1, d), F32),
        ],
        scratch_shapes=[
            pltpu.VMEM((tm, d), BF16),
            pltpu.VMEM((tm, cn), F32),
            pltpu.VMEM((4, (cn // LANES) * (HALO + t_new), LANES), F32),
            pltpu.VMEM((taps, SUBLANES, cn), F32),
            pltpu.VMEM((n_j, tm, cn), F32),
            pltpu.VMEM((n_j, tm, cn), BF16),
            pltpu.VMEM((tm, d), BF16),
        ],
        compiler_params=pltpu.CompilerParams(
            dimension_semantics=("arbitrary", "arbitrary"),
            vmem_limit_bytes=VMEM_LIMIT),
        name="conv_layer_sample",
    )(x2, state, w_in, w_in, w_in, cw, cb, lng, lnb, gpre, gpost, w_out)
    return h.reshape(n_seq, t_new, d), st


def _attn_projections(h_ref, gkv_ref, gpre_ref, wkv_ref, win_ref, ukv_ref, u_ref, q_ref, sg_ref, tm,
                      n_chunk):
    d = h_ref.shape[-1]

    def body(r, c):
        rows = _rows(r, ROW_BLK)
        x = h_ref[rows, :]
        xn = x * _rms_scale(x)
        ukv_ref[rows, :] = (xn * gkv_ref[...]).astype(BF16)
        u_ref[rows, :] = (xn * gpre_ref[...]).astype(BF16)
        return c
    lax.fori_loop(0, tm // ROW_BLK, body, 0)

    kv = _dot(ukv_ref[...], wkv_ref[...])
    u = u_ref[...]
    cw = d // n_chunk
    scale = HEAD_DIM ** -0.5
    for n in range(n_chunk):
        cols = slice(n * cw, (n + 1) * cw)
        q_ref[:, cols] = (_dot(u, win_ref[:, cols]) * scale).astype(BF16)
    for n in range(n_chunk):
        cols = slice(n * cw, (n + 1) * cw)
        sg_ref[:, cols] = jax.nn.silu(_dot(u, win_ref[:, d + n * cw:d + (n + 1) * cw])).astype(BF16)
    return kv


def _attn_output(act_ref, h_ref, gpost_ref, wout_ref, y_ref, tm):
    y_ref[...] = _dot(act_ref[...], wout_ref[...])

    def post_body(r, c):
        rows = _rows(r, ROW_BLK)
        o = y_ref[rows, :]
        y_ref[rows, :] = h_ref[rows, :] + o * _rms_scale(o) * gpost_ref[...]
        return c
    lax.fori_loop(0, tm // ROW_BLK, post_body, 0)


def _attn_prompt_kernel(sink_ref, h_ref, gkv_ref, gpre_ref, gpost_ref, wkv_ref, win_ref, wout_ref,
                        y_ref, ck_ref, cv_ref,
                        ukv_ref, u_ref, q_ref, sg_ref, kbuf_ref, vbuf_ref, act_ref, *, tm, n_kv):
    i = pl.program_id(1)
    n_i = pl.num_programs(1)
    kvw = n_kv * HEAD_DIM
    blk = WINDOW

    kv = _attn_projections(h_ref, gkv_ref, gpre_ref, wkv_ref, win_ref, ukv_ref, u_ref, q_ref, sg_ref,
                           tm, 4)

    @pl.when(i == 0)
    def _():
        kbuf_ref[0:blk, :] = jnp.zeros((blk, kvw), BF16)
        vbuf_ref[0:blk, :] = jnp.zeros((blk, kvw), BF16)

    @pl.when(i > 0)
    def _():
        kbuf_ref[0:blk, :] = kbuf_ref[tm:tm + blk, :]
        vbuf_ref[0:blk, :] = vbuf_ref[tm:tm + blk, :]

    kbuf_ref[blk:blk + tm, :] = kv[:, :kvw].astype(BF16)
    vbuf_ref[blk:blk + tm, :] = kv[:, kvw:].astype(BF16)

    @pl.when(i == n_i - 1)
    def _():
        ck_ref[...] = kv[tm - blk:, :kvw]
        cv_ref[...] = kv[tm - blk:, kvw:]

    qi = lax.broadcasted_iota(jnp.int32, (blk, 2 * blk), 0)
    kj = lax.broadcasted_iota(jnp.int32, (blk, 2 * blk), 1)
    own = (kj >= blk) & (kj - blk <= qi)
    prev = (kj < blk) & (kj > qi)

    def blk_body(bi, carry):
        rows = _rows(bi, blk)
        first_key = jnp.where((i > 0) | (bi > 0), 0, blk)
        mask = own | (prev & (kj >= first_key))
        for g in range(n_kv):
            kcols = slice(g * HEAD_DIM, (g + 1) * HEAD_DIM)
            k = kbuf_ref[pl.ds(pl.multiple_of(bi * blk, blk), 2 * blk), kcols]
            v = vbuf_ref[pl.ds(pl.multiple_of(bi * blk, blk), 2 * blk), kcols]
            qg = jnp.concatenate(
                [q_ref[rows, (g * GROUP + hh) * HEAD_DIM:(g * GROUP + hh + 1) * HEAD_DIM]
                 for hh in range(GROUP)], axis=0)
            s = _dot_t(qg, k)
            ps, rden = [], []
            for hh in range(GROUP):
                sink = sink_ref[g * GROUP + hh]
                sh = jnp.where(mask, s[hh * blk:(hh + 1) * blk], -jnp.inf)
                m = jnp.maximum(jnp.max(sh, axis=-1, keepdims=True), sink)
                p = jnp.exp(sh - m)
                den = jnp.sum(p, axis=-1, keepdims=True) + jnp.exp(sink - m)
                ps.append(p.astype(BF16))
                rden.append(1.0 / den)
            o = _dot(jnp.concatenate(ps, axis=0), v)
            for hp in range(GROUP // 2):
                pair = jnp.concatenate(
                    [o[(2 * hp + e) * blk:(2 * hp + e + 1) * blk] * rden[2 * hp + e] for e in range(2)],
                    axis=1)
                c0 = (g * GROUP + 2 * hp) * HEAD_DIM
                act_ref[rows, c0:c0 + 2 * HEAD_DIM] = (
                    pair * sg_ref[rows, c0:c0 + 2 * HEAD_DIM].astype(F32)).astype(BF16)
        return carry
    lax.fori_loop(0, tm // blk, blk_body, 0)

    _attn_output(act_ref, h_ref, gpost_ref, wout_ref, y_ref, tm)


def _attn_layer_prompt(h, sinks, gkv, gpre, gpost, w_kv, w_in, w_out, *, tm):
    b, t, d = h.shape
    kvw = w_kv.shape[1] // 2
    n_kv = kvw // HEAD_DIM
    kern = functools.partial(_attn_prompt_kernel, tm=tm, n_kv=n_kv)
    vec = lambda: _const_spec((1, d), 2)
    return pl.pallas_call(
        kern,
        grid=(b, t // tm),
        in_specs=[
            pl.BlockSpec(memory_space=pltpu.SMEM),
            pl.BlockSpec((None, tm, d), lambda bb, i: (bb, i, 0)),
            vec(), vec(), vec(),
            _const_spec(w_kv.shape, 2),
            _const_spec(w_in.shape, 2),
            _const_spec(w_out.shape, 2),
        ],
        out_specs=[
            pl.BlockSpec((None, tm, d), lambda bb, i: (bb, i, 0)),
            pl.BlockSpec((None, WINDOW, kvw), lambda bb, i: (bb, 0, 0)),
            pl.BlockSpec((None, WINDOW, kvw), lambda bb, i: (bb, 0, 0)),
        ],
        out_shape=[
            jax.ShapeDtypeStruct((b, t, d), F32),
            jax.ShapeDtypeStruct((b, WINDOW, kvw), F32),
            jax.ShapeDtypeStruct((b, WINDOW, kvw), F32),
        ],
        scratch_shapes=[
            pltpu.VMEM((tm, d), BF16),
            pltpu.VMEM((tm, d), BF16),
            pltpu.VMEM((tm, d), BF16),
            pltpu.VMEM((tm, d), BF16),
            pltpu.VMEM((WINDOW + tm, kvw), BF16),
            pltpu.VMEM((WINDOW + tm, kvw), BF16),
            pltpu.VMEM((tm, d), BF16),
        ],
        compiler_params=pltpu.CompilerParams(
            dimension_semantics=("arbitrary", "arbitrary"),
            vmem_limit_bytes=VMEM_LIMIT),
        name="attn_layer_prompt",
    )(sinks, h, gkv, gpre, gpost, w_kv, w_in, w_out)


def _attn_sample_proj_kernel(h_ref, gkv_ref, gpre_ref, wkv_ref, win_ref, kv_ref, q_ref, sg_ref,
                             ukv_ref, u_ref, *, tm):
    kv_ref[...] = _attn_projections(h_ref, gkv_ref, gpre_ref, wkv_ref, win_ref, ukv_ref, u_ref, q_ref,
                                    sg_ref, tm, 4)


def _attn_sample_core_kernel(sink_ref, q_ref, sg_ref, kv_ref, ck_in_ref, cv_in_ref,
                             act_ref, ck_ref, cv_ref, *, n_seq, t_new, n_kv):
    kvw = n_kv * HEAD_DIM
    w_buf = ck_in_ref.shape[1]
    rows_g = GROUP * t_new
    n_keys = w_buf + 2 * t_new
    tq = lax.broadcasted_iota(jnp.int32, (rows_g, n_keys), 0) % t_new
    kj = lax.broadcasted_iota(jnp.int32, (rows_g, n_keys), 1)
    mask = ((kj < w_buf) & (kj > tq)) | ((kj >= w_buf) & (kj - w_buf <= tq))
    hrow = lax.broadcasted_iota(jnp.int32, (rows_g, 1), 0) // t_new

    def seq_body(s, carry):
        rows = pl.ds(pl.multiple_of(s * t_new, t_new), t_new)
        kc = ck_in_ref[s]
        vc = cv_in_ref[s]
        kvn = kv_ref[rows, :]
        ck_ref[s, 0:w_buf - t_new, :] = kc[t_new:]
        cv_ref[s, 0:w_buf - t_new, :] = vc[t_new:]
        ck_ref[s, w_buf - t_new:w_buf, :] = kvn[:, :kvw]
        cv_ref[s, w_buf - t_new:w_buf, :] = kvn[:, kvw:]
        pad = jnp.zeros((t_new, kvw), F32)
        k_all = jnp.concatenate([kc, kvn[:, :kvw], pad], axis=0).astype(BF16)
        v_all = jnp.concatenate([vc, kvn[:, kvw:], pad], axis=0).astype(BF16)
        for g in range(n_kv):
            kcols = slice(g * HEAD_DIM, (g + 1) * HEAD_DIM)
            qg = jnp.concatenate(
                [q_ref[rows, (g * GROUP + hh) * HEAD_DIM:(g * GROUP + hh + 1) * HEAD_DIM]
                 for hh in range(GROUP)], axis=0)
            sinkv = jnp.zeros((rows_g, 1), F32)
            for hh in range(GROUP):
                sinkv = jnp.where(hrow == hh, sink_ref[g * GROUP + hh], sinkv)
            sc = jnp.where(mask, _dot_t(qg, k_all[:, kcols]), -jnp.inf)
            m = jnp.maximum(jnp.max(sc, axis=-1, keepdims=True), sinkv)
            p = jnp.exp(sc - m)
            den = jnp.sum(p, axis=-1, keepdims=True) + jnp.exp(sinkv - m)
            o = _dot(p.astype(BF16), v_all[:, kcols]) * (1.0 / den)
            for hp in range(GROUP // 2):
                pair = jnp.concatenate(
                    [o[(2 * hp + e) * t_new:(2 * hp + e + 1) * t_new] for e in range(2)], axis=1)
                c0 = (g * GROUP + 2 * hp) * HEAD_DIM
                act_ref[rows, c0:c0 + 2 * HEAD_DIM] = (
                    pair * sg_ref[rows, c0:c0 + 2 * HEAD_DIM].astype(F32)).astype(BF16)
        return carry
    lax.fori_loop(0, n_seq, seq_body, 0)


def _attn_sample_out_kernel(act_ref, h_ref, gpost_ref, wout_ref, y_ref, *, tm):
    _attn_output(act_ref, h_ref, gpost_ref, wout_ref, y_ref, tm)


def _attn_layer_sample(h, cache_k, cache_v, sinks, gkv, gpre, gpost, w_kv, w_in, w_out, *, tm,
                       n_seq_tile):
    n_seq, t_new, d = h.shape
    kvw = w_kv.shape[1] // 2
    n_kv = kvw // HEAD_DIM
    w_buf = cache_k.shape[1]
    n_tok = n_seq * t_new
    h2 = h.reshape(n_tok, d)
    vec = lambda: _const_spec((1, d), 1)
    params = pltpu.CompilerParams(dimension_semantics=("arbitrary",), vmem_limit_bytes=VMEM_LIMIT)

    kv, q, sg = pl.pallas_call(
        functools.partial(_attn_sample_proj_kernel, tm=tm),
        grid=(n_tok // tm,),
        in_specs=[
            pl.BlockSpec((tm, d), lambda i: (i, 0)),
            vec(), vec(),
            _const_spec(w_kv.shape, 1),
            _const_spec(w_in.shape, 1),
        ],
        out_specs=[
            pl.BlockSpec((tm, 2 * kvw), lambda i: (i, 0)),
            pl.BlockSpec((tm, d), lambda i: (i, 0)),
            pl.BlockSpec((tm, d), lambda i: (i, 0)),
        ],
        out_shape=[
            jax.ShapeDtypeStruct((n_tok, 2 * kvw), F32),
            jax.ShapeDtypeStruct((n_tok, d), BF16),
            jax.ShapeDtypeStruct((n_tok, d), BF16),
        ],
        scratch_shapes=[pltpu.VMEM((tm, d), BF16), pltpu.VMEM((tm, d), BF16)],
        compiler_params=params,
        name="attn_sample_proj",
    )(h2, gkv, gpre, w_kv, w_in)

    rows = n_seq_tile * t_new
    act, ck, cv = pl.pallas_call(
        functools.partial(_attn_sample_core_kernel, n_seq=n_seq_tile, t_new=t_new, n_kv=n_kv),
        grid=(n_seq // n_seq_tile,),
        in_specs=[
            pl.BlockSpec(memory_space=pltpu.SMEM),
            pl.BlockSpec((rows, d), lambda i: (i, 0)),
            pl.BlockSpec((rows, d), lambda i: (i, 0)),
            pl.BlockSpec((rows, 2 * kvw), lambda i: (i, 0)),
            pl.BlockSpec((n_seq_tile, w_buf, kvw), lambda i: (i, 0, 0)),
            pl.BlockSpec((n_seq_tile, w_buf, kvw), lambda i: (i, 0, 0)),
        ],
        out_specs=[
            pl.BlockSpec((rows, d), lambda i: (i, 0)),
            pl.BlockSpec((n_seq_tile, w_buf, kvw), lambda i: (i, 0, 0)),
            pl.BlockSpec((n_seq_tile, w_buf, kvw), lambda i: (i, 0, 0)),
        ],
        out_shape=[
            jax.ShapeDtypeStruct((n_tok, d), BF16),
            jax.ShapeDtypeStruct((n_seq, w_buf, kvw), F32),
            jax.ShapeDtypeStruct((n_seq, w_buf, kvw), F32),
        ],
        compiler_params=params,
        name="attn_sample_core",
    )(sinks, q, sg, kv, cache_k, cache_v)

    y = pl.pallas_call(
        functools.partial(_attn_sample_out_kernel, tm=tm),
        grid=(n_tok // tm,),
        in_specs=[
            pl.BlockSpec((tm, d), lambda i: (i, 0)),
            pl.BlockSpec((tm, d), lambda i: (i, 0)),
            vec(),
            _const_spec(w_out.shape, 1),
        ],
        out_specs=pl.BlockSpec((tm, d), lambda i: (i, 0)),
        out_shape=jax.ShapeDtypeStruct((n_tok, d), F32),
        compiler_params=params,
        name="attn_sample_out",
    )(act, h2, gpost, w_out)
    return y.reshape(n_seq, t_new, d), ck, cv


def kernel(x_prompt, x_sample, state_conv, cache_k, cache_v, norm_pre, norm_post, w_in_a, conv_w, conv_b, ln_g, ln_b, w_out_a, kv_norm, w_kv, w_in_b, sinks, w_out_b):
    n_a = w_in_a.shape[0]
    assert n_a == 1 and w_in_b.shape[0] == 1 and norm_pre.shape[0] == 2
    d = x_prompt.shape[-1]
    n_seq, w_buf, n_kv, hd = cache_k.shape
    assert hd == HEAD_DIM and w_buf == WINDOW

    row = lambda v: v.reshape(1, -1)
    w_in_a_bf = w_in_a[0].astype(BF16)
    w_out_a_bf = w_out_a[0].astype(BF16)
    w_kv_bf = w_kv.astype(BF16)
    w_in_b_bf = w_in_b[0].astype(BF16)
    w_out_b_bf = w_out_b[0].astype(BF16)
    conv_args = (w_in_a_bf[None], conv_w[0], row(conv_b[0]), row(ln_g[0]), row(ln_b[0]), row(norm_pre[0]),
                 row(norm_post[0]), w_out_a_bf)
    attn_args = (sinks[0], row(kv_norm), row(norm_pre[1]), row(norm_post[1]), w_kv_bf, w_in_b_bf,
                 w_out_b_bf)

    h_p, st_p = _conv_layer_prompt(x_prompt, *conv_args, tm=512, cn=256)
    y_p, ck_p, cv_p = _attn_layer_prompt(h_p, *attn_args, tm=256)

    h_s, st_s = _conv_layer_sample(x_sample, state_conv[0], *conv_args, n_seq_tile=64, cn=256)
    y_s, ck_s, cv_s = _attn_layer_sample(
        h_s, cache_k.reshape(n_seq, w_buf, n_kv * hd), cache_v.reshape(n_seq, w_buf, n_kv * hd),
        *attn_args, tm=512, n_seq_tile=16)

    b = x_prompt.shape[0]
    return (y_p, y_s, st_p[None], ck_p.reshape(b, w_buf, n_kv, hd), cv_p.reshape(b, w_buf, n_kv, hd),
            st_s[None], ck_s.reshape(n_seq, w_buf, n_kv, hd), cv_s.reshape(n_seq, w_buf, n_kv, hd))
```

```python
import functools

import jax
import jax.numpy as jnp
from jax import lax
from jax.experimental import pallas as pl
from jax.experimental.pallas import tpu as pltpu

RMS_EPS = 1e-6
LN_EPS = 1e-5
HEAD_DIM = 64
GROUP = 8
WINDOW = 128
SUBLANES = 8
LANES = 128
LOG2E = 1.4426950408889634
HALO = 32
VMEM_LIMIT = 56 * 1024 * 1024
ROW_BLK = 64

BF16 = jnp.bfloat16
F32 = jnp.float32


def _rows(i, n):
    return pl.ds(pl.multiple_of(i * n, n), n)


def _rms_scale(x):
    return lax.rsqrt(jnp.mean(x * x, axis=-1, keepdims=True) + RMS_EPS)


def _dot(a, b):
    return jnp.dot(a, b, preferred_element_type=F32)


def _dot_t(a, b):
    return lax.dot_general(a, b, (((1,), (1,)), ((), ())), preferred_element_type=F32)


def _pre_norm_to_bf16(x_ref, g_ref, u_ref, tm):
    def body(r, c):
        rows = _rows(r, ROW_BLK)
        x = x_ref[rows, :]
        u_ref[rows, :] = (x * _rms_scale(x) * g_ref[...]).astype(BF16)
        return c
    lax.fori_loop(0, tm // ROW_BLK, body, 0)


def _glu_chunk(u_ref, wa_ref, wb_ref, wg_ref):
    u = u_ref[...]
    c = _dot(u, wa_ref[0]) * jax.nn.sigmoid(_dot(u, wb_ref[0]))
    sg = jax.nn.silu(_dot(u, wg_ref[0])).astype(BF16)
    return c, sg


def _broadcast_taps(cw_ref, wbc_ref, taps):
    for k in range(taps):
        wbc_ref[k] = jnp.broadcast_to(cw_ref[k:k + 1, :], wbc_ref.shape[1:])


def _conv_finalize(j_chunks, tm, cn, y_ref, sg_ref, act_ref, x_ref, h_ref, cb_ref, lng_ref, lnb_ref,
                   gpost_ref, wout_ref):
    d = j_chunks * cn

    def ln_body(r, c):
        rows = _rows(r, ROW_BLK)
        ys = [y_ref[jj, rows, :] + cb_ref[:, jj * cn:(jj + 1) * cn] for jj in range(j_chunks)]
        mu = sum(jnp.sum(y, axis=-1, keepdims=True) for y in ys) * (1.0 / d)
        yc = [y - mu for y in ys]
        var = sum(jnp.sum(y * y, axis=-1, keepdims=True) for y in yc) * (1.0 / d)
        rstd = lax.rsqrt(var + LN_EPS)
        for jj in range(j_chunks):
            cols = slice(jj * cn, (jj + 1) * cn)
            t = jax.nn.silu(yc[jj] * rstd * lng_ref[:, cols] + lnb_ref[:, cols])
            act_ref[rows, cols] = (t * sg_ref[jj, rows, :].astype(F32)).astype(BF16)
        return c
    lax.fori_loop(0, tm // ROW_BLK, ln_body, 0)

    h_ref[...] = _dot(act_ref[...], wout_ref[...])

    def post_body(r, c):
        rows = _rows(r, ROW_BLK)
        o = h_ref[rows, :]
        h_ref[rows, :] = x_ref[rows, :] + o * _rms_scale(o) * gpost_ref[...]
        return c
    lax.fori_loop(0, tm // ROW_BLK, post_body, 0)


def _conv_chunk(cext_ref, wbc_ref, y_ref, jc, *, tm, cn, taps, grp):
    lbs = cn // LANES
    assert lbs >= 2, "the row-interleaved layout needs at least two lane blocks per chunk"
    lead = HALO - (taps - 1)
    for g in range(tm // (grp * SUBLANES)):
        base = g * grp * SUBLANES
        for lb in range(lbs):
            ls = slice(lb * LANES, (lb + 1) * LANES)
            acc = [None] * grp
            for k in range(taps):
                wv = wbc_ref[jc, k, :, ls]
                for gi in range(grp):
                    r0 = base + SUBLANES * gi + k + lead
                    t = wv * cext_ref[jc, pl.ds(lbs * r0 + lb, SUBLANES, stride=lbs), :]
                    acc[gi] = t if acc[gi] is None else acc[gi] + t
            for gi in range(grp):
                y_ref[jc, base + SUBLANES * gi:base + SUBLANES * (gi + 1), ls] = acc[gi]


def _conv_prompt_kernel(x_ref, wa_ref, wb_ref, wg_ref, cw_ref, cb_ref, lng_ref, lnb_ref, gpre_ref,
                        gpost_ref, wout_ref, h_ref, st_ref,
                        u_ref, cext_ref, wbc_ref, y_ref, sg_ref, act_ref, *, tm, cn, taps, grp):
    i = pl.program_id(1)
    j = pl.program_id(2)
    n_i = pl.num_programs(1)
    n_j = pl.num_programs(2)
    n_chunks = cext_ref.shape[0]
    lbs = cn // LANES

    @pl.when(j == 0)
    def _():
        _pre_norm_to_bf16(x_ref, gpre_ref, u_ref, tm)

    @pl.when(i == 0)
    def _():
        cext_ref[j, 0:lbs * HALO, :] = jnp.zeros((lbs * HALO, LANES), F32)

    @pl.when(i > 0)
    def _():
        cext_ref[j, 0:lbs * HALO, :] = cext_ref[j, lbs * tm:lbs * (tm + HALO), :]

    @pl.when((pl.program_id(0) == 0) & (i == 0) & (j == 0))
    def _():
        for jj in range(n_chunks):
            for k in range(taps):
                wbc_ref[jj, k] = jnp.broadcast_to(cw_ref[k:k + 1, jj * cn:(jj + 1) * cn], (SUBLANES, cn))

    c, sg = _glu_chunk(u_ref, wa_ref, wb_ref, wg_ref)
    for lb in range(lbs):
        cext_ref[j, pl.ds(lbs * HALO + lb, tm, stride=lbs), :] = c[:, lb * LANES:(lb + 1) * LANES]
    sg_ref[j] = sg
    _conv_chunk(cext_ref, wbc_ref, y_ref, j, tm=tm, cn=cn, taps=taps, grp=grp)

    @pl.when(j == n_j - 1)
    def _():
        _conv_finalize(cext_ref.shape[0], tm, cn, y_ref, sg_ref, act_ref, x_ref, h_ref, cb_ref,
                       lng_ref, lnb_ref, gpost_ref, wout_ref)

        @pl.when(i == n_i - 1)
        def _():
            tail = lbs * (HALO + tm - (taps - 1))
            for jj in range(n_chunks):
                for lb in range(lbs):
                    c0 = jj * cn + lb * LANES
                    st_ref[:, c0:c0 + LANES] = cext_ref[jj, pl.ds(tail + lb, taps - 1, stride=lbs), :]


def _conv_sample_kernel(x_ref, st_in_ref, wa_ref, wb_ref, wg_ref, cw_ref, cb_ref, lng_ref, lnb_ref,
                        gpre_ref, gpost_ref, wout_ref, h_ref, st_ref,
                        u_ref, c_ref, full_ref, wbc_ref, y_ref, sg_ref, act_ref, *, tm, cn, taps, t_new):
    j = pl.program_id(1)
    n_j = pl.num_programs(1)
    n_seq = tm // t_new
    hist = taps - 1

    @pl.when(j == 0)
    def _():
        _pre_norm_to_bf16(x_ref, gpre_ref, u_ref, tm)

    c, sg = _glu_chunk(u_ref, wa_ref, wb_ref, wg_ref)
    c_ref[...] = c
    sg_ref[j] = sg
    _broadcast_taps(cw_ref, wbc_ref, taps)

    lbs = cn // LANES
    n_par = full_ref.shape[0]

    def seq_body(sb, carry):
        for q in range(n_par):
            s = sb * n_par + q
            rows = pl.ds(pl.multiple_of(s * t_new, t_new), t_new)
            for lb in range(lbs):
                ls = slice(lb * LANES, (lb + 1) * LANES)
                full_ref[q, pl.ds(lb, hist, stride=lbs), :] = st_in_ref[s, :, ls]
                full_ref[q, pl.ds(lbs * hist + lb, t_new, stride=lbs), :] = c_ref[rows, ls]
        for q in range(n_par):
            s = sb * n_par + q
            rows = pl.ds(pl.multiple_of(s * t_new, t_new), t_new)
            for lb in range(lbs):
                ls = slice(lb * LANES, (lb + 1) * LANES)
                acc = None
                for k in range(taps):
                    t = wbc_ref[k, :, ls] * full_ref[q, pl.ds(lbs * k + lb, t_new, stride=lbs), :]
                    acc = t if acc is None else acc + t
                y_ref[j, rows, ls] = acc
                st_ref[s, :, ls] = full_ref[q, pl.ds(lbs * t_new + lb, hist, stride=lbs), :]
        return carry
    lax.fori_loop(0, n_seq // n_par, seq_body, 0)

    @pl.when(j == n_j - 1)
    def _():
        _conv_finalize(y_ref.shape[0], tm, cn, y_ref, sg_ref, act_ref, x_ref, h_ref, cb_ref,
                       lng_ref, lnb_ref, gpost_ref, wout_ref)


def _const_spec(shape, n_grid):
    zeros = (0,) * len(shape)
    return pl.BlockSpec(shape, lambda *_: zeros, pipeline_mode=pl.Buffered(1))


def _conv_layer_prompt(x, w_in, cw, cb, lng, lnb, gpre, gpost, w_out, *, tm, cn):
    b, t, d = x.shape
    taps = cw.shape[0]
    n_j = d // cn
    grp = 4
    kern = functools.partial(_conv_prompt_kernel, tm=tm, cn=cn, taps=taps, grp=grp)
    vec = lambda: _const_spec((1, d), 3)
    return pl.pallas_call(
        kern,
        grid=(b, t // tm, n_j),
        in_specs=[
            pl.BlockSpec((None, tm, d), lambda bb, i, j: (bb, i, 0)),
            pl.BlockSpec((1, d, cn), lambda bb, i, j: (0, 0, j)),
            pl.BlockSpec((1, d, cn), lambda bb, i, j: (0, 0, n_j + j)),
            pl.BlockSpec((1, d, cn), lambda bb, i, j: (0, 0, 2 * n_j + j)),
            _const_spec((taps, d), 3),
            vec(), vec(), vec(), vec(), vec(),
            _const_spec((d, d), 3),
        ],
        out_specs=[
            pl.BlockSpec((None, tm, d), lambda bb, i, j: (bb, i, 0)),
            pl.BlockSpec((None, taps - 1, d), lambda bb, i, j: (bb, 0, 0)),
        ],
        out_shape=[
            jax.ShapeDtypeStruct((b, t, d), F32),
            jax.ShapeDtypeStruct((b, taps - 1, d), F32),
        ],
        scratch_shapes=[
            pltpu.VMEM((tm, d), BF16),
            pltpu.VMEM((n_j, (cn // LANES) * (HALO + tm), LANES), F32),
            pltpu.VMEM((n_j, taps, SUBLANES, cn), F32),
            pltpu.VMEM((n_j, tm, cn), F32),
            pltpu.VMEM((n_j, tm, cn), BF16),
            pltpu.VMEM((tm, d), BF16),
        ],
        compiler_params=pltpu.CompilerParams(
            dimension_semantics=("arbitrary", "arbitrary", "arbitrary"),
            vmem_limit_bytes=VMEM_LIMIT),
        name="conv_layer_prompt",
    )(x, w_in, w_in, w_in, cw, cb, lng, lnb, gpre, gpost, w_out)


def _conv_layer_sample(x, state, w_in, cw, cb, lng, lnb, gpre, gpost, w_out, *, n_seq_tile, cn):
    n_seq, t_new, d = x.shape
    taps = cw.shape[0]
    n_j = d // cn
    tm = n_seq_tile * t_new
    x2 = x.reshape(n_seq * t_new, d)
    kern = functools.partial(_conv_sample_kernel, tm=tm, cn=cn, taps=taps, t_new=t_new)
    vec = lambda: _const_spec((1, d), 2)
    h, st = pl.pallas_call(
        kern,
        grid=(n_seq // n_seq_tile, n_j),
        in_specs=[
            pl.BlockSpec((tm, d), lambda i, j: (i, 0)),
            pl.BlockSpec((n_seq_tile, taps - 1, cn), lambda i, j: (i, 0, j)),
            pl.BlockSpec((1, d, cn), lambda i, j: (0, 0, j)),
            pl.BlockSpec((1, d, cn), lambda i, j: (0, 0, n_j + j)),
            pl.BlockSpec((1, d, cn), lambda i, j: (0, 0, 2 * n_j + j)),
            pl.BlockSpec((taps, cn), lambda i, j: (0, j)),
            vec(), vec(), vec(), vec(), vec(),
            _const_spec((d, d), 2),
        ],
        out_specs=[
            pl.BlockSpec((tm, d), lambda i, j: (i, 0)),
            pl.BlockSpec((n_seq_tile, taps - 1, cn), lambda i, j: (i, 0, j)),
        ],
        out_shape=[
            jax.ShapeDtypeStruct((n_seq * t_new, d), F32),
            jax.ShapeDtypeStruct((n_seq, taps - 1, d), F32),
        ],
        scratch_shapes=[
            pltpu.VMEM((tm, d), BF16),
            pltpu.VMEM((tm, cn), F32),
            pltpu.VMEM((4, (cn // LANES) * (HALO + t_new), LANES), F32),
            pltpu.VMEM((taps, SUBLANES, cn), F32),
            pltpu.VMEM((n_j, tm, cn), F32),
            pltpu.VMEM((n_j, tm, cn), BF16),
            pltpu.VMEM((tm, d), BF16),
        ],
        compiler_params=pltpu.CompilerParams(
            dimension_semantics=("arbitrary", "arbitrary"),
            vmem_limit_bytes=VMEM_LIMIT),
        name="conv_layer_sample",
    )(x2, state, w_in, w_in, w_in, cw, cb, lng, lnb, gpre, gpost, w_out)
    return h.reshape(n_seq, t_new, d), st


def _attn_projections(h_ref, gkv_ref, gpre_ref, wkv_ref, win_ref, ukv_ref, u_ref, q_ref, sg_ref, tm,
                      n_chunk):
    d = h_ref.shape[-1]

    def body(r, c):
        rows = _rows(r, ROW_BLK)
        x = h_ref[rows, :]
        xn = x * _rms_scale(x)
        ukv_ref[rows, :] = (xn * gkv_ref[...]).astype(BF16)
        u_ref[rows, :] = (xn * gpre_ref[...]).astype(BF16)
        return c
    lax.fori_loop(0, tm // ROW_BLK, body, 0)

    kv = _dot(ukv_ref[...], wkv_ref[...])
    u = u_ref[...]
    cw = d // n_chunk
    scale = HEAD_DIM ** -0.5 * LOG2E
    for n in range(n_chunk):
        cols = slice(n * cw, (n + 1) * cw)
        q_ref[:, cols] = (_dot(u, win_ref[:, cols]) * scale).astype(BF16)
    for n in range(n_chunk):
        cols = slice(n * cw, (n + 1) * cw)
        sg_ref[:, cols] = jax.nn.silu(_dot(u, win_ref[:, d + n * cw:d + (n + 1) * cw])).astype(BF16)
    return kv


def _attn_output(act_ref, h_ref, gpost_ref, wout_ref, y_ref, tm):
    y_ref[...] = _dot(act_ref[...], wout_ref[...])

    def post_body(r, c):
        rows = _rows(r, ROW_BLK)
        o = y_ref[rows, :]
        y_ref[rows, :] = h_ref[rows, :] + o * _rms_scale(o) * gpost_ref[...]
        return c
    lax.fori_loop(0, tm // ROW_BLK, post_body, 0)


def _attn_block(sink_ref, q_ref, sg_ref, kpad_ref, vpad_ref, act_ref, r0, *, n_kv, first):
    blk = WINDOW
    pair_w = 2 * HEAD_DIM
    qi = lax.broadcasted_iota(jnp.int32, (blk, blk), 0)
    kj = lax.broadcasted_iota(jnp.int32, (blk, blk), 1)
    own = kj <= qi
    low = lax.broadcasted_iota(jnp.int32, (blk, pair_w), 1) < HEAD_DIM
    rows = pl.ds(r0, blk)
    krows = pl.ds(r0, 2 * blk)
    for gam in range(n_kv // 2):
        qs = jnp.concatenate(
            [q_ref[rows, (hh * (n_kv // 2) + gam) * pair_w:(hh * (n_kv // 2) + gam + 1) * pair_w]
             for hh in range(GROUP)], axis=0)
        probs, stats = [], []
        for e in range(2):
            g = 2 * gam + e
            s = _dot_t(qs, kpad_ref[krows, g * pair_w:(g + 1) * pair_w])
            p_parts, st = [], []
            for hh in range(GROUP):
                sh = s[hh * blk:(hh + 1) * blk]
                sf = jnp.where(own, sh[:, blk:], -jnp.inf if first else sh[:, :blk])
                sink = sink_ref[g * GROUP + hh] * LOG2E
                m = jnp.maximum(jnp.max(sf, axis=-1, keepdims=True), sink)
                p = jnp.exp2(sf - m)
                p_parts.append(jnp.concatenate([jnp.where(own, 0.0, p), jnp.where(own, p, 0.0)],
                                               axis=1).astype(BF16))
                st.append(jnp.exp2(sink - m))
            probs.append(jnp.concatenate(p_parts, axis=0))
            stats.append(st)
        o = (_dot(probs[0], vpad_ref[krows, (2 * gam) * 2 * pair_w:(2 * gam + 1) * 2 * pair_w])
             + _dot(probs[1], vpad_ref[krows, (2 * gam + 1) * 2 * pair_w:(2 * gam + 2) * 2 * pair_w]))
        for hh in range(GROUP):
            oh = o[hh * blk:(hh + 1) * blk]
            den = oh[:, pair_w:] + jnp.where(low, stats[0][hh], stats[1][hh])
            c0 = (hh * (n_kv // 2) + gam) * pair_w
            act_ref[rows, c0:c0 + pair_w] = (
                oh[:, :pair_w] / den * sg_ref[rows, c0:c0 + pair_w].astype(F32)).astype(BF16)


def _store_padded_kv(kv, kpad_ref, vpad_ref, row0, n_kv):
    n = kv.shape[0]
    kvw = n_kv * HEAD_DIM
    pair_w = 2 * HEAD_DIM
    low = lax.broadcasted_iota(jnp.int32, (n, pair_w), 1) < HEAD_DIM
    for gam in range(n_kv // 2):
        kc = kv[:, gam * pair_w:(gam + 1) * pair_w]
        vc = kv[:, kvw + gam * pair_w:kvw + (gam + 1) * pair_w]
        for e in range(2):
            keep = low if e == 0 else jnp.logical_not(low)
            g = 2 * gam + e
            kpad_ref[row0:row0 + n, g * pair_w:(g + 1) * pair_w] = jnp.where(keep, kc, 0.0).astype(BF16)
            vpad_ref[row0:row0 + n, g * 2 * pair_w:g * 2 * pair_w + pair_w] = (
                jnp.where(keep, vc, 0.0).astype(BF16))
            vpad_ref[row0:row0 + n, g * 2 * pair_w + pair_w:(g + 1) * 2 * pair_w] = (
                jnp.where(keep, 1.0, 0.0).astype(BF16))


def _attn_prompt_kernel(sink_ref, h_ref, gkv_ref, gpre_ref, gpost_ref, wkv_ref, win_ref, wout_ref,
                        y_ref, ck_ref, cv_ref,
                        ukv_ref, u_ref, q_ref, sg_ref, kpad_ref, vpad_ref, act_ref, *, tm, n_kv):
    i = pl.program_id(1)
    n_i = pl.num_programs(1)
    kvw = n_kv * HEAD_DIM
    blk = WINDOW

    kv = _attn_projections(h_ref, gkv_ref, gpre_ref, wkv_ref, win_ref, ukv_ref, u_ref, q_ref, sg_ref,
                           tm, 4)

    @pl.when(i == 0)
    def _():
        kpad_ref[0:blk, :] = jnp.zeros((blk, kpad_ref.shape[1]), BF16)
        vpad_ref[0:blk, :] = jnp.zeros((blk, vpad_ref.shape[1]), BF16)

    @pl.when(i > 0)
    def _():
        kpad_ref[0:blk, :] = kpad_ref[tm:tm + blk, :]
        vpad_ref[0:blk, :] = vpad_ref[tm:tm + blk, :]

    _store_padded_kv(kv, kpad_ref, vpad_ref, blk, n_kv)

    @pl.when(i == n_i - 1)
    def _():
        ck_ref[...] = kv[tm - blk:, :kvw]
        cv_ref[...] = kv[tm - blk:, kvw:]

    block = functools.partial(_attn_block, sink_ref, q_ref, sg_ref, kpad_ref, vpad_ref, act_ref, n_kv=n_kv)

    @pl.when(i == 0)
    def _():
        block(0, first=True)

    @pl.when(i > 0)
    def _():
        block(0, first=False)

    def blk_body(bi, carry):
        block(pl.multiple_of(bi * blk, blk), first=False)
        return carry
    lax.fori_loop(1, tm // blk, blk_body, 0)

    _attn_output(act_ref, h_ref, gpost_ref, wout_ref, y_ref, tm)


def _attn_layer_prompt(h, sinks, gkv, gpre, gpost, w_kv, w_in, w_out, *, tm):
    b, t, d = h.shape
    kvw = w_kv.shape[1] // 2
    n_kv = kvw // HEAD_DIM
    kern = functools.partial(_attn_prompt_kernel, tm=tm, n_kv=n_kv)
    vec = lambda: _const_spec((1, d), 2)
    return pl.pallas_call(
        kern,
        grid=(b, t // tm),
        in_specs=[
            pl.BlockSpec(memory_space=pltpu.SMEM),
            pl.BlockSpec((None, tm, d), lambda bb, i: (bb, i, 0)),
            vec(), vec(), vec(),
            _const_spec(w_kv.shape, 2),
            _const_spec(w_in.shape, 2),
            _const_spec(w_out.shape, 2),
        ],
        out_specs=[
            pl.BlockSpec((None, tm, d), lambda bb, i: (bb, i, 0)),
            pl.BlockSpec((None, WINDOW, kvw), lambda bb, i: (bb, 0, 0)),
            pl.BlockSpec((None, WINDOW, kvw), lambda bb, i: (bb, 0, 0)),
        ],
        out_shape=[
            jax.ShapeDtypeStruct((b, t, d), F32),
            jax.ShapeDtypeStruct((b, WINDOW, kvw), F32),
            jax.ShapeDtypeStruct((b, WINDOW, kvw), F32),
        ],
        scratch_shapes=[
            pltpu.VMEM((tm, d), BF16),
            pltpu.VMEM((tm, d), BF16),
            pltpu.VMEM((tm, d), BF16),
            pltpu.VMEM((tm, d), BF16),
            pltpu.VMEM((WINDOW + tm, 2 * kvw), BF16),
            pltpu.VMEM((WINDOW + tm, 4 * kvw), BF16),
            pltpu.VMEM((tm, d), BF16),
        ],
        compiler_params=pltpu.CompilerParams(
            dimension_semantics=("arbitrary", "arbitrary"),
            vmem_limit_bytes=VMEM_LIMIT),
        name="attn_layer_prompt",
    )(sinks, h, gkv, gpre, gpost, w_kv, w_in, w_out)


def _attn_sample_proj_kernel(h_ref, gkv_ref, gpre_ref, wkv_ref, win_ref, kv_ref, q_ref, sg_ref,
                             ukv_ref, u_ref, *, tm):
    kv_ref[...] = _attn_projections(h_ref, gkv_ref, gpre_ref, wkv_ref, win_ref, ukv_ref, u_ref, q_ref,
                                    sg_ref, tm, 4)


def _attn_sample_core_kernel(sink_ref, q_ref, sg_ref, kv_ref, ck_in_ref, cv_in_ref,
                             act_ref, ck_ref, cv_ref, *, n_seq, t_new, n_kv, n_par):
    kvw = n_kv * HEAD_DIM
    w_buf = ck_in_ref.shape[1]
    n_keys = w_buf + 2 * t_new
    n_rows = GROUP * n_kv * t_new
    tq = lax.broadcasted_iota(jnp.int32, (n_rows, n_keys), 0) % t_new
    kj = lax.broadcasted_iota(jnp.int32, (n_rows, n_keys), 1)
    mask = ((kj < w_buf) & (kj > tq)) | ((kj >= w_buf) & (kj - w_buf <= tq))
    row_blk = lax.broadcasted_iota(jnp.int32, (n_rows, 1), 0) // t_new
    sinkv = jnp.zeros((n_rows, 1), F32)
    for hh in range(GROUP):
        for g in range(n_kv):
            sinkv = jnp.where(row_blk == hh * n_kv + g, sink_ref[g * GROUP + hh] * LOG2E, sinkv)
    lane_kv = lax.broadcasted_iota(jnp.int32, (t_new, kvw), 1) // HEAD_DIM

    def one_seq(s):
        rows = pl.ds(pl.multiple_of(s * t_new, t_new), t_new)
        kc = ck_in_ref[s]
        vc = cv_in_ref[s]
        kvn = kv_ref[rows, :]
        ck_ref[s, 0:w_buf - t_new, :] = kc[t_new:]
        cv_ref[s, 0:w_buf - t_new, :] = vc[t_new:]
        ck_ref[s, w_buf - t_new:w_buf, :] = kvn[:, :kvw]
        cv_ref[s, w_buf - t_new:w_buf, :] = kvn[:, kvw:]
        pad = jnp.zeros((t_new, kvw), F32)
        k_all = jnp.concatenate([kc, kvn[:, :kvw], pad], axis=0).astype(BF16)
        v_all = jnp.concatenate([vc, kvn[:, kvw:], pad], axis=0).astype(BF16)
        qparts = []
        for hh in range(GROUP):
            slab = q_ref[rows, hh * kvw:(hh + 1) * kvw].astype(F32)
            for g in range(n_kv):
                qparts.append(jnp.where(lane_kv == g, slab, 0.0))
        qbd = jnp.concatenate(qparts, axis=0).astype(BF16)
        sc = jnp.where(mask, _dot_t(qbd, k_all), -jnp.inf)
        m = jnp.maximum(jnp.max(sc, axis=-1, keepdims=True), sinkv)
        p = jnp.exp2(sc - m)
        den = jnp.sum(p, axis=-1, keepdims=True) + jnp.exp2(sinkv - m)
        o = _dot(p.astype(BF16), v_all) * (1.0 / den)
        for hh in range(GROUP):
            slab = None
            for g in range(n_kv - 1, -1, -1):
                r = (hh * n_kv + g) * t_new
                part = o[r:r + t_new]
                slab = part if slab is None else jnp.where(lane_kv == g, part, slab)
            cols = slice(hh * kvw, (hh + 1) * kvw)
            act_ref[rows, cols] = (slab * sg_ref[rows, cols].astype(F32)).astype(BF16)

    def seq_body(sb, carry):
        for q in range(n_par):
            one_seq(sb * n_par + q)
        return carry
    lax.fori_loop(0, n_seq // n_par, seq_body, 0)


def _attn_sample_out_kernel(act_ref, h_ref, gpost_ref, wout_ref, y_ref, *, tm):
    _attn_output(act_ref, h_ref, gpost_ref, wout_ref, y_ref, tm)


def _attn_layer_sample(h, cache_k, cache_v, sinks, gkv, gpre, gpost, w_kv, w_in, w_out, *, tm,
                       n_seq_tile):
    n_seq, t_new, d = h.shape
    kvw = w_kv.shape[1] // 2
    n_kv = kvw // HEAD_DIM
    w_buf = cache_k.shape[1]
    n_tok = n_seq * t_new
    h2 = h.reshape(n_tok, d)
    vec = lambda: _const_spec((1, d), 1)
    params = pltpu.CompilerParams(dimension_semantics=("arbitrary",), vmem_limit_bytes=VMEM_LIMIT)

    kv, q, sg = pl.pallas_call(
        functools.partial(_attn_sample_proj_kernel, tm=tm),
        grid=(n_tok // tm,),
        in_specs=[
            pl.BlockSpec((tm, d), lambda i: (i, 0)),
            vec(), vec(),
            _const_spec(w_kv.shape, 1),
            _const_spec(w_in.shape, 1),
        ],
        out_specs=[
            pl.BlockSpec((tm, 2 * kvw), lambda i: (i, 0)),
            pl.BlockSpec((tm, d), lambda i: (i, 0)),
            pl.BlockSpec((tm, d), lambda i: (i, 0)),
        ],
        out_shape=[
            jax.ShapeDtypeStruct((n_tok, 2 * kvw), F32),
            jax.ShapeDtypeStruct((n_tok, d), BF16),
            jax.ShapeDtypeStruct((n_tok, d), BF16),
        ],
        scratch_shapes=[pltpu.VMEM((tm, d), BF16), pltpu.VMEM((tm, d), BF16)],
        compiler_params=params,
        name="attn_sample_proj",
    )(h2, gkv, gpre, w_kv, w_in)

    rows = n_seq_tile * t_new
    act, ck, cv = pl.pallas_call(
        functools.partial(_attn_sample_core_kernel, n_seq=n_seq_tile, t_new=t_new, n_kv=n_kv, n_par=2),
        grid=(n_seq // n_seq_tile,),
        in_specs=[
            pl.BlockSpec(memory_space=pltpu.SMEM),
            pl.BlockSpec((rows, d), lambda i: (i, 0)),
            pl.BlockSpec((rows, d), lambda i: (i, 0)),
            pl.BlockSpec((rows, 2 * kvw), lambda i: (i, 0)),
            pl.BlockSpec((n_seq_tile, w_buf, kvw), lambda i: (i, 0, 0)),
            pl.BlockSpec((n_seq_tile, w_buf, kvw), lambda i: (i, 0, 0)),
        ],
        out_specs=[
            pl.BlockSpec((rows, d), lambda i: (i, 0)),
            pl.BlockSpec((n_seq_tile, w_buf, kvw), lambda i: (i, 0, 0)),
            pl.BlockSpec((n_seq_tile, w_buf, kvw), lambda i: (i, 0, 0)),
        ],
        out_shape=[
            jax.ShapeDtypeStruct((n_tok, d), BF16),
            jax.ShapeDtypeStruct((n_seq, w_buf, kvw), F32),
            jax.ShapeDtypeStruct((n_seq, w_buf, kvw), F32),
        ],
        compiler_params=params,
        name="attn_sample_core",
    )(sinks, q, sg, kv, cache_k, cache_v)

    y = pl.pallas_call(
        functools.partial(_attn_sample_out_kernel, tm=tm),
        grid=(n_tok // tm,),
        in_specs=[
            pl.BlockSpec((tm, d), lambda i: (i, 0)),
            pl.BlockSpec((tm, d), lambda i: (i, 0)),
            vec(),
            _const_spec(w_out.shape, 1),
        ],
        out_specs=pl.BlockSpec((tm, d), lambda i: (i, 0)),
        out_shape=jax.ShapeDtypeStruct((n_tok, d), F32),
        compiler_params=params,
        name="attn_sample_out",
    )(act, h2, gpost, w_out)
    return y.reshape(n_seq, t_new, d), ck, cv


def kernel(x_prompt, x_sample, state_conv, cache_k, cache_v, norm_pre, norm_post, w_in_a, conv_w, conv_b, ln_g, ln_b, w_out_a, kv_norm, w_kv, w_in_b, sinks, w_out_b):
    n_a = w_in_a.shape[0]
    assert n_a == 1 and w_in_b.shape[0] == 1 and norm_pre.shape[0] == 2
    d = x_prompt.shape[-1]
    n_seq, w_buf, n_kv, hd = cache_k.shape
    assert hd == HEAD_DIM and w_buf == WINDOW

    row = lambda v: v.reshape(1, -1)
    w_in_a_bf = w_in_a[0].astype(BF16)
    w_out_a_bf = w_out_a[0].astype(BF16)
    w_kv_bf = w_kv.astype(BF16)
    qw = w_in_b.shape[-1] // 2
    w_in_b_bf = (w_in_b[0].reshape(d, 2, n_kv, GROUP, hd).transpose(0, 1, 3, 2, 4)
                 .reshape(d, 2 * qw).astype(BF16))
    w_out_b_bf = (w_out_b[0].reshape(n_kv, GROUP, hd, d).transpose(1, 0, 2, 3)
                  .reshape(qw, d).astype(BF16))
    conv_args = (w_in_a_bf[None], conv_w[0], row(conv_b[0]), row(ln_g[0]), row(ln_b[0]), row(norm_pre[0]),
                 row(norm_post[0]), w_out_a_bf)
    attn_args = (sinks[0], row(kv_norm), row(norm_pre[1]), row(norm_post[1]), w_kv_bf, w_in_b_bf,
                 w_out_b_bf)

    h_p, st_p = _conv_layer_prompt(x_prompt, *conv_args, tm=512, cn=256)
    y_p, ck_p, cv_p = _attn_layer_prompt(h_p, *attn_args, tm=256)

    h_s, st_s = _conv_layer_sample(x_sample, state_conv[0], *conv_args, n_seq_tile=64, cn=256)
    y_s, ck_s, cv_s = _attn_layer_sample(
        h_s, cache_k.reshape(n_seq, w_buf, n_kv * hd), cache_v.reshape(n_seq, w_buf, n_kv * hd),
        *attn_args, tm=512, n_seq_tile=16)

    b = x_prompt.shape[0]
    return (y_p, y_s, st_p[None], ck_p.reshape(b, w_buf, n_kv, hd), cv_p.reshape(b, w_buf, n_kv, hd),
            st_s[None], ck_s.reshape(n_seq, w_buf, n_kv, hd), cv_s.reshape(n_seq, w_buf, n_kv, hd))
```

```python
import functools

import jax
import jax.numpy as jnp
from jax import lax
from jax.experimental import pallas as pl
from jax.experimental.pallas import tpu as pltpu

RMS_EPS = 1e-6
LN_EPS = 1e-5
HEAD_DIM = 64
GROUP = 8
WINDOW = 128
SUBLANES = 8
LANES = 128
LOG2E = 1.4426950408889634
HALO = 32
VMEM_LIMIT = 56 * 1024 * 1024
ROW_BLK = 64

BF16 = jnp.bfloat16
F32 = jnp.float32


def _rows(i, n):
    return pl.ds(pl.multiple_of(i * n, n), n)


def _rms_scale(x):
    return lax.rsqrt(jnp.mean(x * x, axis=-1, keepdims=True) + RMS_EPS)


def _dot(a, b):
    return jnp.dot(a, b, preferred_element_type=F32)


def _dot_t(a, b):
    return lax.dot_general(a, b, (((1,), (1,)), ((), ())), preferred_element_type=F32)


def _pre_norm_to_bf16(x_ref, g_ref, u_ref, tm):
    def body(r, c):
        rows = _rows(r, ROW_BLK)
        x = x_ref[rows, :]
        u_ref[rows, :] = (x * _rms_scale(x) * g_ref[...]).astype(BF16)
        return c
    lax.fori_loop(0, tm // ROW_BLK, body, 0, unroll=2)


def _glu_chunk(u_ref, wa_ref, wb_ref, wg_ref):
    u = u_ref[...]
    c = _dot(u, wa_ref[0]) * jax.nn.sigmoid(_dot(u, wb_ref[0]))
    sg = jax.nn.silu(_dot(u, wg_ref[0])).astype(BF16)
    return c, sg


def _broadcast_taps(cw_ref, wbc_ref, taps):
    for k in range(taps):
        wbc_ref[k] = jnp.broadcast_to(cw_ref[k:k + 1, :], wbc_ref.shape[1:])


def _conv_finalize(j_chunks, tm, cn, y_ref, sg_ref, act_ref, x_ref, h_ref, cb_ref, lng_ref, lnb_ref,
                   gpost_ref, wout_ref):
    d = j_chunks * cn

    ln_rows = ROW_BLK

    def ln_body(r, c):
        rows = _rows(r, ln_rows)
        ys = [y_ref[jj, rows, :] + cb_ref[:, jj * cn:(jj + 1) * cn] for jj in range(j_chunks)]
        mu = jnp.sum(sum(ys), axis=-1, keepdims=True) * (1.0 / d)
        yc = [y - mu for y in ys]
        var = jnp.sum(sum(y * y for y in yc), axis=-1, keepdims=True) * (1.0 / d)
        rstd = lax.rsqrt(var + LN_EPS)
        for jj in range(j_chunks):
            cols = slice(jj * cn, (jj + 1) * cn)
            t = jax.nn.silu(yc[jj] * rstd * lng_ref[:, cols] + lnb_ref[:, cols])
            act_ref[rows, cols] = (t * sg_ref[jj, rows, :].astype(F32)).astype(BF16)
        return c
    lax.fori_loop(0, tm // ln_rows, ln_body, 0, unroll=2)

    h_ref[...] = _dot(act_ref[...], wout_ref[...])

    def post_body(r, c):
        rows = [_rows(2 * r + e, ROW_BLK) for e in range(2)]
        os = [h_ref[rw, :] for rw in rows]
        res = [x_ref[rw, :] + o * _rms_scale(o) * gpost_ref[...] for rw, o in zip(rows, os)]
        for rw, v in zip(rows, res):
            h_ref[rw, :] = v
        return c
    lax.fori_loop(0, tm // (2 * ROW_BLK), post_body, 0)


def _conv_chunk(cext_ref, wbc_ref, y_ref, jc, *, tm, cn, taps, grp):
    lbs = cn // LANES
    assert lbs >= 2, "the row-interleaved layout needs at least two lane blocks per chunk"
    lead = HALO - (taps - 1)
    for g in range(tm // (grp * SUBLANES)):
        base = g * grp * SUBLANES
        for lb in range(lbs):
            ls = slice(lb * LANES, (lb + 1) * LANES)
            acc = [None] * grp
            for k in range(taps):
                wv = wbc_ref[jc, k, :, ls]
                for gi in range(grp):
                    r0 = base + SUBLANES * gi + k + lead
                    t = wv * cext_ref[jc, pl.ds(lbs * r0 + lb, SUBLANES, stride=lbs), :]
                    acc[gi] = t if acc[gi] is None else acc[gi] + t
            for gi in range(grp):
                y_ref[jc, base + SUBLANES * gi:base + SUBLANES * (gi + 1), ls] = acc[gi]


def _conv_prompt_kernel(x_ref, wa_ref, wb_ref, wg_ref, cw_ref, cb_ref, lng_ref, lnb_ref, gpre_ref,
                        gpost_ref, wout_ref, h_ref, st_ref,
                        u_ref, cext_ref, wbc_ref, y_ref, sg_ref, act_ref, *, tm, cn, taps, grp):
    i = pl.program_id(1)
    j = pl.program_id(2)
    n_j = pl.num_programs(2)
    n_chunks = cext_ref.shape[0]
    lbs = cn // LANES

    @pl.when(j == 0)
    def _():
        _pre_norm_to_bf16(x_ref, gpre_ref, u_ref, tm)

    @pl.when(i == 0)
    def _():
        cext_ref[j, 0:lbs * HALO, :] = jnp.zeros((lbs * HALO, LANES), F32)

    @pl.when(i > 0)
    def _():
        cext_ref[j, 0:lbs * HALO, :] = cext_ref[j, lbs * tm:lbs * (tm + HALO), :]

    @pl.when((pl.program_id(0) == 0) & (i == 0) & (j == 0))
    def _():
        for jj in range(n_chunks):
            for k in range(taps):
                wbc_ref[jj, k] = jnp.broadcast_to(cw_ref[k:k + 1, jj * cn:(jj + 1) * cn], (SUBLANES, cn))

    c, sg = _glu_chunk(u_ref, wa_ref, wb_ref, wg_ref)
    sg_ref[j] = sg
    for lb in range(lbs):
        cext_ref[j, pl.ds(lbs * HALO + lb, tm, stride=lbs), :] = c[:, lb * LANES:(lb + 1) * LANES]
    _conv_chunk(cext_ref, wbc_ref, y_ref, j, tm=tm, cn=cn, taps=taps, grp=grp)

    @pl.when(j == n_j - 1)
    def _():
        _conv_finalize(n_chunks, tm, cn, y_ref, sg_ref, act_ref, x_ref, h_ref, cb_ref,
                       lng_ref, lnb_ref, gpost_ref, wout_ref)

        @pl.when(i == pl.num_programs(1) - 1)
        def _():
            tail = lbs * (HALO + tm - (taps - 1))
            for jj in range(n_chunks):
                for lb in range(lbs):
                    c0 = jj * cn + lb * LANES
                    st_ref[:, c0:c0 + LANES] = cext_ref[jj, pl.ds(tail + lb, taps - 1, stride=lbs), :]


def _conv_sample_kernel(x_ref, st_in_ref, wa_ref, wb_ref, wg_ref, cw_ref, cb_ref, lng_ref, lnb_ref,
                        gpre_ref, gpost_ref, wout_ref, h_ref, st_ref,
                        u_ref, c_ref, full_ref, wbc_ref, y_ref, sg_ref, act_ref, *, tm, cn, taps, t_new):
    j = pl.program_id(1)
    n_j = pl.num_programs(1)
    n_seq = tm // t_new
    hist = taps - 1

    @pl.when(j == 0)
    def _():
        _pre_norm_to_bf16(x_ref, gpre_ref, u_ref, tm)

    c, sg = _glu_chunk(u_ref, wa_ref, wb_ref, wg_ref)
    c_ref[...] = c
    sg_ref[j] = sg
    _broadcast_taps(cw_ref, wbc_ref, taps)

    lbs = cn // LANES
    n_par = full_ref.shape[0]

    def seq_body(sb, carry):
        for q in range(n_par):
            s = sb * n_par + q
            rows = pl.ds(pl.multiple_of(s * t_new, t_new), t_new)
            for lb in range(lbs):
                ls = slice(lb * LANES, (lb + 1) * LANES)
                full_ref[q, pl.ds(lb, hist, stride=lbs), :] = st_in_ref[s, :, ls]
                full_ref[q, pl.ds(lbs * hist + lb, t_new, stride=lbs), :] = c_ref[rows, ls]
        for q in range(n_par):
            s = sb * n_par + q
            rows = pl.ds(pl.multiple_of(s * t_new, t_new), t_new)
            for lb in range(lbs):
                ls = slice(lb * LANES, (lb + 1) * LANES)
                acc = None
                for k in range(taps):
                    t = wbc_ref[k, :, ls] * full_ref[q, pl.ds(lbs * k + lb, t_new, stride=lbs), :]
                    acc = t if acc is None else acc + t
                y_ref[j, rows, ls] = acc
                st_ref[s, :, ls] = full_ref[q, pl.ds(lbs * t_new + lb, hist, stride=lbs), :]
        return carry
    lax.fori_loop(0, n_seq // n_par, seq_body, 0)

    @pl.when(j == n_j - 1)
    def _():
        _conv_finalize(y_ref.shape[0], tm, cn, y_ref, sg_ref, act_ref, x_ref, h_ref, cb_ref,
                       lng_ref, lnb_ref, gpost_ref, wout_ref)


def _const_spec(shape, n_grid):
    zeros = (0,) * len(shape)
    return pl.BlockSpec(shape, lambda *_: zeros, pipeline_mode=pl.Buffered(1))


def _conv_layer_prompt(x, w_in, cw, cb, lng, lnb, gpre, gpost, w_out, *, tm, cn):
    b, t, d = x.shape
    taps = cw.shape[0]
    n_j = d // cn
    lbs = cn // LANES
    grp = 4
    kern = functools.partial(_conv_prompt_kernel, tm=tm, cn=cn, taps=taps, grp=grp)
    vec = lambda: _const_spec((1, d), 3)
    return pl.pallas_call(
        kern,
        grid=(b, t // tm, n_j),
        in_specs=[
            pl.BlockSpec((None, tm, d), lambda bb, i, j: (bb, i, 0)),
            pl.BlockSpec((1, d, cn), lambda bb, i, j: (0, 0, j)),
            pl.BlockSpec((1, d, cn), lambda bb, i, j: (0, 0, n_j + j)),
            pl.BlockSpec((1, d, cn), lambda bb, i, j: (0, 0, 2 * n_j + j)),
            _const_spec((taps, d), 3),
            vec(), vec(), vec(), vec(), vec(),
            _const_spec((d, d), 3),
        ],
        out_specs=[
            pl.BlockSpec((None, tm, d), lambda bb, i, j: (bb, i, 0)),
            pl.BlockSpec((None, taps - 1, d), lambda bb, i, j: (bb, 0, 0)),
        ],
        out_shape=[
            jax.ShapeDtypeStruct((b, t, d), F32),
            jax.ShapeDtypeStruct((b, taps - 1, d), F32),
        ],
        scratch_shapes=[
            pltpu.VMEM((tm, d), BF16),
            pltpu.VMEM((n_j, lbs * (HALO + tm), LANES), F32),
            pltpu.VMEM((n_j, taps, SUBLANES, cn), F32),
            pltpu.VMEM((n_j, tm, cn), F32),
            pltpu.VMEM((n_j, tm, cn), BF16),
            pltpu.VMEM((tm, d), BF16),
        ],
        compiler_params=pltpu.CompilerParams(
            dimension_semantics=("arbitrary", "arbitrary", "arbitrary"),
            vmem_limit_bytes=VMEM_LIMIT),
        name="conv_layer_prompt",
    )(x, w_in, w_in, w_in, cw, cb, lng, lnb, gpre, gpost, w_out)


def _conv_layer_sample(x, state, w_in, cw, cb, lng, lnb, gpre, gpost, w_out, *, n_seq_tile, cn):
    n_seq, t_new, d = x.shape
    taps = cw.shape[0]
    n_j = d // cn
    tm = n_seq_tile * t_new
    x2 = x.reshape(n_seq * t_new, d)
    kern = functools.partial(_conv_sample_kernel, tm=tm, cn=cn, taps=taps, t_new=t_new)
    vec = lambda: _const_spec((1, d), 2)
    h, st = pl.pallas_call(
        kern,
        grid=(n_seq // n_seq_tile, n_j),
        in_specs=[
            pl.BlockSpec((tm, d), lambda i, j: (i, 0)),
            pl.BlockSpec((n_seq_tile, taps - 1, cn), lambda i, j: (i, 0, j)),
            pl.BlockSpec((1, d, cn), lambda i, j: (0, 0, j)),
            pl.BlockSpec((1, d, cn), lambda i, j: (0, 0, n_j + j)),
            pl.BlockSpec((1, d, cn), lambda i, j: (0, 0, 2 * n_j + j)),
            pl.BlockSpec((taps, cn), lambda i, j: (0, j)),
            vec(), vec(), vec(), vec(), vec(),
            _const_spec((d, d), 2),
        ],
        out_specs=[
            pl.BlockSpec((tm, d), lambda i, j: (i, 0)),
            pl.BlockSpec((n_seq_tile, taps - 1, cn), lambda i, j: (i, 0, j)),
        ],
        out_shape=[
            jax.ShapeDtypeStruct((n_seq * t_new, d), F32),
            jax.ShapeDtypeStruct((n_seq, taps - 1, d), F32),
        ],
        scratch_shapes=[
            pltpu.VMEM((tm, d), BF16),
            pltpu.VMEM((tm, cn), F32),
            pltpu.VMEM((4, (cn // LANES) * (HALO + t_new), LANES), F32),
            pltpu.VMEM((taps, SUBLANES, cn), F32),
            pltpu.VMEM((n_j, tm, cn), F32),
            pltpu.VMEM((n_j, tm, cn), BF16),
            pltpu.VMEM((tm, d), BF16),
        ],
        compiler_params=pltpu.CompilerParams(
            dimension_semantics=("arbitrary", "arbitrary"),
            vmem_limit_bytes=VMEM_LIMIT),
        name="conv_layer_sample",
    )(x2, state, w_in, w_in, w_in, cw, cb, lng, lnb, gpre, gpost, w_out)
    return h.reshape(n_seq, t_new, d), st


def _attn_projections(h_ref, gkv_ref, gpre_ref, wkv_ref, win_ref, ukv_ref, u_ref, q_ref, sg_ref, tm,
                      n_chunk):
    d = h_ref.shape[-1]

    def body(r, c):
        rows = _rows(r, ROW_BLK)
        x = h_ref[rows, :]
        xn = x * _rms_scale(x)
        ukv_ref[rows, :] = (xn * gkv_ref[...]).astype(BF16)
        u_ref[rows, :] = (xn * gpre_ref[...]).astype(BF16)
        return c
    lax.fori_loop(0, tm // ROW_BLK, body, 0, unroll=2)

    kv = _dot(ukv_ref[...], wkv_ref[...])
    u = u_ref[...]
    cw = d // n_chunk
    scale = HEAD_DIM ** -0.5 * LOG2E
    for n in range(n_chunk):
        cols = slice(n * cw, (n + 1) * cw)
        q_ref[:, cols] = (_dot(u, win_ref[:, cols]) * scale).astype(BF16)
    for n in range(n_chunk):
        cols = slice(n * cw, (n + 1) * cw)
        sg_ref[:, cols] = jax.nn.silu(_dot(u, win_ref[:, d + n * cw:d + (n + 1) * cw])).astype(BF16)
    return kv


def _attn_output(act_ref, h_ref, gpost_ref, wout_ref, y_ref, tm):
    y_ref[...] = _dot(act_ref[...], wout_ref[...])

    def post_body(r, c):
        rows = [_rows(2 * r + e, ROW_BLK) for e in range(2)]
        os = [y_ref[rw, :] for rw in rows]
        res = [h_ref[rw, :] + o * _rms_scale(o) * gpost_ref[...] for rw, o in zip(rows, os)]
        for rw, v in zip(rows, res):
            y_ref[rw, :] = v
        return c
    lax.fori_loop(0, tm // (2 * ROW_BLK), post_body, 0)


def _attn_block(sink_ref, q_ref, sg_ref, kpad_ref, vpad_ref, act_ref, r0, *, n_kv, first):
    blk = WINDOW
    pair_w = 2 * HEAD_DIM
    qi = lax.broadcasted_iota(jnp.int32, (blk, blk), 0)
    kj = lax.broadcasted_iota(jnp.int32, (blk, blk), 1)
    own = kj <= qi
    low = lax.broadcasted_iota(jnp.int32, (blk, pair_w), 1) < HEAD_DIM
    rows = pl.ds(r0, blk)
    krows = pl.ds(r0, 2 * blk)
    for gam in range(n_kv // 2):
        qs = jnp.concatenate(
            [q_ref[rows, (hh * (n_kv // 2) + gam) * pair_w:(hh * (n_kv // 2) + gam + 1) * pair_w]
             for hh in range(GROUP)], axis=0)
        probs, stats = [], []
        for e in range(2):
            g = 2 * gam + e
            s = _dot_t(qs, kpad_ref[krows, g * pair_w:(g + 1) * pair_w])
            p_parts, st = [], []
            for hh in range(GROUP):
                sh = s[hh * blk:(hh + 1) * blk]
                sf = jnp.where(own, sh[:, blk:], -jnp.inf if first else sh[:, :blk])
                sink = sink_ref[g * GROUP + hh] * LOG2E
                m = jnp.maximum(jnp.max(sf, axis=-1, keepdims=True), sink)
                p = jnp.exp2(sf - m)
                p_parts.append(jnp.concatenate([jnp.where(own, 0.0, p), jnp.where(own, p, 0.0)],
                                               axis=1).astype(BF16))
                st.append(jnp.exp2(sink - m))
            probs.append(jnp.concatenate(p_parts, axis=0))
            stats.append(st)
        o = (_dot(probs[0], vpad_ref[krows, (2 * gam) * 2 * pair_w:(2 * gam + 1) * 2 * pair_w])
             + _dot(probs[1], vpad_ref[krows, (2 * gam + 1) * 2 * pair_w:(2 * gam + 2) * 2 * pair_w]))
        for hh in range(GROUP):
            oh = o[hh * blk:(hh + 1) * blk]
            den = oh[:, pair_w:] + jnp.where(low, stats[0][hh], stats[1][hh])
            c0 = (hh * (n_kv // 2) + gam) * pair_w
            act_ref[rows, c0:c0 + pair_w] = (
                oh[:, :pair_w] / den * sg_ref[rows, c0:c0 + pair_w].astype(F32)).astype(BF16)


def _store_padded_kv(kv, kpad_ref, vpad_ref, row0, n_kv):
    n = kv.shape[0]
    kvw = n_kv * HEAD_DIM
    pair_w = 2 * HEAD_DIM
    low = lax.broadcasted_iota(jnp.int32, (n, pair_w), 1) < HEAD_DIM
    for gam in range(n_kv // 2):
        kc = kv[:, gam * pair_w:(gam + 1) * pair_w]
        vc = kv[:, kvw + gam * pair_w:kvw + (gam + 1) * pair_w]
        for e in range(2):
            keep = low if e == 0 else jnp.logical_not(low)
            g = 2 * gam + e
            kpad_ref[row0:row0 + n, g * pair_w:(g + 1) * pair_w] = jnp.where(keep, kc, 0.0).astype(BF16)
            vpad_ref[row0:row0 + n, g * 2 * pair_w:g * 2 * pair_w + pair_w] = (
                jnp.where(keep, vc, 0.0).astype(BF16))
            vpad_ref[row0:row0 + n, g * 2 * pair_w + pair_w:(g + 1) * 2 * pair_w] = (
                jnp.where(keep, 1.0, 0.0).astype(BF16))


def _attn_prompt_kernel(sink_ref, h_ref, gkv_ref, gpre_ref, gpost_ref, wkv_ref, win_ref, wout_ref,
                        y_ref, ck_ref, cv_ref,
                        ukv_ref, u_ref, q_ref, sg_ref, kpad_ref, vpad_ref, act_ref, *, tm, n_kv):
    i = pl.program_id(1)
    n_i = pl.num_programs(1)
    kvw = n_kv * HEAD_DIM
    blk = WINDOW

    kv = _attn_projections(h_ref, gkv_ref, gpre_ref, wkv_ref, win_ref, ukv_ref, u_ref, q_ref, sg_ref,
                           tm, 4)

    @pl.when(i == 0)
    def _():
        kpad_ref[0:blk, :] = jnp.zeros((blk, kpad_ref.shape[1]), BF16)
        vpad_ref[0:blk, :] = jnp.zeros((blk, vpad_ref.shape[1]), BF16)

    @pl.when(i > 0)
    def _():
        kpad_ref[0:blk, :] = kpad_ref[tm:tm + blk, :]
        vpad_ref[0:blk, :] = vpad_ref[tm:tm + blk, :]

    _store_padded_kv(kv, kpad_ref, vpad_ref, blk, n_kv)

    @pl.when(i == n_i - 1)
    def _():
        ck_ref[...] = kv[tm - blk:, :kvw]
        cv_ref[...] = kv[tm - blk:, kvw:]

    block = functools.partial(_attn_block, sink_ref, q_ref, sg_ref, kpad_ref, vpad_ref, act_ref, n_kv=n_kv)

    @pl.when(i == 0)
    def _():
        block(0, first=True)

    @pl.when(i > 0)
    def _():
        block(0, first=False)

    def blk_body(bi, carry):
        block(pl.multiple_of(bi * blk, blk), first=False)
        return carry
    lax.fori_loop(1, tm // blk, blk_body, 0)

    _attn_output(act_ref, h_ref, gpost_ref, wout_ref, y_ref, tm)


def _attn_layer_prompt(h, sinks, gkv, gpre, gpost, w_kv, w_in, w_out, *, tm):
    b, t, d = h.shape
    kvw = w_kv.shape[1] // 2
    n_kv = kvw // HEAD_DIM
    kern = functools.partial(_attn_prompt_kernel, tm=tm, n_kv=n_kv)
    vec = lambda: _const_spec((1, d), 2)
    return pl.pallas_call(
        kern,
        grid=(b, t // tm),
        in_specs=[
            pl.BlockSpec(memory_space=pltpu.SMEM),
            pl.BlockSpec((None, tm, d), lambda bb, i: (bb, i, 0)),
            vec(), vec(), vec(),
            _const_spec(w_kv.shape, 2),
            _const_spec(w_in.shape, 2),
            _const_spec(w_out.shape, 2),
        ],
        out_specs=[
            pl.BlockSpec((None, tm, d), lambda bb, i: (bb, i, 0)),
            pl.BlockSpec((None, WINDOW, kvw), lambda bb, i: (bb, 0, 0)),
            pl.BlockSpec((None, WINDOW, kvw), lambda bb, i: (bb, 0, 0)),
        ],
        out_shape=[
            jax.ShapeDtypeStruct((b, t, d), F32),
            jax.ShapeDtypeStruct((b, WINDOW, kvw), F32),
            jax.ShapeDtypeStruct((b, WINDOW, kvw), F32),
        ],
        scratch_shapes=[
            pltpu.VMEM((tm, d), BF16),
            pltpu.VMEM((tm, d), BF16),
            pltpu.VMEM((tm, d), BF16),
            pltpu.VMEM((tm, d), BF16),
            pltpu.VMEM((WINDOW + tm, 2 * kvw), BF16),
            pltpu.VMEM((WINDOW + tm, 4 * kvw), BF16),
            pltpu.VMEM((tm, d), BF16),
        ],
        compiler_params=pltpu.CompilerParams(
            dimension_semantics=("arbitrary", "arbitrary"),
            vmem_limit_bytes=VMEM_LIMIT),
        name="attn_layer_prompt",
    )(sinks, h, gkv, gpre, gpost, w_kv, w_in, w_out)


def _attn_sample_proj_kernel(h_ref, gkv_ref, gpre_ref, wkv_ref, win_ref, kv_ref, q_ref, sg_ref,
                             ukv_ref, u_ref, *, tm):
    kv_ref[...] = _attn_projections(h_ref, gkv_ref, gpre_ref, wkv_ref, win_ref, ukv_ref, u_ref, q_ref,
                                    sg_ref, tm, 4)


def _attn_sample_core_kernel(sink_ref, q_ref, sg_ref, kv_ref, ck_in_ref, cv_in_ref,
                             act_ref, ck_ref, cv_ref, *, n_seq, t_new, n_kv, n_par):
    kvw = n_kv * HEAD_DIM
    w_buf = ck_in_ref.shape[1]
    n_keys = w_buf + 2 * t_new
    n_rows = GROUP * n_kv * t_new
    tq = lax.broadcasted_iota(jnp.int32, (n_rows, n_keys), 0) % t_new
    kj = lax.broadcasted_iota(jnp.int32, (n_rows, n_keys), 1)
    mask = ((kj < w_buf) & (kj > tq)) | ((kj >= w_buf) & (kj - w_buf <= tq))
    row_blk = lax.broadcasted_iota(jnp.int32, (n_rows, 1), 0) // t_new
    sinkv = jnp.zeros((n_rows, 1), F32)
    for hh in range(GROUP):
        for g in range(n_kv):
            sinkv = jnp.where(row_blk == hh * n_kv + g, sink_ref[g * GROUP + hh] * LOG2E, sinkv)
    lane_kv = lax.broadcasted_iota(jnp.int32, (t_new, kvw), 1) // HEAD_DIM

    def one_seq(s):
        rows = pl.ds(pl.multiple_of(s * t_new, t_new), t_new)
        kc = ck_in_ref[s]
        vc = cv_in_ref[s]
        kvn = kv_ref[rows, :]
        ck_ref[s, 0:w_buf - t_new, :] = kc[t_new:]
        cv_ref[s, 0:w_buf - t_new, :] = vc[t_new:]
        ck_ref[s, w_buf - t_new:w_buf, :] = kvn[:, :kvw]
        cv_ref[s, w_buf - t_new:w_buf, :] = kvn[:, kvw:]
        pad = jnp.zeros((t_new, kvw), F32)
        k_all = jnp.concatenate([kc, kvn[:, :kvw], pad], axis=0).astype(BF16)
        v_all = jnp.concatenate([vc, kvn[:, kvw:], pad], axis=0).astype(BF16)
        qparts = []
        for hh in range(GROUP):
            slab = q_ref[rows, hh * kvw:(hh + 1) * kvw].astype(F32)
            for g in range(n_kv):
                qparts.append(jnp.where(lane_kv == g, slab, 0.0))
        qbd = jnp.concatenate(qparts, axis=0).astype(BF16)
        sc = jnp.where(mask, _dot_t(qbd, k_all), -jnp.inf)
        m = jnp.maximum(jnp.max(sc, axis=-1, keepdims=True), sinkv)
        p = jnp.exp2(sc - m)
        den = jnp.sum(p, axis=-1, keepdims=True) + jnp.exp2(sinkv - m)
        o = _dot(p.astype(BF16), v_all) * (1.0 / den)
        for hh in range(GROUP):
            slab = None
            for g in range(n_kv - 1, -1, -1):
                r = (hh * n_kv + g) * t_new
                part = o[r:r + t_new]
                slab = part if slab is None else jnp.where(lane_kv == g, part, slab)
            cols = slice(hh * kvw, (hh + 1) * kvw)
            act_ref[rows, cols] = (slab * sg_ref[rows, cols].astype(F32)).astype(BF16)

    def seq_body(sb, carry):
        for q in range(n_par):
            one_seq(sb * n_par + q)
        return carry
    lax.fori_loop(0, n_seq // n_par, seq_body, 0)


def _attn_sample_out_kernel(act_ref, h_ref, gpost_ref, wout_ref, y_ref, *, tm):
    _attn_output(act_ref, h_ref, gpost_ref, wout_ref, y_ref, tm)


def _attn_layer_sample(h, cache_k, cache_v, sinks, gkv, gpre, gpost, w_kv, w_in, w_out, *, tm,
                       n_seq_tile):
    n_seq, t_new, d = h.shape
    kvw = w_kv.shape[1] // 2
    n_kv = kvw // HEAD_DIM
    w_buf = cache_k.shape[1]
    n_tok = n_seq * t_new
    h2 = h.reshape(n_tok, d)
    vec = lambda: _const_spec((1, d), 1)
    params = pltpu.CompilerParams(dimension_semantics=("arbitrary",), vmem_limit_bytes=VMEM_LIMIT)

    kv, q, sg = pl.pallas_call(
        functools.partial(_attn_sample_proj_kernel, tm=tm),
        grid=(n_tok // tm,),
        in_specs=[
            pl.BlockSpec((tm, d), lambda i: (i, 0)),
            vec(), vec(),
            _const_spec(w_kv.shape, 1),
            _const_spec(w_in.shape, 1),
        ],
        out_specs=[
            pl.BlockSpec((tm, 2 * kvw), lambda i: (i, 0)),
            pl.BlockSpec((tm, d), lambda i: (i, 0)),
            pl.BlockSpec((tm, d), lambda i: (i, 0)),
        ],
        out_shape=[
            jax.ShapeDtypeStruct((n_tok, 2 * kvw), F32),
            jax.ShapeDtypeStruct((n_tok, d), BF16),
            jax.ShapeDtypeStruct((n_tok, d), BF16),
        ],
        scratch_shapes=[pltpu.VMEM((tm, d), BF16), pltpu.VMEM((tm, d), BF16)],
        compiler_params=params,
        name="attn_sample_proj",
    )(h2, gkv, gpre, w_kv, w_in)

    rows = n_seq_tile * t_new
    act, ck, cv = pl.pallas_call(
        functools.partial(_attn_sample_core_kernel, n_seq=n_seq_tile, t_new=t_new, n_kv=n_kv, n_par=2),
        grid=(n_seq // n_seq_tile,),
        in_specs=[
            pl.BlockSpec(memory_space=pltpu.SMEM),
            pl.BlockSpec((rows, d), lambda i: (i, 0)),
            pl.BlockSpec((rows, d), lambda i: (i, 0)),
            pl.BlockSpec((rows, 2 * kvw), lambda i: (i, 0)),
            pl.BlockSpec((n_seq_tile, w_buf, kvw), lambda i: (i, 0, 0)),
            pl.BlockSpec((n_seq_tile, w_buf, kvw), lambda i: (i, 0, 0)),
        ],
        out_specs=[
            pl.BlockSpec((rows, d), lambda i: (i, 0)),
            pl.BlockSpec((n_seq_tile, w_buf, kvw), lambda i: (i, 0, 0)),
            pl.BlockSpec((n_seq_tile, w_buf, kvw), lambda i: (i, 0, 0)),
        ],
        out_shape=[
            jax.ShapeDtypeStruct((n_tok, d), BF16),
            jax.ShapeDtypeStruct((n_seq, w_buf, kvw), F32),
            jax.ShapeDtypeStruct((n_seq, w_buf, kvw), F32),
        ],
        compiler_params=params,
        name="attn_sample_core",
    )(sinks, q, sg, kv, cache_k, cache_v)

    y = pl.pallas_call(
        functools.partial(_attn_sample_out_kernel, tm=tm),
        grid=(n_tok // tm,),
        in_specs=[
            pl.BlockSpec((tm, d), lambda i: (i, 0)),
            pl.BlockSpec((tm, d), lambda i: (i, 0)),
            vec(),
            _const_spec(w_out.shape, 1),
        ],
        out_specs=pl.BlockSpec((tm, d), lambda i: (i, 0)),
        out_shape=jax.ShapeDtypeStruct((n_tok, d), F32),
        compiler_params=params,
        name="attn_sample_out",
    )(act, h2, gpost, w_out)
    return y.reshape(n_seq, t_new, d), ck, cv


def kernel(x_prompt, x_sample, state_conv, cache_k, cache_v, norm_pre, norm_post, w_in_a, conv_w, conv_b, ln_g, ln_b, w_out_a, kv_norm, w_kv, w_in_b, sinks, w_out_b):
    n_a = w_in_a.shape[0]
    assert n_a == 1 and w_in_b.shape[0] == 1 and norm_pre.shape[0] == 2
    d = x_prompt.shape[-1]
    n_seq, w_buf, n_kv, hd = cache_k.shape
    assert hd == HEAD_DIM and w_buf == WINDOW

    row = lambda v: v.reshape(1, -1)
    w_in_a_bf = w_in_a[0].astype(BF16)
    w_out_a_bf = w_out_a[0].astype(BF16)
    w_kv_bf = w_kv.astype(BF16)
    qw = w_in_b.shape[-1] // 2
    w_in_b_bf = (w_in_b[0].astype(BF16).reshape(d, 2, n_kv, GROUP, hd).transpose(0, 1, 3, 2, 4)
                 .reshape(d, 2 * qw))
    w_out_b_bf = (w_out_b[0].astype(BF16).reshape(n_kv, GROUP, hd, d).transpose(1, 0, 2, 3)
                  .reshape(qw, d))
    conv_args = (w_in_a_bf[None], conv_w[0], row(conv_b[0]), row(ln_g[0]), row(ln_b[0]), row(norm_pre[0]),
                 row(norm_post[0]), w_out_a_bf)
    attn_args = (sinks[0], row(kv_norm), row(norm_pre[1]), row(norm_post[1]), w_kv_bf, w_in_b_bf,
                 w_out_b_bf)

    h_p, st_p = _conv_layer_prompt(x_prompt, *conv_args, tm=512, cn=256)
    y_p, ck_p, cv_p = _attn_layer_prompt(h_p, *attn_args, tm=256)

    h_s, st_s = _conv_layer_sample(x_sample, state_conv[0], *conv_args, n_seq_tile=64, cn=256)
    y_s, ck_s, cv_s = _attn_layer_sample(
        h_s, cache_k.reshape(n_seq, w_buf, n_kv * hd), cache_v.reshape(n_seq, w_buf, n_kv * hd),
        *attn_args, tm=512, n_seq_tile=16)

    b = x_prompt.shape[0]
    return (y_p, y_s, st_p[None], ck_p.reshape(b, w_buf, n_kv, hd), cv_p.reshape(b, w_buf, n_kv, hd),
            st_s[None], ck_s.reshape(n_seq, w_buf, n_kv, hd), cv_s.reshape(n_seq, w_buf, n_kv, hd))
```

```python
import functools

import jax
import jax.numpy as jnp
from jax import lax
from jax.experimental import pallas as pl
from jax.experimental.pallas import tpu as pltpu

RMS_EPS = 1e-6
LN_EPS = 1e-5
HEAD_DIM = 64
GROUP = 8
WINDOW = 128
SUBLANES = 8
LANES = 128
LOG2E = 1.4426950408889634
HALO = 32
VMEM_LIMIT = 56 * 1024 * 1024
ROW_BLK = 64

BF16 = jnp.bfloat16
F32 = jnp.float32


def _rows(i, n):
    return pl.ds(pl.multiple_of(i * n, n), n)


def _rms_scale(x):
    return lax.rsqrt(jnp.mean(x * x, axis=-1, keepdims=True) + RMS_EPS)


def _dot(a, b):
    return jnp.dot(a, b, preferred_element_type=F32)


def _dot_t(a, b):
    return lax.dot_general(a, b, (((1,), (1,)), ((), ())), preferred_element_type=F32)


def _pre_norm_to_bf16(x_ref, g_ref, u_ref, tm):
    def body(r, c):
        rows = _rows(r, ROW_BLK)
        x = x_ref[rows, :]
        u_ref[rows, :] = (x * _rms_scale(x) * g_ref[...]).astype(BF16)
        return c
    lax.fori_loop(0, tm // ROW_BLK, body, 0, unroll=2)


def _glu_chunk(u_ref, wa_ref, wb_ref, wg_ref):
    u = u_ref[...]
    c = _dot(u, wa_ref[0]) * jax.nn.sigmoid(_dot(u, wb_ref[0]))
    sg = jax.nn.silu(_dot(u, wg_ref[0])).astype(BF16)
    return c, sg


def _broadcast_taps(cw_ref, wbc_ref, taps):
    for k in range(taps):
        wbc_ref[k] = jnp.broadcast_to(cw_ref[k:k + 1, :], wbc_ref.shape[1:])


def _conv_finalize(j_chunks, tm, cn, y_ref, sg_ref, act_ref, x_ref, h_ref, cb_ref, lng_ref, lnb_ref,
                   gpost_ref, wout_ref):
    d = j_chunks * cn

    ln_rows = ROW_BLK

    def ln_body(r, c):
        rows = _rows(r, ln_rows)
        ys = [y_ref[jj, rows, :] + cb_ref[:, jj * cn:(jj + 1) * cn] for jj in range(j_chunks)]
        mu = jnp.sum(sum(ys), axis=-1, keepdims=True) * (1.0 / d)
        yc = [y - mu for y in ys]
        var = jnp.sum(sum(y * y for y in yc), axis=-1, keepdims=True) * (1.0 / d)
        rstd = lax.rsqrt(var + LN_EPS)
        for jj in range(j_chunks):
            cols = slice(jj * cn, (jj + 1) * cn)
            t = jax.nn.silu(yc[jj] * rstd * lng_ref[:, cols] + lnb_ref[:, cols])
            act_ref[rows, cols] = (t * sg_ref[jj, rows, :].astype(F32)).astype(BF16)
        return c
    lax.fori_loop(0, tm // ln_rows, ln_body, 0, unroll=2)

    h_ref[...] = _dot(act_ref[...], wout_ref[...])

    def post_body(r, c):
        rows = [_rows(2 * r + e, ROW_BLK) for e in range(2)]
        os = [h_ref[rw, :] for rw in rows]
        res = [x_ref[rw, :] + o * _rms_scale(o) * gpost_ref[...] for rw, o in zip(rows, os)]
        for rw, v in zip(rows, res):
            h_ref[rw, :] = v
        return c
    lax.fori_loop(0, tm // (2 * ROW_BLK), post_body, 0)


def _conv_chunk(cext_ref, wbc_ref, y_ref, jc, *, tm, cn, taps, grp):
    lbs = cn // LANES
    assert lbs >= 2, "the row-interleaved layout needs at least two lane blocks per chunk"
    lead = HALO - (taps - 1)
    for g in range(tm // (grp * SUBLANES)):
        base = g * grp * SUBLANES
        for lb in range(lbs):
            ls = slice(lb * LANES, (lb + 1) * LANES)
            acc = [None] * grp
            for k in range(taps):
                wv = wbc_ref[jc, k, :, ls]
                for gi in range(grp):
                    r0 = base + SUBLANES * gi + k + lead
                    t = wv * cext_ref[jc, pl.ds(lbs * r0 + lb, SUBLANES, stride=lbs), :]
                    acc[gi] = t if acc[gi] is None else acc[gi] + t
            for gi in range(grp):
                y_ref[jc, base + SUBLANES * gi:base + SUBLANES * (gi + 1), ls] = acc[gi]


def _conv_prompt_kernel(x_ref, wa_ref, wb_ref, wg_ref, cw_ref, cb_ref, lng_ref, lnb_ref, gpre_ref,
                        gpost_ref, wout_ref, h_ref, st_ref,
                        u_ref, cext_ref, wbc_ref, y_ref, sg_ref, act_ref, *, tm, cn, taps, grp):
    i = pl.program_id(1)
    j = pl.program_id(2)
    n_j = pl.num_programs(2)
    n_chunks = cext_ref.shape[0]
    lbs = cn // LANES

    @pl.when(j == 0)
    def _():
        _pre_norm_to_bf16(x_ref, gpre_ref, u_ref, tm)

    @pl.when(i == 0)
    def _():
        cext_ref[j, 0:lbs * HALO, :] = jnp.zeros((lbs * HALO, LANES), F32)

    @pl.when(i > 0)
    def _():
        cext_ref[j, 0:lbs * HALO, :] = cext_ref[j, lbs * tm:lbs * (tm + HALO), :]

    @pl.when((pl.program_id(0) == 0) & (i == 0) & (j == 0))
    def _():
        for jj in range(n_chunks):
            for k in range(taps):
                wbc_ref[jj, k] = jnp.broadcast_to(cw_ref[k:k + 1, jj * cn:(jj + 1) * cn], (SUBLANES, cn))

    c, sg = _glu_chunk(u_ref, wa_ref, wb_ref, wg_ref)
    sg_ref[j] = sg
    for lb in range(lbs):
        cext_ref[j, pl.ds(lbs * HALO + lb, tm, stride=lbs), :] = c[:, lb * LANES:(lb + 1) * LANES]
    _conv_chunk(cext_ref, wbc_ref, y_ref, j, tm=tm, cn=cn, taps=taps, grp=grp)

    @pl.when(j == n_j - 1)
    def _():
        _conv_finalize(n_chunks, tm, cn, y_ref, sg_ref, act_ref, x_ref, h_ref, cb_ref,
                       lng_ref, lnb_ref, gpost_ref, wout_ref)

        @pl.when(i == pl.num_programs(1) - 1)
        def _():
            tail = lbs * (HALO + tm - (taps - 1))
            for jj in range(n_chunks):
                for lb in range(lbs):
                    c0 = jj * cn + lb * LANES
                    st_ref[:, c0:c0 + LANES] = cext_ref[jj, pl.ds(tail + lb, taps - 1, stride=lbs), :]


def _conv_sample_kernel(x_ref, st_in_ref, wa_ref, wb_ref, wg_ref, cw_ref, cb_ref, lng_ref, lnb_ref,
                        gpre_ref, gpost_ref, wout_ref, h_ref, st_ref,
                        u_ref, c_ref, full_ref, wbc_ref, y_ref, sg_ref, act_ref, *, tm, cn, taps, t_new):
    j = pl.program_id(1)
    n_j = pl.num_programs(1)
    n_seq = tm // t_new
    hist = taps - 1

    @pl.when(j == 0)
    def _():
        _pre_norm_to_bf16(x_ref, gpre_ref, u_ref, tm)

    c, sg = _glu_chunk(u_ref, wa_ref, wb_ref, wg_ref)
    c_ref[...] = c
    sg_ref[j] = sg
    _broadcast_taps(cw_ref, wbc_ref, taps)

    lbs = cn // LANES
    n_par = full_ref.shape[0]

    def seq_body(sb, carry):
        for q in range(n_par):
            s = sb * n_par + q
            rows = pl.ds(pl.multiple_of(s * t_new, t_new), t_new)
            for lb in range(lbs):
                ls = slice(lb * LANES, (lb + 1) * LANES)
                full_ref[q, pl.ds(lb, hist, stride=lbs), :] = st_in_ref[s, :, ls]
                full_ref[q, pl.ds(lbs * hist + lb, t_new, stride=lbs), :] = c_ref[rows, ls]
        for q in range(n_par):
            s = sb * n_par + q
            rows = pl.ds(pl.multiple_of(s * t_new, t_new), t_new)
            for lb in range(lbs):
                ls = slice(lb * LANES, (lb + 1) * LANES)
                acc = None
                for k in range(taps):
                    t = wbc_ref[k, :, ls] * full_ref[q, pl.ds(lbs * k + lb, t_new, stride=lbs), :]
                    acc = t if acc is None else acc + t
                y_ref[j, rows, ls] = acc
                st_ref[s, :, ls] = full_ref[q, pl.ds(lbs * t_new + lb, hist, stride=lbs), :]
        return carry
    lax.fori_loop(0, n_seq // n_par, seq_body, 0)

    @pl.when(j == n_j - 1)
    def _():
        _conv_finalize(y_ref.shape[0], tm, cn, y_ref, sg_ref, act_ref, x_ref, h_ref, cb_ref,
                       lng_ref, lnb_ref, gpost_ref, wout_ref)


def _const_spec(shape, n_grid):
    zeros = (0,) * len(shape)
    return pl.BlockSpec(shape, lambda *_: zeros, pipeline_mode=pl.Buffered(1))


def _conv_layer_prompt(x, w_in, cw, cb, lng, lnb, gpre, gpost, w_out, *, tm, cn):
    b, t, d = x.shape
    taps = cw.shape[0]
    n_j = d // cn
    lbs = cn // LANES
    grp = 4
    kern = functools.partial(_conv_prompt_kernel, tm=tm, cn=cn, taps=taps, grp=grp)
    vec = lambda: _const_spec((1, d), 3)
    return pl.pallas_call(
        kern,
        grid=(b, t // tm, n_j),
        in_specs=[
            pl.BlockSpec((None, tm, d), lambda bb, i, j: (bb, i, 0)),
            pl.BlockSpec((1, d, cn), lambda bb, i, j: (0, 0, j)),
            pl.BlockSpec((1, d, cn), lambda bb, i, j: (0, 0, n_j + j)),
            pl.BlockSpec((1, d, cn), lambda bb, i, j: (0, 0, 2 * n_j + j)),
            _const_spec((taps, d), 3),
            vec(), vec(), vec(), vec(), vec(),
            _const_spec((d, d), 3),
        ],
        out_specs=[
            pl.BlockSpec((None, tm, d), lambda bb, i, j: (bb, i, 0)),
            pl.BlockSpec((None, taps - 1, d), lambda bb, i, j: (bb, 0, 0)),
        ],
        out_shape=[
            jax.ShapeDtypeStruct((b, t, d), F32),
            jax.ShapeDtypeStruct((b, taps - 1, d), F32),
        ],
        scratch_shapes=[
            pltpu.VMEM((tm, d), BF16),
            pltpu.VMEM((n_j, lbs * (HALO + tm), LANES), F32),
            pltpu.VMEM((n_j, taps, SUBLANES, cn), F32),
            pltpu.VMEM((n_j, tm, cn), F32),
            pltpu.VMEM((n_j, tm, cn), BF16),
            pltpu.VMEM((tm, d), BF16),
        ],
        compiler_params=pltpu.CompilerParams(
            dimension_semantics=("arbitrary", "arbitrary", "arbitrary"),
            vmem_limit_bytes=VMEM_LIMIT),
        name="conv_layer_prompt",
    )(x, w_in, w_in, w_in, cw, cb, lng, lnb, gpre, gpost, w_out)


def _conv_layer_sample(x, state, w_in, cw, cb, lng, lnb, gpre, gpost, w_out, *, n_seq_tile, cn):
    n_seq, t_new, d = x.shape
    taps = cw.shape[0]
    n_j = d // cn
    tm = n_seq_tile * t_new
    x2 = x.reshape(n_seq * t_new, d)
    kern = functools.partial(_conv_sample_kernel, tm=tm, cn=cn, taps=taps, t_new=t_new)
    vec = lambda: _const_spec((1, d), 2)
    h, st = pl.pallas_call(
        kern,
        grid=(n_seq // n_seq_tile, n_j),
        in_specs=[
            pl.BlockSpec((tm, d), lambda i, j: (i, 0)),
            pl.BlockSpec((n_seq_tile, taps - 1, cn), lambda i, j: (i, 0, j)),
            pl.BlockSpec((1, d, cn), lambda i, j: (0, 0, j)),
            pl.BlockSpec((1, d, cn), lambda i, j: (0, 0, n_j + j)),
            pl.BlockSpec((1, d, cn), lambda i, j: (0, 0, 2 * n_j + j)),
            pl.BlockSpec((taps, cn), lambda i, j: (0, j)),
            vec(), vec(), vec(), vec(), vec(),
            _const_spec((d, d), 2),
        ],
        out_specs=[
            pl.BlockSpec((tm, d), lambda i, j: (i, 0)),
            pl.BlockSpec((n_seq_tile, taps - 1, cn), lambda i, j: (i, 0, j)),
        ],
        out_shape=[
            jax.ShapeDtypeStruct((n_seq * t_new, d), F32),
            jax.ShapeDtypeStruct((n_seq, taps - 1, d), F32),
        ],
        scratch_shapes=[
            pltpu.VMEM((tm, d), BF16),
            pltpu.VMEM((tm, cn), F32),
            pltpu.VMEM((4, (cn // LANES) * (HALO + t_new), LANES), F32),
            pltpu.VMEM((taps, SUBLANES, cn), F32),
            pltpu.VMEM((n_j, tm, cn), F32),
            pltpu.VMEM((n_j, tm, cn), BF16),
            pltpu.VMEM((tm, d), BF16),
        ],
        compiler_params=pltpu.CompilerParams(
            dimension_semantics=("arbitrary", "arbitrary"),
            vmem_limit_bytes=VMEM_LIMIT),
        name="conv_layer_sample",
    )(x2, state, w_in, w_in, w_in, cw, cb, lng, lnb, gpre, gpost, w_out)
    return h.reshape(n_seq, t_new, d), st


def _attn_projections(h_ref, gkv_ref, gpre_ref, wkv_ref, win_ref, ukv_ref, u_ref, q_ref, sg_ref, tm,
                      n_chunk):
    d = h_ref.shape[-1]

    def body(r, c):
        rows = _rows(r, ROW_BLK)
        x = h_ref[rows, :]
        xn = x * _rms_scale(x)
        ukv_ref[rows, :] = (xn * gkv_ref[...]).astype(BF16)
        u_ref[rows, :] = (xn * gpre_ref[...]).astype(BF16)
        return c
    lax.fori_loop(0, tm // ROW_BLK, body, 0, unroll=2)

    kv = _dot(ukv_ref[...], wkv_ref[...])
    u = u_ref[...]
    cw = d // n_chunk
    scale = HEAD_DIM ** -0.5 * LOG2E
    for n in range(n_chunk):
        cols = slice(n * cw, (n + 1) * cw)
        q_ref[:, cols] = (_dot(u, win_ref[:, cols]) * scale).astype(BF16)
    for n in range(n_chunk):
        cols = slice(n * cw, (n + 1) * cw)
        sg_ref[:, cols] = jax.nn.silu(_dot(u, win_ref[:, d + n * cw:d + (n + 1) * cw])).astype(BF16)
    return kv


def _attn_output(act_ref, h_ref, gpost_ref, wout_ref, y_ref, tm):
    y_ref[...] = _dot(act_ref[...], wout_ref[...])

    def post_body(r, c):
        rows = [_rows(2 * r + e, ROW_BLK) for e in range(2)]
        os = [y_ref[rw, :] for rw in rows]
        res = [h_ref[rw, :] + o * _rms_scale(o) * gpost_ref[...] for rw, o in zip(rows, os)]
        for rw, v in zip(rows, res):
            y_ref[rw, :] = v
        return c
    lax.fori_loop(0, tm // (2 * ROW_BLK), post_body, 0)


def _attn_block(sink_ref, q_ref, sg_ref, kpad_ref, vpad_ref, act_ref, r0, *, n_kv, first):
    blk = WINDOW
    pair_w = 2 * HEAD_DIM
    n_pair = GROUP // 2
    qi = lax.broadcasted_iota(jnp.int32, (blk, blk), 0)
    kj = lax.broadcasted_iota(jnp.int32, (blk, blk), 1)
    own = kj <= qi
    low = lax.broadcasted_iota(jnp.int32, (blk, pair_w), 1) < HEAD_DIM
    rows = pl.ds(r0, blk)
    krows = pl.ds(r0, 2 * blk)
    for g in range(n_kv):
        qs = jnp.concatenate(
            [q_ref[rows, (g * n_pair + pp) * pair_w:(g * n_pair + pp + 1) * pair_w]
             for pp in range(n_pair)], axis=0)
        probs, stats = [], []
        for e in range(2):
            s = _dot_t(qs, kpad_ref[krows, (2 * g + e) * pair_w:(2 * g + e + 1) * pair_w])
            p_parts, st = [], []
            for pp in range(n_pair):
                sh = s[pp * blk:(pp + 1) * blk]
                sf = jnp.where(own, sh[:, blk:], -jnp.inf if first else sh[:, :blk])
                sink = sink_ref[g * GROUP + 2 * pp + e] * LOG2E
                m = jnp.maximum(jnp.max(sf, axis=-1, keepdims=True), sink)
                p = jnp.exp2(sf - m)
                p_parts.append(jnp.concatenate([jnp.where(own, 0.0, p), jnp.where(own, p, 0.0)],
                                               axis=1).astype(BF16))
                st.append(jnp.exp2(sink - m))
            probs.append(jnp.concatenate(p_parts, axis=0))
            stats.append(st)
        o = (_dot(probs[0], vpad_ref[krows, (2 * g) * 2 * pair_w:(2 * g + 1) * 2 * pair_w])
             + _dot(probs[1], vpad_ref[krows, (2 * g + 1) * 2 * pair_w:(2 * g + 2) * 2 * pair_w]))
        for pp in range(n_pair):
            oh = o[pp * blk:(pp + 1) * blk]
            den = oh[:, pair_w:] + jnp.where(low, stats[0][pp], stats[1][pp])
            c0 = (g * n_pair + pp) * pair_w
            act_ref[rows, c0:c0 + pair_w] = (
                oh[:, :pair_w] / den * sg_ref[rows, c0:c0 + pair_w].astype(F32)).astype(BF16)


def _store_padded_kv(kv, kpad_ref, vpad_ref, row0, n_kv):
    n = kv.shape[0]
    kvw = n_kv * HEAD_DIM
    pair_w = 2 * HEAD_DIM
    low = lax.broadcasted_iota(jnp.int32, (n, pair_w), 1) < HEAD_DIM
    halves = (low, jnp.logical_not(low))
    for gam in range(n_kv // 2):
        kc = kv[:, gam * pair_w:(gam + 1) * pair_w]
        vc = kv[:, kvw + gam * pair_w:kvw + (gam + 1) * pair_w]
        for side in range(2):
            g = 2 * gam + side
            k_here = jnp.where(halves[side], kc, 0.0)
            v_here = jnp.where(halves[side], vc, 0.0)
            k_other = pltpu.roll(k_here, HEAD_DIM, axis=1)
            v_other = pltpu.roll(v_here, HEAD_DIM, axis=1)
            for e in range(2):
                kc0 = (2 * g + e) * pair_w
                vc0 = (2 * g + e) * 2 * pair_w
                kpad_ref[row0:row0 + n, kc0:kc0 + pair_w] = (k_here if e == side else k_other).astype(BF16)
                vpad_ref[row0:row0 + n, vc0:vc0 + pair_w] = (v_here if e == side else v_other).astype(BF16)
                vpad_ref[row0:row0 + n, vc0 + pair_w:vc0 + 2 * pair_w] = (
                    jnp.where(halves[e], 1.0, 0.0).astype(BF16))


def _attn_prompt_kernel(sink_ref, h_ref, gkv_ref, gpre_ref, gpost_ref, wkv_ref, win_ref, wout_ref,
                        y_ref, ck_ref, cv_ref,
                        ukv_ref, u_ref, q_ref, sg_ref, kpad_ref, vpad_ref, act_ref, *, tm, n_kv):
    i = pl.program_id(1)
    n_i = pl.num_programs(1)
    kvw = n_kv * HEAD_DIM
    blk = WINDOW

    kv = _attn_projections(h_ref, gkv_ref, gpre_ref, wkv_ref, win_ref, ukv_ref, u_ref, q_ref, sg_ref,
                           tm, 4)

    @pl.when(i == 0)
    def _():
        kpad_ref[0:blk, :] = jnp.zeros((blk, kpad_ref.shape[1]), BF16)
        vpad_ref[0:blk, :] = jnp.zeros((blk, vpad_ref.shape[1]), BF16)

    @pl.when(i > 0)
    def _():
        kpad_ref[0:blk, :] = kpad_ref[tm:tm + blk, :]
        vpad_ref[0:blk, :] = vpad_ref[tm:tm + blk, :]

    _store_padded_kv(kv, kpad_ref, vpad_ref, blk, n_kv)

    @pl.when(i == n_i - 1)
    def _():
        ck_ref[...] = kv[tm - blk:, :kvw]
        cv_ref[...] = kv[tm - blk:, kvw:]

    block = functools.partial(_attn_block, sink_ref, q_ref, sg_ref, kpad_ref, vpad_ref, act_ref, n_kv=n_kv)

    @pl.when(i == 0)
    def _():
        block(0, first=True)

    @pl.when(i > 0)
    def _():
        block(0, first=False)

    def blk_body(bi, carry):
        block(pl.multiple_of(bi * blk, blk), first=False)
        return carry
    lax.fori_loop(1, tm // blk, blk_body, 0)

    _attn_output(act_ref, h_ref, gpost_ref, wout_ref, y_ref, tm)


def _attn_layer_prompt(h, sinks, gkv, gpre, gpost, w_kv, w_in, w_out, *, tm):
    b, t, d = h.shape
    kvw = w_kv.shape[1] // 2
    n_kv = kvw // HEAD_DIM
    kern = functools.partial(_attn_prompt_kernel, tm=tm, n_kv=n_kv)
    vec = lambda: _const_spec((1, d), 2)
    return pl.pallas_call(
        kern,
        grid=(b, t // tm),
        in_specs=[
            pl.BlockSpec(memory_space=pltpu.SMEM),
            pl.BlockSpec((None, tm, d), lambda bb, i: (bb, i, 0)),
            vec(), vec(), vec(),
            _const_spec(w_kv.shape, 2),
            _const_spec(w_in.shape, 2),
            _const_spec(w_out.shape, 2),
        ],
        out_specs=[
            pl.BlockSpec((None, tm, d), lambda bb, i: (bb, i, 0)),
            pl.BlockSpec((None, WINDOW, kvw), lambda bb, i: (bb, 0, 0)),
            pl.BlockSpec((None, WINDOW, kvw), lambda bb, i: (bb, 0, 0)),
        ],
        out_shape=[
            jax.ShapeDtypeStruct((b, t, d), F32),
            jax.ShapeDtypeStruct((b, WINDOW, kvw), F32),
            jax.ShapeDtypeStruct((b, WINDOW, kvw), F32),
        ],
        scratch_shapes=[
            pltpu.VMEM((tm, d), BF16),
            pltpu.VMEM((tm, d), BF16),
            pltpu.VMEM((tm, d), BF16),
            pltpu.VMEM((tm, d), BF16),
            pltpu.VMEM((WINDOW + tm, 4 * kvw), BF16),
            pltpu.VMEM((WINDOW + tm, 8 * kvw), BF16),
            pltpu.VMEM((tm, d), BF16),
        ],
        compiler_params=pltpu.CompilerParams(
            dimension_semantics=("arbitrary", "arbitrary"),
            vmem_limit_bytes=VMEM_LIMIT),
        name="attn_layer_prompt",
    )(sinks, h, gkv, gpre, gpost, w_kv, w_in, w_out)


def _attn_sample_proj_kernel(h_ref, gkv_ref, gpre_ref, wkv_ref, win_ref, kv_ref, q_ref, sg_ref,
                             ukv_ref, u_ref, *, tm):
    kv_ref[...] = _attn_projections(h_ref, gkv_ref, gpre_ref, wkv_ref, win_ref, ukv_ref, u_ref, q_ref,
                                    sg_ref, tm, 4)


def _attn_sample_core_kernel(sink_ref, q_ref, sg_ref, kv_ref, ck_in_ref, cv_in_ref,
                             act_ref, ck_ref, cv_ref, *, n_seq, t_new, n_kv, n_par):
    kvw = n_kv * HEAD_DIM
    w_buf = ck_in_ref.shape[1]
    n_keys = w_buf + 2 * t_new
    n_rows = GROUP * n_kv * t_new
    tq = lax.broadcasted_iota(jnp.int32, (n_rows, n_keys), 0) % t_new
    kj = lax.broadcasted_iota(jnp.int32, (n_rows, n_keys), 1)
    mask = ((kj < w_buf) & (kj > tq)) | ((kj >= w_buf) & (kj - w_buf <= tq))
    row_blk = lax.broadcasted_iota(jnp.int32, (n_rows, 1), 0) // t_new
    sinkv = jnp.zeros((n_rows, 1), F32)
    for hh in range(GROUP):
        for g in range(n_kv):
            sinkv = jnp.where(row_blk == hh * n_kv + g, sink_ref[g * GROUP + hh] * LOG2E, sinkv)
    lane_kv = lax.broadcasted_iota(jnp.int32, (t_new, kvw), 1) // HEAD_DIM

    def one_seq(s):
        rows = pl.ds(pl.multiple_of(s * t_new, t_new), t_new)
        kc = ck_in_ref[s]
        vc = cv_in_ref[s]
        kvn = kv_ref[rows, :]
        ck_ref[s, 0:w_buf - t_new, :] = kc[t_new:]
        cv_ref[s, 0:w_buf - t_new, :] = vc[t_new:]
        ck_ref[s, w_buf - t_new:w_buf, :] = kvn[:, :kvw]
        cv_ref[s, w_buf - t_new:w_buf, :] = kvn[:, kvw:]
        pad = jnp.zeros((t_new, kvw), F32)
        k_all = jnp.concatenate([kc, kvn[:, :kvw], pad], axis=0).astype(BF16)
        v_all = jnp.concatenate([vc, kvn[:, kvw:], pad], axis=0).astype(BF16)
        qparts = []
        for hh in range(GROUP):
            slab = q_ref[rows, hh * kvw:(hh + 1) * kvw].astype(F32)
            for g in range(n_kv):
                qparts.append(jnp.where(lane_kv == g, slab, 0.0))
        qbd = jnp.concatenate(qparts, axis=0).astype(BF16)
        sc = jnp.where(mask, _dot_t(qbd, k_all), -jnp.inf)
        m = jnp.maximum(jnp.max(sc, axis=-1, keepdims=True), sinkv)
        p = jnp.exp2(sc - m)
        den = jnp.sum(p, axis=-1, keepdims=True) + jnp.exp2(sinkv - m)
        o = _dot(p.astype(BF16), v_all) * (1.0 / den)
        for hh in range(GROUP):
            slab = None
            for g in range(n_kv - 1, -1, -1):
                r = (hh * n_kv + g) * t_new
                part = o[r:r + t_new]
                slab = part if slab is None else jnp.where(lane_kv == g, part, slab)
            cols = slice(hh * kvw, (hh + 1) * kvw)
            act_ref[rows, cols] = (slab * sg_ref[rows, cols].astype(F32)).astype(BF16)

    def seq_body(sb, carry):
        for q in range(n_par):
            one_seq(sb * n_par + q)
        return carry
    lax.fori_loop(0, n_seq // n_par, seq_body, 0)


def _attn_sample_out_kernel(act_ref, h_ref, gpost_ref, wout_ref, y_ref, *, tm):
    _attn_output(act_ref, h_ref, gpost_ref, wout_ref, y_ref, tm)


def _attn_layer_sample(h, cache_k, cache_v, sinks, gkv, gpre, gpost, w_kv, w_in, w_out, *, tm,
                       n_seq_tile):
    n_seq, t_new, d = h.shape
    kvw = w_kv.shape[1] // 2
    n_kv = kvw // HEAD_DIM
    w_buf = cache_k.shape[1]
    n_tok = n_seq * t_new
    h2 = h.reshape(n_tok, d)
    vec = lambda: _const_spec((1, d), 1)
    params = pltpu.CompilerParams(dimension_semantics=("arbitrary",), vmem_limit_bytes=VMEM_LIMIT)

    kv, q, sg = pl.pallas_call(
        functools.partial(_attn_sample_proj_kernel, tm=tm),
        grid=(n_tok // tm,),
        in_specs=[
            pl.BlockSpec((tm, d), lambda i: (i, 0)),
            vec(), vec(),
            _const_spec(w_kv.shape, 1),
            _const_spec(w_in.shape, 1),
        ],
        out_specs=[
            pl.BlockSpec((tm, 2 * kvw), lambda i: (i, 0)),
            pl.BlockSpec((tm, d), lambda i: (i, 0)),
            pl.BlockSpec((tm, d), lambda i: (i, 0)),
        ],
        out_shape=[
            jax.ShapeDtypeStruct((n_tok, 2 * kvw), F32),
            jax.ShapeDtypeStruct((n_tok, d), BF16),
            jax.ShapeDtypeStruct((n_tok, d), BF16),
        ],
        scratch_shapes=[pltpu.VMEM((tm, d), BF16), pltpu.VMEM((tm, d), BF16)],
        compiler_params=params,
        name="attn_sample_proj",
    )(h2, gkv, gpre, w_kv, w_in)

    def to_slabs(a):
        return a.reshape(n_tok, n_kv, GROUP, HEAD_DIM).transpose(0, 2, 1, 3).reshape(n_tok, d)

    def from_slabs(a):
        return a.reshape(n_tok, GROUP, n_kv, HEAD_DIM).transpose(0, 2, 1, 3).reshape(n_tok, d)

    q, sg = to_slabs(q), to_slabs(sg)
    rows = n_seq_tile * t_new
    act, ck, cv = pl.pallas_call(
        functools.partial(_attn_sample_core_kernel, n_seq=n_seq_tile, t_new=t_new, n_kv=n_kv, n_par=2),
        grid=(n_seq // n_seq_tile,),
        in_specs=[
            pl.BlockSpec(memory_space=pltpu.SMEM),
            pl.BlockSpec((rows, d), lambda i: (i, 0)),
            pl.BlockSpec((rows, d), lambda i: (i, 0)),
            pl.BlockSpec((rows, 2 * kvw), lambda i: (i, 0)),
            pl.BlockSpec((n_seq_tile, w_buf, kvw), lambda i: (i, 0, 0)),
            pl.BlockSpec((n_seq_tile, w_buf, kvw), lambda i: (i, 0, 0)),
        ],
        out_specs=[
            pl.BlockSpec((rows, d), lambda i: (i, 0)),
            pl.BlockSpec((n_seq_tile, w_buf, kvw), lambda i: (i, 0, 0)),
            pl.BlockSpec((n_seq_tile, w_buf, kvw), lambda i: (i, 0, 0)),
        ],
        out_shape=[
            jax.ShapeDtypeStruct((n_tok, d), BF16),
            jax.ShapeDtypeStruct((n_seq, w_buf, kvw), F32),
            jax.ShapeDtypeStruct((n_seq, w_buf, kvw), F32),
        ],
        compiler_params=params,
        name="attn_sample_core",
    )(sinks, q, sg, kv, cache_k, cache_v)

    y = pl.pallas_call(
        functools.partial(_attn_sample_out_kernel, tm=tm),
        grid=(n_tok // tm,),
        in_specs=[
            pl.BlockSpec((tm, d), lambda i: (i, 0)),
            pl.BlockSpec((tm, d), lambda i: (i, 0)),
            vec(),
            _const_spec(w_out.shape, 1),
        ],
        out_specs=pl.BlockSpec((tm, d), lambda i: (i, 0)),
        out_shape=jax.ShapeDtypeStruct((n_tok, d), F32),
        compiler_params=params,
        name="attn_sample_out",
    )(from_slabs(act), h2, gpost, w_out)
    return y.reshape(n_seq, t_new, d), ck, cv


def kernel(x_prompt, x_sample, state_conv, cache_k, cache_v, norm_pre, norm_post, w_in_a, conv_w, conv_b, ln_g, ln_b, w_out_a, kv_norm, w_kv, w_in_b, sinks, w_out_b):
    n_a = w_in_a.shape[0]
    assert n_a == 1 and w_in_b.shape[0] == 1 and norm_pre.shape[0] == 2
    d = x_prompt.shape[-1]
    n_seq, w_buf, n_kv, hd = cache_k.shape
    assert hd == HEAD_DIM and w_buf == WINDOW

    row = lambda v: v.reshape(1, -1)
    w_in_a_bf = w_in_a[0].astype(BF16)
    w_out_a_bf = w_out_a[0].astype(BF16)
    w_kv_bf = w_kv.astype(BF16)
    w_in_b_bf = w_in_b[0].astype(BF16)
    w_out_b_bf = w_out_b[0].astype(BF16)
    conv_args = (w_in_a_bf[None], conv_w[0], row(conv_b[0]), row(ln_g[0]), row(ln_b[0]), row(norm_pre[0]),
                 row(norm_post[0]), w_out_a_bf)
    attn_args = (sinks[0], row(kv_norm), row(norm_pre[1]), row(norm_post[1]), w_kv_bf, w_in_b_bf,
                 w_out_b_bf)

    h_p, st_p = _conv_layer_prompt(x_prompt, *conv_args, tm=512, cn=256)
    y_p, ck_p, cv_p = _attn_layer_prompt(h_p, *attn_args, tm=256)

    h_s, st_s = _conv_layer_sample(x_sample, state_conv[0], *conv_args, n_seq_tile=64, cn=256)
    y_s, ck_s, cv_s = _attn_layer_sample(
        h_s, cache_k.reshape(n_seq, w_buf, n_kv * hd), cache_v.reshape(n_seq, w_buf, n_kv * hd),
        *attn_args, tm=512, n_seq_tile=16)

    b = x_prompt.shape[0]
    return (y_p, y_s, st_p[None], ck_p.reshape(b, w_buf, n_kv, hd), cv_p.reshape(b, w_buf, n_kv, hd),
            st_s[None], ck_s.reshape(n_seq, w_buf, n_kv, hd), cv_s.reshape(n_seq, w_buf, n_kv, hd))
```

```python
import functools

import jax
import jax.numpy as jnp
from jax import lax
from jax.experimental import pallas as pl
from jax.experimental.pallas import tpu as pltpu

RMS_EPS = 1e-6
LN_EPS = 1e-5
HEAD_DIM = 64
GROUP = 8
WINDOW = 128
SUBLANES = 8
LANES = 128
LOG2E = 1.4426950408889634
HALO = 32
VMEM_LIMIT = 56 * 1024 * 1024
ROW_BLK = 64

BF16 = jnp.bfloat16
F32 = jnp.float32


def _rows(i, n):
    return pl.ds(pl.multiple_of(i * n, n), n)


def _rms_scale(x):
    return lax.rsqrt(jnp.mean(x * x, axis=-1, keepdims=True) + RMS_EPS)


def _dot(a, b):
    return jnp.dot(a, b, preferred_element_type=F32)


def _dot_t(a, b):
    return lax.dot_general(a, b, (((1,), (1,)), ((), ())), preferred_element_type=F32)


def _pre_norm_to_bf16(x_ref, g_ref, u_ref, tm):
    def body(r, c):
        rows = _rows(r, ROW_BLK)
        x = x_ref[rows, :]
        u_ref[rows, :] = (x * _rms_scale(x) * g_ref[...]).astype(BF16)
        return c
    lax.fori_loop(0, tm // ROW_BLK, body, 0, unroll=2)


def _glu_chunk(u_ref, wa_ref, wb_ref, wg_ref):
    u = u_ref[...]
    c = _dot(u, wa_ref[0]) * jax.nn.sigmoid(_dot(u, wb_ref[0]))
    sg = jax.nn.silu(_dot(u, wg_ref[0])).astype(BF16)
    return c, sg


def _broadcast_taps(cw_ref, wbc_ref, taps):
    for k in range(taps):
        wbc_ref[k] = jnp.broadcast_to(cw_ref[k:k + 1, :], wbc_ref.shape[1:])


def _conv_finalize(j_chunks, tm, cn, y_ref, sg_ref, act_ref, x_ref, h_ref, cb_ref, lng_ref, lnb_ref,
                   gpost_ref, wout_ref):
    d = j_chunks * cn

    ln_rows = ROW_BLK

    def ln_body(r, c):
        rows = _rows(r, ln_rows)
        ys = [y_ref[jj, rows, :] + cb_ref[:, jj * cn:(jj + 1) * cn] for jj in range(j_chunks)]
        mu = jnp.sum(sum(ys), axis=-1, keepdims=True) * (1.0 / d)
        yc = [y - mu for y in ys]
        var = jnp.sum(sum(y * y for y in yc), axis=-1, keepdims=True) * (1.0 / d)
        rstd = lax.rsqrt(var + LN_EPS)
        for jj in range(j_chunks):
            cols = slice(jj * cn, (jj + 1) * cn)
            t = jax.nn.silu(yc[jj] * rstd * lng_ref[:, cols] + lnb_ref[:, cols])
            act_ref[rows, cols] = (t * sg_ref[jj, rows, :].astype(F32)).astype(BF16)
        return c
    lax.fori_loop(0, tm // ln_rows, ln_body, 0, unroll=2)

    h_ref[...] = _dot(act_ref[...], wout_ref[...])

    def post_body(r, c):
        rows = [_rows(2 * r + e, ROW_BLK) for e in range(2)]
        os = [h_ref[rw, :] for rw in rows]
        res = [x_ref[rw, :] + o * _rms_scale(o) * gpost_ref[...] for rw, o in zip(rows, os)]
        for rw, v in zip(rows, res):
            h_ref[rw, :] = v
        return c
    lax.fori_loop(0, tm // (2 * ROW_BLK), post_body, 0)


def _conv_chunk(cext_ref, wbc_ref, y_ref, jc, *, tm, cn, taps, grp):
    lbs = cn // LANES
    assert lbs >= 2, "the row-interleaved layout needs at least two lane blocks per chunk"
    lead = HALO - (taps - 1)
    for g in range(tm // (grp * SUBLANES)):
        base = g * grp * SUBLANES
        for lb in range(lbs):
            ls = slice(lb * LANES, (lb + 1) * LANES)
            acc = [None] * grp
            for k in range(taps):
                wv = wbc_ref[jc, k, :, ls]
                for gi in range(grp):
                    r0 = base + SUBLANES * gi + k + lead
                    t = wv * cext_ref[jc, pl.ds(lbs * r0 + lb, SUBLANES, stride=lbs), :]
                    acc[gi] = t if acc[gi] is None else acc[gi] + t
            for gi in range(grp):
                y_ref[jc, base + SUBLANES * gi:base + SUBLANES * (gi + 1), ls] = acc[gi]


def _conv_prompt_kernel(x_ref, wa_ref, wb_ref, wg_ref, cw_ref, cb_ref, lng_ref, lnb_ref, gpre_ref,
                        gpost_ref, wout_ref, h_ref, st_ref,
                        u_ref, cext_ref, wbc_ref, y_ref, sg_ref, act_ref, *, tm, cn, taps, grp):
    i = pl.program_id(1)
    j = pl.program_id(2)
    n_j = pl.num_programs(2)
    n_chunks = cext_ref.shape[0]
    lbs = cn // LANES

    @pl.when(j == 0)
    def _():
        _pre_norm_to_bf16(x_ref, gpre_ref, u_ref, tm)

    @pl.when(i == 0)
    def _():
        cext_ref[j, 0:lbs * HALO, :] = jnp.zeros((lbs * HALO, LANES), F32)

    @pl.when(i > 0)
    def _():
        cext_ref[j, 0:lbs * HALO, :] = cext_ref[j, lbs * tm:lbs * (tm + HALO), :]

    @pl.when((pl.program_id(0) == 0) & (i == 0) & (j == 0))
    def _():
        for jj in range(n_chunks):
            for k in range(taps):
                wbc_ref[jj, k] = jnp.broadcast_to(cw_ref[k:k + 1, jj * cn:(jj + 1) * cn], (SUBLANES, cn))

    c, sg = _glu_chunk(u_ref, wa_ref, wb_ref, wg_ref)
    sg_ref[j] = sg
    for lb in range(lbs):
        cext_ref[j, pl.ds(lbs * HALO + lb, tm, stride=lbs), :] = c[:, lb * LANES:(lb + 1) * LANES]
    _conv_chunk(cext_ref, wbc_ref, y_ref, j, tm=tm, cn=cn, taps=taps, grp=grp)

    @pl.when(j == n_j - 1)
    def _():
        _conv_finalize(n_chunks, tm, cn, y_ref, sg_ref, act_ref, x_ref, h_ref, cb_ref,
                       lng_ref, lnb_ref, gpost_ref, wout_ref)

        @pl.when(i == pl.num_programs(1) - 1)
        def _():
            tail = lbs * (HALO + tm - (taps - 1))
            for jj in range(n_chunks):
                for lb in range(lbs):
                    c0 = jj * cn + lb * LANES
                    st_ref[:, c0:c0 + LANES] = cext_ref[jj, pl.ds(tail + lb, taps - 1, stride=lbs), :]


def _conv_sample_kernel(x_ref, st_in_ref, wa_ref, wb_ref, wg_ref, cw_ref, cb_ref, lng_ref, lnb_ref,
                        gpre_ref, gpost_ref, wout_ref, h_ref, st_ref,
                        u_ref, c_ref, full_ref, wbc_ref, y_ref, sg_ref, act_ref, *, tm, cn, taps, t_new):
    j = pl.program_id(1)
    n_j = pl.num_programs(1)
    n_seq = tm // t_new
    hist = taps - 1

    @pl.when(j == 0)
    def _():
        _pre_norm_to_bf16(x_ref, gpre_ref, u_ref, tm)

    c, sg = _glu_chunk(u_ref, wa_ref, wb_ref, wg_ref)
    c_ref[...] = c
    sg_ref[j] = sg
    _broadcast_taps(cw_ref, wbc_ref, taps)

    lbs = cn // LANES
    n_par = full_ref.shape[0]

    def seq_body(sb, carry):
        for q in range(n_par):
            s = sb * n_par + q
            rows = pl.ds(pl.multiple_of(s * t_new, t_new), t_new)
            for lb in range(lbs):
                ls = slice(lb * LANES, (lb + 1) * LANES)
                full_ref[q, pl.ds(lb, hist, stride=lbs), :] = st_in_ref[s, :, ls]
                full_ref[q, pl.ds(lbs * hist + lb, t_new, stride=lbs), :] = c_ref[rows, ls]
        for lb in range(lbs):
            ls = slice(lb * LANES, (lb + 1) * LANES)
            acc = [None] * n_par
            for k in range(taps):
                wv = wbc_ref[k, :, ls]
                for q in range(n_par):
                    t = wv * full_ref[q, pl.ds(lbs * k + lb, t_new, stride=lbs), :]
                    acc[q] = t if acc[q] is None else acc[q] + t
            for q in range(n_par):
                s = sb * n_par + q
                y_ref[j, pl.ds(pl.multiple_of(s * t_new, t_new), t_new), ls] = acc[q]
                st_ref[s, :, ls] = full_ref[q, pl.ds(lbs * t_new + lb, hist, stride=lbs), :]
        return carry
    lax.fori_loop(0, n_seq // n_par, seq_body, 0)

    @pl.when(j == n_j - 1)
    def _():
        _conv_finalize(y_ref.shape[0], tm, cn, y_ref, sg_ref, act_ref, x_ref, h_ref, cb_ref,
                       lng_ref, lnb_ref, gpost_ref, wout_ref)


def _const_spec(shape, n_grid):
    zeros = (0,) * len(shape)
    return pl.BlockSpec(shape, lambda *_: zeros, pipeline_mode=pl.Buffered(1))


def _conv_layer_prompt(x, w_in, cw, cb, lng, lnb, gpre, gpost, w_out, *, tm, cn):
    b, t, d = x.shape
    taps = cw.shape[0]
    n_j = d // cn
    lbs = cn // LANES
    grp = 4
    kern = functools.partial(_conv_prompt_kernel, tm=tm, cn=cn, taps=taps, grp=grp)
    vec = lambda: _const_spec((1, d), 3)
    return pl.pallas_call(
        kern,
        grid=(b, t // tm, n_j),
        in_specs=[
            pl.BlockSpec((None, tm, d), lambda bb, i, j: (bb, i, 0)),
            pl.BlockSpec((1, d, cn), lambda bb, i, j: (0, 0, j)),
            pl.BlockSpec((1, d, cn), lambda bb, i, j: (0, 0, n_j + j)),
            pl.BlockSpec((1, d, cn), lambda bb, i, j: (0, 0, 2 * n_j + j)),
            _const_spec((taps, d), 3),
            vec(), vec(), vec(), vec(), vec(),
            _const_spec((d, d), 3),
        ],
        out_specs=[
            pl.BlockSpec((None, tm, d), lambda bb, i, j: (bb, i, 0)),
            pl.BlockSpec((None, taps - 1, d), lambda bb, i, j: (bb, 0, 0)),
        ],
        out_shape=[
            jax.ShapeDtypeStruct((b, t, d), F32),
            jax.ShapeDtypeStruct((b, taps - 1, d), F32),
        ],
        scratch_shapes=[
            pltpu.VMEM((tm, d), BF16),
            pltpu.VMEM((n_j, lbs * (HALO + tm), LANES), F32),
            pltpu.VMEM((n_j, taps, SUBLANES, cn), F32),
            pltpu.VMEM((n_j, tm, cn), F32),
            pltpu.VMEM((n_j, tm, cn), BF16),
            pltpu.VMEM((tm, d), BF16),
        ],
        compiler_params=pltpu.CompilerParams(
            dimension_semantics=("arbitrary", "arbitrary", "arbitrary"),
            vmem_limit_bytes=VMEM_LIMIT),
        name="conv_layer_prompt",
    )(x, w_in, w_in, w_in, cw, cb, lng, lnb, gpre, gpost, w_out)


def _conv_layer_sample(x, state, w_in, cw, cb, lng, lnb, gpre, gpost, w_out, *, n_seq_tile, cn):
    n_seq, t_new, d = x.shape
    taps = cw.shape[0]
    n_j = d // cn
    tm = n_seq_tile * t_new
    x2 = x.reshape(n_seq * t_new, d)
    kern = functools.partial(_conv_sample_kernel, tm=tm, cn=cn, taps=taps, t_new=t_new)
    vec = lambda: _const_spec((1, d), 2)
    h, st = pl.pallas_call(
        kern,
        grid=(n_seq // n_seq_tile, n_j),
        in_specs=[
            pl.BlockSpec((tm, d), lambda i, j: (i, 0)),
            pl.BlockSpec((n_seq_tile, taps - 1, cn), lambda i, j: (i, 0, j)),
            pl.BlockSpec((1, d, cn), lambda i, j: (0, 0, j)),
            pl.BlockSpec((1, d, cn), lambda i, j: (0, 0, n_j + j)),
            pl.BlockSpec((1, d, cn), lambda i, j: (0, 0, 2 * n_j + j)),
            pl.BlockSpec((taps, cn), lambda i, j: (0, j)),
            vec(), vec(), vec(), vec(), vec(),
            _const_spec((d, d), 2),
        ],
        out_specs=[
            pl.BlockSpec((tm, d), lambda i, j: (i, 0)),
            pl.BlockSpec((n_seq_tile, taps - 1, cn), lambda i, j: (i, 0, j)),
        ],
        out_shape=[
            jax.ShapeDtypeStruct((n_seq * t_new, d), F32),
            jax.ShapeDtypeStruct((n_seq, taps - 1, d), F32),
        ],
        scratch_shapes=[
            pltpu.VMEM((tm, d), BF16),
            pltpu.VMEM((tm, cn), F32),
            pltpu.VMEM((4, (cn // LANES) * (HALO + t_new), LANES), F32),
            pltpu.VMEM((taps, SUBLANES, cn), F32),
            pltpu.VMEM((n_j, tm, cn), F32),
            pltpu.VMEM((n_j, tm, cn), BF16),
            pltpu.VMEM((tm, d), BF16),
        ],
        compiler_params=pltpu.CompilerParams(
            dimension_semantics=("arbitrary", "arbitrary"),
            vmem_limit_bytes=VMEM_LIMIT),
        name="conv_layer_sample",
    )(x2, state, w_in, w_in, w_in, cw, cb, lng, lnb, gpre, gpost, w_out)
    return h.reshape(n_seq, t_new, d), st


def _attn_projections(h_ref, gkv_ref, gpre_ref, wkv_ref, win_ref, ukv_ref, u_ref, q_ref, sg_ref, tm,
                      n_chunk):
    d = h_ref.shape[-1]

    def body(r, c):
        rows = _rows(r, ROW_BLK)
        x = h_ref[rows, :]
        xn = x * _rms_scale(x)
        ukv_ref[rows, :] = (xn * gkv_ref[...]).astype(BF16)
        u_ref[rows, :] = (xn * gpre_ref[...]).astype(BF16)
        return c
    lax.fori_loop(0, tm // ROW_BLK, body, 0, unroll=2)

    kv = _dot(ukv_ref[...], wkv_ref[...])
    u = u_ref[...]
    cw = d // n_chunk
    scale = HEAD_DIM ** -0.5 * LOG2E
    for n in range(n_chunk):
        cols = slice(n * cw, (n + 1) * cw)
        q_ref[:, cols] = (_dot(u, win_ref[:, cols]) * scale).astype(BF16)
    for n in range(n_chunk):
        cols = slice(n * cw, (n + 1) * cw)
        sg_ref[:, cols] = jax.nn.silu(_dot(u, win_ref[:, d + n * cw:d + (n + 1) * cw])).astype(BF16)
    return kv


def _attn_output(act_ref, h_ref, gpost_ref, wout_ref, y_ref, tm):
    y_ref[...] = _dot(act_ref[...], wout_ref[...])

    def post_body(r, c):
        rows = [_rows(2 * r + e, ROW_BLK) for e in range(2)]
        os = [y_ref[rw, :] for rw in rows]
        res = [h_ref[rw, :] + o * _rms_scale(o) * gpost_ref[...] for rw, o in zip(rows, os)]
        for rw, v in zip(rows, res):
            y_ref[rw, :] = v
        return c
    lax.fori_loop(0, tm // (2 * ROW_BLK), post_body, 0)


def _attn_block(sink_ref, q_ref, sg_ref, kpad_ref, vpad_ref, act_ref, r0, *, n_kv, first):
    blk = WINDOW
    pair_w = 2 * HEAD_DIM
    n_pair = GROUP // 2
    qi = lax.broadcasted_iota(jnp.int32, (blk, blk), 0)
    kj = lax.broadcasted_iota(jnp.int32, (blk, blk), 1)
    own = kj <= qi
    low = lax.broadcasted_iota(jnp.int32, (blk, pair_w), 1) < HEAD_DIM
    rows = pl.ds(r0, blk)
    krows = pl.ds(r0, 2 * blk)
    for g in range(n_kv):
        qs = jnp.concatenate(
            [q_ref[rows, (g * n_pair + pp) * pair_w:(g * n_pair + pp + 1) * pair_w]
             for pp in range(n_pair)], axis=0)
        probs, stats = [], []
        for e in range(2):
            s = _dot_t(qs, kpad_ref[krows, (2 * g + e) * pair_w:(2 * g + e + 1) * pair_w])
            p_parts, st = [], []
            for pp in range(n_pair):
                sh = s[pp * blk:(pp + 1) * blk]
                sf = jnp.where(own, sh[:, blk:], -jnp.inf if first else sh[:, :blk])
                sink = sink_ref[g * GROUP + 2 * pp + e] * LOG2E
                m = jnp.maximum(jnp.max(sf, axis=-1, keepdims=True), sink)
                p = jnp.exp2(sf - m)
                p_parts.append(jnp.concatenate([jnp.where(own, 0.0, p), jnp.where(own, p, 0.0)],
                                               axis=1).astype(BF16))
                st.append(jnp.exp2(sink - m))
            probs.append(jnp.concatenate(p_parts, axis=0))
            stats.append(st)
        o = (_dot(probs[0], vpad_ref[krows, (2 * g) * 2 * pair_w:(2 * g + 1) * 2 * pair_w])
             + _dot(probs[1], vpad_ref[krows, (2 * g + 1) * 2 * pair_w:(2 * g + 2) * 2 * pair_w]))
        for pp in range(n_pair):
            oh = o[pp * blk:(pp + 1) * blk]
            den = oh[:, pair_w:] + jnp.where(low, stats[0][pp], stats[1][pp])
            c0 = (g * n_pair + pp) * pair_w
            act_ref[rows, c0:c0 + pair_w] = (
                oh[:, :pair_w] / den * sg_ref[rows, c0:c0 + pair_w].astype(F32)).astype(BF16)


def _store_padded_kv(kv, kpad_ref, vpad_ref, row0, n_kv):
    n = kv.shape[0]
    kvw = n_kv * HEAD_DIM
    pair_w = 2 * HEAD_DIM
    low = lax.broadcasted_iota(jnp.int32, (n, pair_w), 1) < HEAD_DIM
    halves = (low, jnp.logical_not(low))
    for gam in range(n_kv // 2):
        kc = kv[:, gam * pair_w:(gam + 1) * pair_w]
        vc = kv[:, kvw + gam * pair_w:kvw + (gam + 1) * pair_w]
        for side in range(2):
            g = 2 * gam + side
            k_here = jnp.where(halves[side], kc, 0.0)
            v_here = jnp.where(halves[side], vc, 0.0)
            k_other = pltpu.roll(k_here, HEAD_DIM, axis=1)
            v_other = pltpu.roll(v_here, HEAD_DIM, axis=1)
            for e in range(2):
                kc0 = (2 * g + e) * pair_w
                vc0 = (2 * g + e) * 2 * pair_w
                kpad_ref[row0:row0 + n, kc0:kc0 + pair_w] = (k_here if e == side else k_other).astype(BF16)
                vpad_ref[row0:row0 + n, vc0:vc0 + pair_w] = (v_here if e == side else v_other).astype(BF16)
                vpad_ref[row0:row0 + n, vc0 + pair_w:vc0 + 2 * pair_w] = (
                    jnp.where(halves[e], 1.0, 0.0).astype(BF16))


def _attn_prompt_kernel(sink_ref, h_ref, gkv_ref, gpre_ref, gpost_ref, wkv_ref, win_ref, wout_ref,
                        y_ref, ck_ref, cv_ref,
                        ukv_ref, u_ref, q_ref, sg_ref, kpad_ref, vpad_ref, act_ref, *, tm, n_kv):
    i = pl.program_id(1)
    n_i = pl.num_programs(1)
    kvw = n_kv * HEAD_DIM
    blk = WINDOW

    kv = _attn_projections(h_ref, gkv_ref, gpre_ref, wkv_ref, win_ref, ukv_ref, u_ref, q_ref, sg_ref,
                           tm, 4)

    @pl.when(i == 0)
    def _():
        kpad_ref[0:blk, :] = jnp.zeros((blk, kpad_ref.shape[1]), BF16)
        vpad_ref[0:blk, :] = jnp.zeros((blk, vpad_ref.shape[1]), BF16)

    @pl.when(i > 0)
    def _():
        kpad_ref[0:blk, :] = kpad_ref[tm:tm + blk, :]
        vpad_ref[0:blk, :] = vpad_ref[tm:tm + blk, :]

    _store_padded_kv(kv, kpad_ref, vpad_ref, blk, n_kv)

    @pl.when(i == n_i - 1)
    def _():
        ck_ref[...] = kv[tm - blk:, :kvw]
        cv_ref[...] = kv[tm - blk:, kvw:]

    block = functools.partial(_attn_block, sink_ref, q_ref, sg_ref, kpad_ref, vpad_ref, act_ref, n_kv=n_kv)

    @pl.when(i == 0)
    def _():
        block(0, first=True)

    @pl.when(i > 0)
    def _():
        block(0, first=False)

    def blk_body(bi, carry):
        block(pl.multiple_of(bi * blk, blk), first=False)
        return carry
    lax.fori_loop(1, tm // blk, blk_body, 0)

    _attn_output(act_ref, h_ref, gpost_ref, wout_ref, y_ref, tm)


def _attn_layer_prompt(h, sinks, gkv, gpre, gpost, w_kv, w_in, w_out, *, tm):
    b, t, d = h.shape
    kvw = w_kv.shape[1] // 2
    n_kv = kvw // HEAD_DIM
    kern = functools.partial(_attn_prompt_kernel, tm=tm, n_kv=n_kv)
    vec = lambda: _const_spec((1, d), 2)
    return pl.pallas_call(
        kern,
        grid=(b, t // tm),
        in_specs=[
            pl.BlockSpec(memory_space=pltpu.SMEM),
            pl.BlockSpec((None, tm, d), lambda bb, i: (bb, i, 0)),
            vec(), vec(), vec(),
            _const_spec(w_kv.shape, 2),
            _const_spec(w_in.shape, 2),
            _const_spec(w_out.shape, 2),
        ],
        out_specs=[
            pl.BlockSpec((None, tm, d), lambda bb, i: (bb, i, 0)),
            pl.BlockSpec((None, WINDOW, kvw), lambda bb, i: (bb, 0, 0)),
            pl.BlockSpec((None, WINDOW, kvw), lambda bb, i: (bb, 0, 0)),
        ],
        out_shape=[
            jax.ShapeDtypeStruct((b, t, d), F32),
            jax.ShapeDtypeStruct((b, WINDOW, kvw), F32),
            jax.ShapeDtypeStruct((b, WINDOW, kvw), F32),
        ],
        scratch_shapes=[
            pltpu.VMEM((tm, d), BF16),
            pltpu.VMEM((tm, d), BF16),
            pltpu.VMEM((tm, d), BF16),
            pltpu.VMEM((tm, d), BF16),
            pltpu.VMEM((WINDOW + tm, 4 * kvw), BF16),
            pltpu.VMEM((WINDOW + tm, 8 * kvw), BF16),
            pltpu.VMEM((tm, d), BF16),
        ],
        compiler_params=pltpu.CompilerParams(
            dimension_semantics=("arbitrary", "arbitrary"),
            vmem_limit_bytes=VMEM_LIMIT),
        name="attn_layer_prompt",
    )(sinks, h, gkv, gpre, gpost, w_kv, w_in, w_out)


def _attn_sample_proj_kernel(h_ref, gkv_ref, gpre_ref, wkv_ref, win_ref, kv_ref, q_ref, sg_ref,
                             ukv_ref, u_ref, *, tm):
    kv_ref[...] = _attn_projections(h_ref, gkv_ref, gpre_ref, wkv_ref, win_ref, ukv_ref, u_ref, q_ref,
                                    sg_ref, tm, 4)


def _attn_sample_core_kernel(sink_ref, q_ref, sg_ref, kv_ref, ck_in_ref, cv_in_ref,
                             act_ref, ck_ref, cv_ref, *, n_seq, t_new, n_kv, n_par):
    kvw = n_kv * HEAD_DIM
    w_buf = ck_in_ref.shape[1]
    n_keys = w_buf + 2 * t_new
    n_rows = GROUP * n_kv * t_new
    tq = lax.broadcasted_iota(jnp.int32, (n_rows, n_keys), 0) % t_new
    kj = lax.broadcasted_iota(jnp.int32, (n_rows, n_keys), 1)
    mask = ((kj < w_buf) & (kj > tq)) | ((kj >= w_buf) & (kj - w_buf <= tq))
    row_blk = lax.broadcasted_iota(jnp.int32, (n_rows, 1), 0) // t_new
    sinkv = jnp.zeros((n_rows, 1), F32)
    for hh in range(GROUP):
        for g in range(n_kv):
            sinkv = jnp.where(row_blk == hh * n_kv + g, sink_ref[g * GROUP + hh] * LOG2E, sinkv)
    lane_kv = lax.broadcasted_iota(jnp.int32, (t_new, kvw), 1) // HEAD_DIM

    def one_seq(s):
        rows = pl.ds(pl.multiple_of(s * t_new, t_new), t_new)
        kc = ck_in_ref[s]
        vc = cv_in_ref[s]
        kvn = kv_ref[rows, :]
        ck_ref[s, 0:w_buf - t_new, :] = kc[t_new:]
        cv_ref[s, 0:w_buf - t_new, :] = vc[t_new:]
        ck_ref[s, w_buf - t_new:w_buf, :] = kvn[:, :kvw]
        cv_ref[s, w_buf - t_new:w_buf, :] = kvn[:, kvw:]
        pad = jnp.zeros((t_new, kvw), F32)
        k_all = jnp.concatenate([kc, kvn[:, :kvw], pad], axis=0).astype(BF16)
        v_all = jnp.concatenate([vc, kvn[:, kvw:], pad], axis=0).astype(BF16)
        qparts = []
        for hh in range(GROUP):
            slab = q_ref[rows, hh * kvw:(hh + 1) * kvw].astype(F32)
            for g in range(n_kv):
                qparts.append(jnp.where(lane_kv == g, slab, 0.0))
        qbd = jnp.concatenate(qparts, axis=0).astype(BF16)
        sc = jnp.where(mask, _dot_t(qbd, k_all), -jnp.inf)
        m = jnp.maximum(jnp.max(sc, axis=-1, keepdims=True), sinkv)
        p = jnp.exp2(sc - m)
        den = jnp.sum(p, axis=-1, keepdims=True) + jnp.exp2(sinkv - m)
        o = _dot(p.astype(BF16), v_all) * (1.0 / den)
        for hh in range(GROUP):
            slab = None
            for g in range(n_kv - 1, -1, -1):
                r = (hh * n_kv + g) * t_new
                part = o[r:r + t_new]
                slab = part if slab is None else jnp.where(lane_kv == g, part, slab)
            cols = slice(hh * kvw, (hh + 1) * kvw)
            act_ref[rows, cols] = (slab * sg_ref[rows, cols].astype(F32)).astype(BF16)

    def seq_body(sb, carry):
        for q in range(n_par):
            one_seq(sb * n_par + q)
        return carry
    lax.fori_loop(0, n_seq // n_par, seq_body, 0)


def _attn_sample_out_kernel(act_ref, h_ref, gpost_ref, wout_ref, y_ref, *, tm):
    _attn_output(act_ref, h_ref, gpost_ref, wout_ref, y_ref, tm)


def _attn_layer_sample(h, cache_k, cache_v, sinks, gkv, gpre, gpost, w_kv, w_in, w_out, *, tm,
                       n_seq_tile):
    n_seq, t_new, d = h.shape
    kvw = w_kv.shape[1] // 2
    n_kv = kvw // HEAD_DIM
    w_buf = cache_k.shape[1]
    n_tok = n_seq * t_new
    h2 = h.reshape(n_tok, d)
    vec = lambda: _const_spec((1, d), 1)
    params = pltpu.CompilerParams(dimension_semantics=("arbitrary",), vmem_limit_bytes=VMEM_LIMIT)

    kv, q, sg = pl.pallas_call(
        functools.partial(_attn_sample_proj_kernel, tm=tm),
        grid=(n_tok // tm,),
        in_specs=[
            pl.BlockSpec((tm, d), lambda i: (i, 0)),
            vec(), vec(),
            _const_spec(w_kv.shape, 1),
            _const_spec(w_in.shape, 1),
        ],
        out_specs=[
            pl.BlockSpec((tm, 2 * kvw), lambda i: (i, 0)),
            pl.BlockSpec((tm, d), lambda i: (i, 0)),
            pl.BlockSpec((tm, d), lambda i: (i, 0)),
        ],
        out_shape=[
            jax.ShapeDtypeStruct((n_tok, 2 * kvw), F32),
            jax.ShapeDtypeStruct((n_tok, d), BF16),
            jax.ShapeDtypeStruct((n_tok, d), BF16),
        ],
        scratch_shapes=[pltpu.VMEM((tm, d), BF16), pltpu.VMEM((tm, d), BF16)],
        compiler_params=params,
        name="attn_sample_proj",
    )(h2, gkv, gpre, w_kv, w_in)

    def to_slabs(a):
        return a.reshape(n_tok, n_kv, GROUP, HEAD_DIM).transpose(0, 2, 1, 3).reshape(n_tok, d)

    def from_slabs(a):
        return a.reshape(n_tok, GROUP, n_kv, HEAD_DIM).transpose(0, 2, 1, 3).reshape(n_tok, d)

    q, sg = to_slabs(q), to_slabs(sg)
    rows = n_seq_tile * t_new
    act, ck, cv = pl.pallas_call(
        functools.partial(_attn_sample_core_kernel, n_seq=n_seq_tile, t_new=t_new, n_kv=n_kv, n_par=4),
        grid=(n_seq // n_seq_tile,),
        in_specs=[
            pl.BlockSpec(memory_space=pltpu.SMEM),
            pl.BlockSpec((rows, d), lambda i: (i, 0)),
            pl.BlockSpec((rows, d), lambda i: (i, 0)),
            pl.BlockSpec((rows, 2 * kvw), lambda i: (i, 0)),
            pl.BlockSpec((n_seq_tile, w_buf, kvw), lambda i: (i, 0, 0)),
            pl.BlockSpec((n_seq_tile, w_buf, kvw), lambda i: (i, 0, 0)),
        ],
        out_specs=[
            pl.BlockSpec((rows, d), lambda i: (i, 0)),
            pl.BlockSpec((n_seq_tile, w_buf, kvw), lambda i: (i, 0, 0)),
            pl.BlockSpec((n_seq_tile, w_buf, kvw), lambda i: (i, 0, 0)),
        ],
        out_shape=[
            jax.ShapeDtypeStruct((n_tok, d), BF16),
            jax.ShapeDtypeStruct((n_seq, w_buf, kvw), F32),
            jax.ShapeDtypeStruct((n_seq, w_buf, kvw), F32),
        ],
        compiler_params=params,
        name="attn_sample_core",
    )(sinks, q, sg, kv, cache_k, cache_v)

    y = pl.pallas_call(
        functools.partial(_attn_sample_out_kernel, tm=tm),
        grid=(n_tok // tm,),
        in_specs=[
            pl.BlockSpec((tm, d), lambda i: (i, 0)),
            pl.BlockSpec((tm, d), lambda i: (i, 0)),
            vec(),
            _const_spec(w_out.shape, 1),
        ],
        out_specs=pl.BlockSpec((tm, d), lambda i: (i, 0)),
        out_shape=jax.ShapeDtypeStruct((n_tok, d), F32),
        compiler_params=params,
        name="attn_sample_out",
    )(from_slabs(act), h2, gpost, w_out)
    return y.reshape(n_seq, t_new, d), ck, cv


def kernel(x_prompt, x_sample, state_conv, cache_k, cache_v, norm_pre, norm_post, w_in_a, conv_w, conv_b, ln_g, ln_b, w_out_a, kv_norm, w_kv, w_in_b, sinks, w_out_b):
    n_a = w_in_a.shape[0]
    assert n_a == 1 and w_in_b.shape[0] == 1 and norm_pre.shape[0] == 2
    d = x_prompt.shape[-1]
    n_seq, w_buf, n_kv, hd = cache_k.shape
    assert hd == HEAD_DIM and w_buf == WINDOW

    row = lambda v: v.reshape(1, -1)
    w_in_a_bf = w_in_a[0].astype(BF16)
    w_out_a_bf = w_out_a[0].astype(BF16)
    w_kv_bf = w_kv.astype(BF16)
    w_in_b_bf = w_in_b[0].astype(BF16)
    w_out_b_bf = w_out_b[0].astype(BF16)
    conv_args = (w_in_a_bf[None], conv_w[0], row(conv_b[0]), row(ln_g[0]), row(ln_b[0]), row(norm_pre[0]),
                 row(norm_post[0]), w_out_a_bf)
    attn_args = (sinks[0], row(kv_norm), row(norm_pre[1]), row(norm_post[1]), w_kv_bf, w_in_b_bf,
                 w_out_b_bf)

    h_p, st_p = _conv_layer_prompt(x_prompt, *conv_args, tm=512, cn=256)
    y_p, ck_p, cv_p = _attn_layer_prompt(h_p, *attn_args, tm=256)

    h_s, st_s = _conv_layer_sample(x_sample, state_conv[0], *conv_args, n_seq_tile=64, cn=256)
    y_s, ck_s, cv_s = _attn_layer_sample(
        h_s, cache_k.reshape(n_seq, w_buf, n_kv * hd), cache_v.reshape(n_seq, w_buf, n_kv * hd),
        *attn_args, tm=512, n_seq_tile=16)

    b = x_prompt.shape[0]
    return (y_p, y_s, st_p[None], ck_p.reshape(b, w_buf, n_kv, hd), cv_p.reshape(b, w_buf, n_kv, hd),
            st_s[None], ck_s.reshape(n_seq, w_buf, n_kv, hd), cv_s.reshape(n_seq, w_buf, n_kv, hd))
```

```python
import functools

import jax
import jax.numpy as jnp
from jax import lax
from jax.experimental import pallas as pl
from jax.experimental.pallas import tpu as pltpu

RMS_EPS = 1e-6
LN_EPS = 1e-5
HEAD_DIM = 64
GROUP = 8
WINDOW = 128
SUBLANES = 8
LANES = 128
LOG2E = 1.4426950408889634
HALO = 32
VMEM_LIMIT = 56 * 1024 * 1024
ROW_BLK = 64

BF16 = jnp.bfloat16
F32 = jnp.float32


def _rows(i, n):
    return pl.ds(pl.multiple_of(i * n, n), n)


def _rms_scale(x):
    return lax.rsqrt(jnp.mean(x * x, axis=-1, keepdims=True) + RMS_EPS)


def _dot(a, b):
    return jnp.dot(a, b, preferred_element_type=F32)


def _dot_t(a, b):
    return lax.dot_general(a, b, (((1,), (1,)), ((), ())), preferred_element_type=F32)


def _pre_norm_to_bf16(x_ref, g_ref, u_ref, tm):
    def body(r, c):
        rows = _rows(r, ROW_BLK)
        x = x_ref[rows, :]
        u_ref[rows, :] = (x * _rms_scale(x) * g_ref[...]).astype(BF16)
        return c
    lax.fori_loop(0, tm // ROW_BLK, body, 0, unroll=2)


def _glu_chunk(u_ref, wa_ref, wb_ref, wg_ref):
    u = u_ref[...]
    c = _dot(u, wa_ref[...]) * jax.nn.sigmoid(_dot(u, wb_ref[...]))
    sg = jax.nn.silu(_dot(u, wg_ref[...])).astype(BF16)
    return c, sg


def _broadcast_taps(cw_ref, wbc_ref, taps):
    for k in range(taps):
        wbc_ref[k] = jnp.broadcast_to(cw_ref[k:k + 1, :], wbc_ref.shape[1:])


def _conv_finalize(j_chunks, tm, cn, y_ref, sg_ref, act_ref, x_ref, h_ref, cb_ref, lng_ref, lnb_ref,
                   gpost_ref, wout_ref):
    d = j_chunks * cn

    ln_rows = ROW_BLK

    def ln_body(r, c):
        rows = _rows(r, ln_rows)
        ys = [y_ref[jj, rows, :] + cb_ref[:, jj * cn:(jj + 1) * cn] for jj in range(j_chunks)]
        mu = jnp.sum(sum(ys), axis=-1, keepdims=True) * (1.0 / d)
        yc = [y - mu for y in ys]
        var = jnp.sum(sum(y * y for y in yc), axis=-1, keepdims=True) * (1.0 / d)
        rstd = lax.rsqrt(var + LN_EPS)
        for jj in range(j_chunks):
            cols = slice(jj * cn, (jj + 1) * cn)
            t = jax.nn.silu(yc[jj] * rstd * lng_ref[:, cols] + lnb_ref[:, cols])
            act_ref[rows, cols] = (t * sg_ref[jj, rows, :].astype(F32)).astype(BF16)
        return c
    lax.fori_loop(0, tm // ln_rows, ln_body, 0, unroll=2)

    h_ref[...] = _dot(act_ref[...], wout_ref[...])

    def post_body(r, c):
        rows = [_rows(2 * r + e, ROW_BLK) for e in range(2)]
        os = [h_ref[rw, :] for rw in rows]
        res = [x_ref[rw, :] + o * _rms_scale(o) * gpost_ref[...] for rw, o in zip(rows, os)]
        for rw, v in zip(rows, res):
            h_ref[rw, :] = v
        return c
    lax.fori_loop(0, tm // (2 * ROW_BLK), post_body, 0)


def _conv_chunk(cext_ref, wbc_ref, y_ref, jc, *, tm, cn, taps, grp):
    lbs = cn // LANES
    assert lbs >= 2, "the row-interleaved layout needs at least two lane blocks per chunk"
    lead = HALO - (taps - 1)
    for g in range(tm // (grp * SUBLANES)):
        base = g * grp * SUBLANES
        for lb in range(lbs):
            ls = slice(lb * LANES, (lb + 1) * LANES)
            acc = [None] * grp
            for k in range(taps):
                wv = wbc_ref[jc, k, :, ls]
                for gi in range(grp):
                    r0 = base + SUBLANES * gi + k + lead
                    t = wv * cext_ref[jc, pl.ds(lbs * r0 + lb, SUBLANES, stride=lbs), :]
                    acc[gi] = t if acc[gi] is None else acc[gi] + t
            for gi in range(grp):
                y_ref[jc, base + SUBLANES * gi:base + SUBLANES * (gi + 1), ls] = acc[gi]


def _conv_prompt_kernel(x_ref, w_hbm, cw_ref, cb_ref, lng_ref, lnb_ref, gpre_ref, gpost_ref, wout_ref,
                        h_ref, st_ref,
                        u_ref, wbuf_ref, wsem, cext_ref, wbc_ref, y_ref, sg_ref, act_ref,
                        *, tm, cn, taps, grp):
    bb = pl.program_id(0)
    i = pl.program_id(1)
    n_chunks = cext_ref.shape[0]
    lbs = cn // LANES
    first_step = (bb == 0) & (i == 0)
    last_step = (bb == pl.num_programs(0) - 1) & (i == pl.num_programs(1) - 1)

    def weight_copy(jc, slot, part):
        col0 = pl.multiple_of((part * n_chunks + jc) * cn, cn)
        return pltpu.make_async_copy(w_hbm.at[:, pl.ds(col0, cn)], wbuf_ref.at[slot, part],
                                     wsem.at[slot, part])

    @pl.when(first_step)
    def _():
        for part in range(3):
            weight_copy(0, 0, part).start()
        for jj in range(n_chunks):
            for k in range(taps):
                wbc_ref[jj, k] = jnp.broadcast_to(cw_ref[k:k + 1, jj * cn:(jj + 1) * cn], (SUBLANES, cn))

    _pre_norm_to_bf16(x_ref, gpre_ref, u_ref, tm)

    def chunk_body(j, carry):
        slot = j % 2
        for part in range(3):
            weight_copy(j, slot, part).wait()

        @pl.when(jnp.logical_not(last_step & (j == n_chunks - 1)))
        def _():
            nxt = jnp.where(j == n_chunks - 1, 0, j + 1)
            for part in range(3):
                weight_copy(nxt, 1 - slot, part).start()

        @pl.when(i == 0)
        def _():
            cext_ref[j, 0:lbs * HALO, :] = jnp.zeros((lbs * HALO, LANES), F32)

        @pl.when(i > 0)
        def _():
            cext_ref[j, 0:lbs * HALO, :] = cext_ref[j, lbs * tm:lbs * (tm + HALO), :]

        u = u_ref[...]
        c = _dot(u, wbuf_ref[slot, 0]) * jax.nn.sigmoid(_dot(u, wbuf_ref[slot, 1]))
        sg_ref[j] = jax.nn.silu(_dot(u, wbuf_ref[slot, 2])).astype(BF16)
        for lb in range(lbs):
            cext_ref[j, pl.ds(lbs * HALO + lb, tm, stride=lbs), :] = c[:, lb * LANES:(lb + 1) * LANES]
        _conv_chunk(cext_ref, wbc_ref, y_ref, j, tm=tm, cn=cn, taps=taps, grp=grp)
        return carry
    lax.fori_loop(0, n_chunks, chunk_body, 0)

    _conv_finalize(n_chunks, tm, cn, y_ref, sg_ref, act_ref, x_ref, h_ref, cb_ref,
                   lng_ref, lnb_ref, gpost_ref, wout_ref)

    @pl.when(i == pl.num_programs(1) - 1)
    def _():
        tail = lbs * (HALO + tm - (taps - 1))
        for jj in range(n_chunks):
            for lb in range(lbs):
                c0 = jj * cn + lb * LANES
                st_ref[:, c0:c0 + LANES] = cext_ref[jj, pl.ds(tail + lb, taps - 1, stride=lbs), :]


def _conv_sample_kernel(x_ref, st_in_ref, wa_ref, wb_ref, wg_ref, cw_ref, cb_ref, lng_ref, lnb_ref,
                        gpre_ref, gpost_ref, wout_ref, h_ref, st_ref,
                        u_ref, c_ref, full_ref, wbc_ref, y_ref, sg_ref, act_ref, *, tm, cn, taps, t_new):
    j = pl.program_id(1)
    n_j = pl.num_programs(1)
    n_seq = tm // t_new
    hist = taps - 1

    @pl.when(j == 0)
    def _():
        _pre_norm_to_bf16(x_ref, gpre_ref, u_ref, tm)

    c, sg = _glu_chunk(u_ref, wa_ref, wb_ref, wg_ref)
    c_ref[...] = c
    sg_ref[j] = sg
    _broadcast_taps(cw_ref, wbc_ref, taps)

    lbs = cn // LANES
    n_par = full_ref.shape[0]

    def seq_body(sb, carry):
        for q in range(n_par):
            s = sb * n_par + q
            rows = pl.ds(pl.multiple_of(s * t_new, t_new), t_new)
            for lb in range(lbs):
                ls = slice(lb * LANES, (lb + 1) * LANES)
                full_ref[q, pl.ds(lb, hist, stride=lbs), :] = st_in_ref[s, :, ls]
                full_ref[q, pl.ds(lbs * hist + lb, t_new, stride=lbs), :] = c_ref[rows, ls]
        for lb in range(lbs):
            ls = slice(lb * LANES, (lb + 1) * LANES)
            acc = [None] * n_par
            for k in range(taps):
                wv = wbc_ref[k, :, ls]
                for q in range(n_par):
                    t = wv * full_ref[q, pl.ds(lbs * k + lb, t_new, stride=lbs), :]
                    acc[q] = t if acc[q] is None else acc[q] + t
            for q in range(n_par):
                s = sb * n_par + q
                y_ref[j, pl.ds(pl.multiple_of(s * t_new, t_new), t_new), ls] = acc[q]
                st_ref[s, :, ls] = full_ref[q, pl.ds(lbs * t_new + lb, hist, stride=lbs), :]
        return carry
    lax.fori_loop(0, n_seq // n_par, seq_body, 0)

    @pl.when(j == n_j - 1)
    def _():
        _conv_finalize(y_ref.shape[0], tm, cn, y_ref, sg_ref, act_ref, x_ref, h_ref, cb_ref,
                       lng_ref, lnb_ref, gpost_ref, wout_ref)


def _const_spec(shape, n_grid):
    zeros = (0,) * len(shape)
    return pl.BlockSpec(shape, lambda *_: zeros, pipeline_mode=pl.Buffered(1))


def _conv_layer_prompt(x, w_in, cw, cb, lng, lnb, gpre, gpost, w_out, *, tm, cn):
    b, t, d = x.shape
    taps = cw.shape[0]
    n_j = d // cn
    lbs = cn // LANES
    grp = 4
    kern = functools.partial(_conv_prompt_kernel, tm=tm, cn=cn, taps=taps, grp=grp)
    vec = lambda: _const_spec((1, d), 2)
    return pl.pallas_call(
        kern,
        grid=(b, t // tm),
        in_specs=[
            pl.BlockSpec((None, tm, d), lambda bb, i: (bb, i, 0)),
            pl.BlockSpec(memory_space=pl.ANY),
            _const_spec((taps, d), 2),
            vec(), vec(), vec(), vec(), vec(),
            _const_spec((d, d), 2),
        ],
        out_specs=[
            pl.BlockSpec((None, tm, d), lambda bb, i: (bb, i, 0)),
            pl.BlockSpec((None, taps - 1, d), lambda bb, i: (bb, 0, 0)),
        ],
        out_shape=[
            jax.ShapeDtypeStruct((b, t, d), F32),
            jax.ShapeDtypeStruct((b, taps - 1, d), F32),
        ],
        scratch_shapes=[
            pltpu.VMEM((tm, d), BF16),
            pltpu.VMEM((2, 3, d, cn), BF16),
            pltpu.SemaphoreType.DMA((2, 3)),
            pltpu.VMEM((n_j, lbs * (HALO + tm), LANES), F32),
            pltpu.VMEM((n_j, taps, SUBLANES, cn), F32),
            pltpu.VMEM((n_j, tm, cn), F32),
            pltpu.VMEM((n_j, tm, cn), BF16),
            pltpu.VMEM((tm, d), BF16),
        ],
        compiler_params=pltpu.CompilerParams(
            dimension_semantics=("arbitrary", "arbitrary"),
            vmem_limit_bytes=VMEM_LIMIT),
        name="conv_layer_prompt",
    )(x, w_in, cw, cb, lng, lnb, gpre, gpost, w_out)


def _conv_layer_sample(x, state, w_in, cw, cb, lng, lnb, gpre, gpost, w_out, *, n_seq_tile, cn):
    n_seq, t_new, d = x.shape
    taps = cw.shape[0]
    n_j = d // cn
    tm = n_seq_tile * t_new
    x2 = x.reshape(n_seq * t_new, d)
    kern = functools.partial(_conv_sample_kernel, tm=tm, cn=cn, taps=taps, t_new=t_new)
    vec = lambda: _const_spec((1, d), 2)
    h, st = pl.pallas_call(
        kern,
        grid=(n_seq // n_seq_tile, n_j),
        in_specs=[
            pl.BlockSpec((tm, d), lambda i, j: (i, 0)),
            pl.BlockSpec((n_seq_tile, taps - 1, cn), lambda i, j: (i, 0, j)),
            pl.BlockSpec((d, cn), lambda i, j: (0, j)),
            pl.BlockSpec((d, cn), lambda i, j: (0, n_j + j)),
            pl.BlockSpec((d, cn), lambda i, j: (0, 2 * n_j + j)),
            pl.BlockSpec((taps, cn), lambda i, j: (0, j)),
            vec(), vec(), vec(), vec(), vec(),
            _const_spec((d, d), 2),
        ],
        out_specs=[
            pl.BlockSpec((tm, d), lambda i, j: (i, 0)),
            pl.BlockSpec((n_seq_tile, taps - 1, cn), lambda i, j: (i, 0, j)),
        ],
        out_shape=[
            jax.ShapeDtypeStruct((n_seq * t_new, d), F32),
            jax.ShapeDtypeStruct((n_seq, taps - 1, d), F32),
        ],
        scratch_shapes=[
            pltpu.VMEM((tm, d), BF16),
            pltpu.VMEM((tm, cn), F32),
            pltpu.VMEM((4, (cn // LANES) * (HALO + t_new), LANES), F32),
            pltpu.VMEM((taps, SUBLANES, cn), F32),
            pltpu.VMEM((n_j, tm, cn), F32),
            pltpu.VMEM((n_j, tm, cn), BF16),
            pltpu.VMEM((tm, d), BF16),
        ],
        compiler_params=pltpu.CompilerParams(
            dimension_semantics=("arbitrary", "arbitrary"),
            vmem_limit_bytes=VMEM_LIMIT),
        name="conv_layer_sample",
    )(x2, state, w_in, w_in, w_in, cw, cb, lng, lnb, gpre, gpost, w_out)
    return h.reshape(n_seq, t_new, d), st


def _attn_projections(h_ref, gkv_ref, gpre_ref, wkv_ref, win_ref, ukv_ref, u_ref, q_ref, sg_ref, tm,
                      n_chunk):
    d = h_ref.shape[-1]

    def body(r, c):
        rows = _rows(r, ROW_BLK)
        x = h_ref[rows, :]
        xn = x * _rms_scale(x)
        ukv_ref[rows, :] = (xn * gkv_ref[...]).astype(BF16)
        u_ref[rows, :] = (xn * gpre_ref[...]).astype(BF16)
        return c
    lax.fori_loop(0, tm // ROW_BLK, body, 0, unroll=2)

    kv = _dot(ukv_ref[...], wkv_ref[...])
    u = u_ref[...]
    cw = d // n_chunk
    scale = HEAD_DIM ** -0.5 * LOG2E
    for n in range(n_chunk):
        cols = slice(n * cw, (n + 1) * cw)
        q_ref[:, cols] = (_dot(u, win_ref[:, cols]) * scale).astype(BF16)
    for n in range(n_chunk):
        cols = slice(n * cw, (n + 1) * cw)
        sg_ref[:, cols] = jax.nn.silu(_dot(u, win_ref[:, d + n * cw:d + (n + 1) * cw])).astype(BF16)
    return kv


def _attn_output(act_ref, h_ref, gpost_ref, wout_ref, y_ref, tm):
    y_ref[...] = _dot(act_ref[...], wout_ref[...])

    def post_body(r, c):
        rows = [_rows(2 * r + e, ROW_BLK) for e in range(2)]
        os = [y_ref[rw, :] for rw in rows]
        res = [h_ref[rw, :] + o * _rms_scale(o) * gpost_ref[...] for rw, o in zip(rows, os)]
        for rw, v in zip(rows, res):
            y_ref[rw, :] = v
        return c
    lax.fori_loop(0, tm // (2 * ROW_BLK), post_body, 0)


def _attn_block(sink_ref, q_ref, sg_ref, kpad_ref, vpad_ref, act_ref, r0, *, n_kv, first):
    blk = WINDOW
    pair_w = 2 * HEAD_DIM
    n_pair = GROUP // 2
    qi = lax.broadcasted_iota(jnp.int32, (blk, blk), 0)
    kj = lax.broadcasted_iota(jnp.int32, (blk, blk), 1)
    own = kj <= qi
    low = lax.broadcasted_iota(jnp.int32, (blk, pair_w), 1) < HEAD_DIM
    rows = pl.ds(r0, blk)
    krows = pl.ds(r0, 2 * blk)
    for g in range(n_kv):
        qs = jnp.concatenate(
            [q_ref[rows, (g * n_pair + pp) * pair_w:(g * n_pair + pp + 1) * pair_w]
             for pp in range(n_pair)], axis=0)
        probs, stats = [], []
        for e in range(2):
            s = _dot_t(qs, kpad_ref[krows, (2 * g + e) * pair_w:(2 * g + e + 1) * pair_w])
            p_parts, st = [], []
            for pp in range(n_pair):
                sh = s[pp * blk:(pp + 1) * blk]
                sf = jnp.where(own, sh[:, blk:], -jnp.inf if first else sh[:, :blk])
                sink = sink_ref[g * GROUP + 2 * pp + e] * LOG2E
                m = jnp.maximum(jnp.max(sf, axis=-1, keepdims=True), sink)
                p = jnp.exp2(sf - m)
                p_parts.append(jnp.concatenate([jnp.where(own, 0.0, p), jnp.where(own, p, 0.0)],
                                               axis=1).astype(BF16))
                st.append(jnp.exp2(sink - m))
            probs.append(jnp.concatenate(p_parts, axis=0))
            stats.append(st)
        o = (_dot(probs[0], vpad_ref[krows, (2 * g) * 2 * pair_w:(2 * g + 1) * 2 * pair_w])
             + _dot(probs[1], vpad_ref[krows, (2 * g + 1) * 2 * pair_w:(2 * g + 2) * 2 * pair_w]))
        for pp in range(n_pair):
            oh = o[pp * blk:(pp + 1) * blk]
            den = oh[:, pair_w:] + jnp.where(low, stats[0][pp], stats[1][pp])
            c0 = (g * n_pair + pp) * pair_w
            act_ref[rows, c0:c0 + pair_w] = (
                oh[:, :pair_w] / den * sg_ref[rows, c0:c0 + pair_w].astype(F32)).astype(BF16)


def _store_padded_kv(kv, kpad_ref, vpad_ref, row0, n_kv):
    n = kv.shape[0]
    kvw = n_kv * HEAD_DIM
    pair_w = 2 * HEAD_DIM
    low = lax.broadcasted_iota(jnp.int32, (n, pair_w), 1) < HEAD_DIM
    halves = (low, jnp.logical_not(low))
    for gam in range(n_kv // 2):
        kc = kv[:, gam * pair_w:(gam + 1) * pair_w]
        vc = kv[:, kvw + gam * pair_w:kvw + (gam + 1) * pair_w]
        for side in range(2):
            g = 2 * gam + side
            k_here = jnp.where(halves[side], kc, 0.0)
            v_here = jnp.where(halves[side], vc, 0.0)
            k_other = pltpu.roll(k_here, HEAD_DIM, axis=1)
            v_other = pltpu.roll(v_here, HEAD_DIM, axis=1)
            for e in range(2):
                kc0 = (2 * g + e) * pair_w
                vc0 = (2 * g + e) * 2 * pair_w
                kpad_ref[row0:row0 + n, kc0:kc0 + pair_w] = (k_here if e == side else k_other).astype(BF16)
                vpad_ref[row0:row0 + n, vc0:vc0 + pair_w] = (v_here if e == side else v_other).astype(BF16)
                vpad_ref[row0:row0 + n, vc0 + pair_w:vc0 + 2 * pair_w] = (
                    jnp.where(halves[e], 1.0, 0.0).astype(BF16))


def _attn_prompt_kernel(sink_ref, h_ref, gkv_ref, gpre_ref, gpost_ref, wkv_ref, win_ref, wout_ref,
                        y_ref, ck_ref, cv_ref,
                        ukv_ref, u_ref, q_ref, sg_ref, kpad_ref, vpad_ref, act_ref, *, tm, n_kv):
    i = pl.program_id(1)
    n_i = pl.num_programs(1)
    kvw = n_kv * HEAD_DIM
    blk = WINDOW

    kv = _attn_projections(h_ref, gkv_ref, gpre_ref, wkv_ref, win_ref, ukv_ref, u_ref, q_ref, sg_ref,
                           tm, 4)

    @pl.when(i == 0)
    def _():
        kpad_ref[0:blk, :] = jnp.zeros((blk, kpad_ref.shape[1]), BF16)
        vpad_ref[0:blk, :] = jnp.zeros((blk, vpad_ref.shape[1]), BF16)

    @pl.when(i > 0)
    def _():
        kpad_ref[0:blk, :] = kpad_ref[tm:tm + blk, :]
        vpad_ref[0:blk, :] = vpad_ref[tm:tm + blk, :]

    _store_padded_kv(kv, kpad_ref, vpad_ref, blk, n_kv)

    @pl.when(i == n_i - 1)
    def _():
        ck_ref[...] = kv[tm - blk:, :kvw]
        cv_ref[...] = kv[tm - blk:, kvw:]

    block = functools.partial(_attn_block, sink_ref, q_ref, sg_ref, kpad_ref, vpad_ref, act_ref, n_kv=n_kv)

    @pl.when(i == 0)
    def _():
        block(0, first=True)

    @pl.when(i > 0)
    def _():
        block(0, first=False)

    def blk_body(bi, carry):
        block(pl.multiple_of(bi * blk, blk), first=False)
        return carry
    lax.fori_loop(1, tm // blk, blk_body, 0)

    _attn_output(act_ref, h_ref, gpost_ref, wout_ref, y_ref, tm)


def _attn_layer_prompt(h, sinks, gkv, gpre, gpost, w_kv, w_in, w_out, *, tm):
    b, t, d = h.shape
    kvw = w_kv.shape[1] // 2
    n_kv = kvw // HEAD_DIM
    kern = functools.partial(_attn_prompt_kernel, tm=tm, n_kv=n_kv)
    vec = lambda: _const_spec((1, d), 2)
    return pl.pallas_call(
        kern,
        grid=(b, t // tm),
        in_specs=[
            pl.BlockSpec(memory_space=pltpu.SMEM),
            pl.BlockSpec((None, tm, d), lambda bb, i: (bb, i, 0)),
            vec(), vec(), vec(),
            _const_spec(w_kv.shape, 2),
            _const_spec(w_in.shape, 2),
            _const_spec(w_out.shape, 2),
        ],
        out_specs=[
            pl.BlockSpec((None, tm, d), lambda bb, i: (bb, i, 0)),
            pl.BlockSpec((None, WINDOW, kvw), lambda bb, i: (bb, 0, 0)),
            pl.BlockSpec((None, WINDOW, kvw), lambda bb, i: (bb, 0, 0)),
        ],
        out_shape=[
            jax.ShapeDtypeStruct((b, t, d), F32),
            jax.ShapeDtypeStruct((b, WINDOW, kvw), F32),
            jax.ShapeDtypeStruct((b, WINDOW, kvw), F32),
        ],
        scratch_shapes=[
            pltpu.VMEM((tm, d), BF16),
            pltpu.VMEM((tm, d), BF16),
            pltpu.VMEM((tm, d), BF16),
            pltpu.VMEM((tm, d), BF16),
            pltpu.VMEM((WINDOW + tm, 4 * kvw), BF16),
            pltpu.VMEM((WINDOW + tm, 8 * kvw), BF16),
            pltpu.VMEM((tm, d), BF16),
        ],
        compiler_params=pltpu.CompilerParams(
            dimension_semantics=("arbitrary", "arbitrary"),
            vmem_limit_bytes=VMEM_LIMIT),
        name="attn_layer_prompt",
    )(sinks, h, gkv, gpre, gpost, w_kv, w_in, w_out)


def _attn_sample_proj_kernel(h_ref, gkv_ref, gpre_ref, wkv_ref, win_ref, kv_ref, q_ref, sg_ref,
                             ukv_ref, u_ref, *, tm):
    kv_ref[...] = _attn_projections(h_ref, gkv_ref, gpre_ref, wkv_ref, win_ref, ukv_ref, u_ref, q_ref,
                                    sg_ref, tm, 4)


def _attn_sample_core_kernel(sink_ref, q_ref, sg_ref, kv_ref, ck_in_ref, cv_in_ref,
                             act_ref, ck_ref, cv_ref, *, n_seq, t_new, n_kv, n_par):
    kvw = n_kv * HEAD_DIM
    w_buf = ck_in_ref.shape[1]
    n_keys = w_buf + 2 * t_new
    n_rows = GROUP * n_kv * t_new
    tq = lax.broadcasted_iota(jnp.int32, (n_rows, n_keys), 0) % t_new
    kj = lax.broadcasted_iota(jnp.int32, (n_rows, n_keys), 1)
    mask = ((kj < w_buf) & (kj > tq)) | ((kj >= w_buf) & (kj - w_buf <= tq))
    row_blk = lax.broadcasted_iota(jnp.int32, (n_rows, 1), 0) // t_new
    sinkv = jnp.zeros((n_rows, 1), F32)
    for hh in range(GROUP):
        for g in range(n_kv):
            sinkv = jnp.where(row_blk == hh * n_kv + g, sink_ref[g * GROUP + hh] * LOG2E, sinkv)
    lane_kv = lax.broadcasted_iota(jnp.int32, (t_new, kvw), 1) // HEAD_DIM

    def one_seq(s):
        rows = pl.ds(pl.multiple_of(s * t_new, t_new), t_new)
        kc = ck_in_ref[s]
        vc = cv_in_ref[s]
        kvn = kv_ref[rows, :]
        ck_ref[s, 0:w_buf - t_new, :] = kc[t_new:]
        cv_ref[s, 0:w_buf - t_new, :] = vc[t_new:]
        ck_ref[s, w_buf - t_new:w_buf, :] = kvn[:, :kvw]
        cv_ref[s, w_buf - t_new:w_buf, :] = kvn[:, kvw:]
        pad = jnp.zeros((t_new, kvw), F32)
        k_all = jnp.concatenate([kc, kvn[:, :kvw], pad], axis=0).astype(BF16)
        v_all = jnp.concatenate([vc, kvn[:, kvw:], pad], axis=0).astype(BF16)
        qparts = []
        for hh in range(GROUP):
            slab = q_ref[rows, hh * kvw:(hh + 1) * kvw].astype(F32)
            for g in range(n_kv):
                qparts.append(jnp.where(lane_kv == g, slab, 0.0))
        qbd = jnp.concatenate(qparts, axis=0).astype(BF16)
        sc = jnp.where(mask, _dot_t(qbd, k_all), -jnp.inf)
        m = jnp.maximum(jnp.max(sc, axis=-1, keepdims=True), sinkv)
        p = jnp.exp2(sc - m)
        den = jnp.sum(p, axis=-1, keepdims=True) + jnp.exp2(sinkv - m)
        o = _dot(p.astype(BF16), v_all) * (1.0 / den)
        for hh in range(GROUP):
            slab = None
            for g in range(n_kv - 1, -1, -1):
                r = (hh * n_kv + g) * t_new
                part = o[r:r + t_new]
                slab = part if slab is None else jnp.where(lane_kv == g, part, slab)
            cols = slice(hh * kvw, (hh + 1) * kvw)
            act_ref[rows, cols] = (slab * sg_ref[rows, cols].astype(F32)).astype(BF16)

    def seq_body(sb, carry):
        for q in range(n_par):
            one_seq(sb * n_par + q)
        return carry
    lax.fori_loop(0, n_seq // n_par, seq_body, 0)


def _attn_sample_out_kernel(act_ref, h_ref, gpost_ref, wout_ref, y_ref, *, tm):
    _attn_output(act_ref, h_ref, gpost_ref, wout_ref, y_ref, tm)


def _attn_layer_sample(h, cache_k, cache_v, sinks, gkv, gpre, gpost, w_kv, w_in, w_out, *, tm,
                       n_seq_tile):
    n_seq, t_new, d = h.shape
    kvw = w_kv.shape[1] // 2
    n_kv = kvw // HEAD_DIM
    w_buf = cache_k.shape[1]
    n_tok = n_seq * t_new
    h2 = h.reshape(n_tok, d)
    vec = lambda: _const_spec((1, d), 1)
    params = pltpu.CompilerParams(dimension_semantics=("arbitrary",), vmem_limit_bytes=VMEM_LIMIT)

    kv, q, sg = pl.pallas_call(
        functools.partial(_attn_sample_proj_kernel, tm=tm),
        grid=(n_tok // tm,),
        in_specs=[
            pl.BlockSpec((tm, d), lambda i: (i, 0)),
            vec(), vec(),
            _const_spec(w_kv.shape, 1),
            _const_spec(w_in.shape, 1),
        ],
        out_specs=[
            pl.BlockSpec((tm, 2 * kvw), lambda i: (i, 0)),
            pl.BlockSpec((tm, d), lambda i: (i, 0)),
            pl.BlockSpec((tm, d), lambda i: (i, 0)),
        ],
        out_shape=[
            jax.ShapeDtypeStruct((n_tok, 2 * kvw), F32),
            jax.ShapeDtypeStruct((n_tok, d), BF16),
            jax.ShapeDtypeStruct((n_tok, d), BF16),
        ],
        scratch_shapes=[pltpu.VMEM((tm, d), BF16), pltpu.VMEM((tm, d), BF16)],
        compiler_params=params,
        name="attn_sample_proj",
    )(h2, gkv, gpre, w_kv, w_in)

    def to_slabs(a):
        return a.reshape(n_tok, n_kv, GROUP, HEAD_DIM).transpose(0, 2, 1, 3).reshape(n_tok, d)

    def from_slabs(a):
        return a.reshape(n_tok, GROUP, n_kv, HEAD_DIM).transpose(0, 2, 1, 3).reshape(n_tok, d)

    q, sg = to_slabs(q), to_slabs(sg)
    rows = n_seq_tile * t_new
    act, ck, cv = pl.pallas_call(
        functools.partial(_attn_sample_core_kernel, n_seq=n_seq_tile, t_new=t_new, n_kv=n_kv, n_par=4),
        grid=(n_seq // n_seq_tile,),
        in_specs=[
            pl.BlockSpec(memory_space=pltpu.SMEM),
            pl.BlockSpec((rows, d), lambda i: (i, 0)),
            pl.BlockSpec((rows, d), lambda i: (i, 0)),
            pl.BlockSpec((rows, 2 * kvw), lambda i: (i, 0)),
            pl.BlockSpec((n_seq_tile, w_buf, kvw), lambda i: (i, 0, 0)),
            pl.BlockSpec((n_seq_tile, w_buf, kvw), lambda i: (i, 0, 0)),
        ],
        out_specs=[
            pl.BlockSpec((rows, d), lambda i: (i, 0)),
            pl.BlockSpec((n_seq_tile, w_buf, kvw), lambda i: (i, 0, 0)),
            pl.BlockSpec((n_seq_tile, w_buf, kvw), lambda i: (i, 0, 0)),
        ],
        out_shape=[
            jax.ShapeDtypeStruct((n_tok, d), BF16),
            jax.ShapeDtypeStruct((n_seq, w_buf, kvw), F32),
            jax.ShapeDtypeStruct((n_seq, w_buf, kvw), F32),
        ],
        compiler_params=params,
        name="attn_sample_core",
    )(sinks, q, sg, kv, cache_k, cache_v)

    y = pl.pallas_call(
        functools.partial(_attn_sample_out_kernel, tm=tm),
        grid=(n_tok // tm,),
        in_specs=[
            pl.BlockSpec((tm, d), lambda i: (i, 0)),
            pl.BlockSpec((tm, d), lambda i: (i, 0)),
            vec(),
            _const_spec(w_out.shape, 1),
        ],
        out_specs=pl.BlockSpec((tm, d), lambda i: (i, 0)),
        out_shape=jax.ShapeDtypeStruct((n_tok, d), F32),
        compiler_params=params,
        name="attn_sample_out",
    )(from_slabs(act), h2, gpost, w_out)
    return y.reshape(n_seq, t_new, d), ck, cv


def kernel(x_prompt, x_sample, state_conv, cache_k, cache_v, norm_pre, norm_post, w_in_a, conv_w, conv_b, ln_g, ln_b, w_out_a, kv_norm, w_kv, w_in_b, sinks, w_out_b):
    n_a = w_in_a.shape[0]
    assert n_a == 1 and w_in_b.shape[0] == 1 and norm_pre.shape[0] == 2
    d = x_prompt.shape[-1]
    n_seq, w_buf, n_kv, hd = cache_k.shape
    assert hd == HEAD_DIM and w_buf == WINDOW

    row = lambda v: v.reshape(1, -1)
    w_in_a_bf = w_in_a[0].astype(BF16)
    w_out_a_bf = w_out_a[0].astype(BF16)
    w_kv_bf = w_kv.astype(BF16)
    w_in_b_bf = w_in_b[0].astype(BF16)
    w_out_b_bf = w_out_b[0].astype(BF16)
    conv_args = (w_in_a_bf, conv_w[0], row(conv_b[0]), row(ln_g[0]), row(ln_b[0]), row(norm_pre[0]),
                 row(norm_post[0]), w_out_a_bf)
    attn_args = (sinks[0], row(kv_norm), row(norm_pre[1]), row(norm_post[1]), w_kv_bf, w_in_b_bf,
                 w_out_b_bf)

    h_p, st_p = _conv_layer_prompt(x_prompt, *conv_args, tm=512, cn=256)
    y_p, ck_p, cv_p = _attn_layer_prompt(h_p, *attn_args, tm=256)

    h_s, st_s = _conv_layer_sample(x_sample, state_conv[0], *conv_args, n_seq_tile=64, cn=256)
    y_s, ck_s, cv_s = _attn_layer_sample(
        h_s, cache_k.reshape(n_seq, w_buf, n_kv * hd), cache_v.reshape(n_seq, w_buf, n_kv * hd),
        *attn_args, tm=512, n_seq_tile=16)

    b = x_prompt.shape[0]
    return (y_p, y_s, st_p[None], ck_p.reshape(b, w_buf, n_kv, hd), cv_p.reshape(b, w_buf, n_kv, hd),
            st_s[None], ck_s.reshape(n_seq, w_buf, n_kv, hd), cv_s.reshape(n_seq, w_buf, n_kv, hd))
```

```python
import functools

import jax
import jax.numpy as jnp
from jax import lax
from jax.experimental import pallas as pl
from jax.experimental.pallas import tpu as pltpu

RMS_EPS = 1e-6
LN_EPS = 1e-5
HEAD_DIM = 64
GROUP = 8
WINDOW = 128
SUBLANES = 8
LANES = 128
LOG2E = 1.4426950408889634
HALO = 32
VMEM_LIMIT = 56 * 1024 * 1024
ROW_BLK = 64

BF16 = jnp.bfloat16
F32 = jnp.float32


def _rows(i, n):
    return pl.ds(pl.multiple_of(i * n, n), n)


def _rms_scale(x):
    return lax.rsqrt(jnp.mean(x * x, axis=-1, keepdims=True) + RMS_EPS)


def _dot(a, b):
    return jnp.dot(a, b, preferred_element_type=F32)


def _dot_t(a, b):
    return lax.dot_general(a, b, (((1,), (1,)), ((), ())), preferred_element_type=F32)


def _pre_norm_to_bf16(x_ref, g_ref, u_ref, tm):
    def body(r, c):
        rows = _rows(r, ROW_BLK)
        x = x_ref[rows, :]
        u_ref[rows, :] = (x * _rms_scale(x) * g_ref[...]).astype(BF16)
        return c
    lax.fori_loop(0, tm // ROW_BLK, body, 0, unroll=2)


def _glu_chunk(u_ref, wa_ref, wb_ref, wg_ref):
    u = u_ref[...]
    c = _dot(u, wa_ref[...]) * jax.nn.sigmoid(_dot(u, wb_ref[...]))
    sg = jax.nn.silu(_dot(u, wg_ref[...])).astype(BF16)
    return c, sg


def _broadcast_taps(cw_ref, wbc_ref, taps):
    for k in range(taps):
        wbc_ref[k] = jnp.broadcast_to(cw_ref[k:k + 1, :], wbc_ref.shape[1:])


def _conv_finalize(j_chunks, tm, cn, y_ref, sg_ref, act_ref, x_ref, h_ref, cb_ref, lng_ref, lnb_ref,
                   gpost_ref, wout_ref):
    d = j_chunks * cn

    ln_rows = ROW_BLK

    def ln_body(r, c):
        rows = _rows(r, ln_rows)
        ys = [y_ref[jj, rows, :] + cb_ref[:, jj * cn:(jj + 1) * cn] for jj in range(j_chunks)]
        mu = jnp.sum(sum(ys), axis=-1, keepdims=True) * (1.0 / d)
        yc = [y - mu for y in ys]
        var = jnp.sum(sum(y * y for y in yc), axis=-1, keepdims=True) * (1.0 / d)
        rstd = lax.rsqrt(var + LN_EPS)
        for jj in range(j_chunks):
            cols = slice(jj * cn, (jj + 1) * cn)
            t = jax.nn.silu(yc[jj] * rstd * lng_ref[:, cols] + lnb_ref[:, cols])
            act_ref[rows, cols] = (t * sg_ref[jj, rows, :].astype(F32)).astype(BF16)
        return c
    lax.fori_loop(0, tm // ln_rows, ln_body, 0, unroll=2)

    h_ref[...] = _dot(act_ref[...], wout_ref[...])

    def post_body(r, c):
        rows = [_rows(2 * r + e, ROW_BLK) for e in range(2)]
        os = [h_ref[rw, :] for rw in rows]
        res = [x_ref[rw, :] + o * _rms_scale(o) * gpost_ref[...] for rw, o in zip(rows, os)]
        for rw, v in zip(rows, res):
            h_ref[rw, :] = v
        return c
    lax.fori_loop(0, tm // (2 * ROW_BLK), post_body, 0)


def _conv_chunk(cext_ref, wbc_ref, y_ref, jc, *, tm, cn, taps, grp):
    lbs = cn // LANES
    assert lbs >= 2, "the row-interleaved layout needs at least two lane blocks per chunk"
    lead = HALO - (taps - 1)
    for g in range(tm // (grp * SUBLANES)):
        base = g * grp * SUBLANES
        for lb in range(lbs):
            ls = slice(lb * LANES, (lb + 1) * LANES)
            acc = [None] * grp
            for k in range(taps):
                wv = wbc_ref[jc, k, :, ls]
                for gi in range(grp):
                    r0 = base + SUBLANES * gi + k + lead
                    t = wv * cext_ref[jc, pl.ds(lbs * r0 + lb, SUBLANES, stride=lbs), :]
                    acc[gi] = t if acc[gi] is None else acc[gi] + t
            for gi in range(grp):
                y_ref[jc, base + SUBLANES * gi:base + SUBLANES * (gi + 1), ls] = acc[gi]


def _conv_prompt_kernel(x_ref, w_hbm, cw_ref, cb_ref, lng_ref, lnb_ref, gpre_ref, gpost_ref, wout_ref,
                        h_ref, st_ref,
                        u_ref, wbuf_ref, wsem, cext_ref, wbc_ref, y_ref, sg_ref, act_ref,
                        *, tm, cn, taps, grp):
    bb = pl.program_id(0)
    i = pl.program_id(1)
    n_chunks = cext_ref.shape[0]
    lbs = cn // LANES
    first_step = (bb == 0) & (i == 0)
    last_step = (bb == pl.num_programs(0) - 1) & (i == pl.num_programs(1) - 1)

    def weight_copy(jc, slot, part):
        col0 = pl.multiple_of((part * n_chunks + jc) * cn, cn)
        return pltpu.make_async_copy(w_hbm.at[:, pl.ds(col0, cn)], wbuf_ref.at[slot, part],
                                     wsem.at[slot, part])

    @pl.when(first_step)
    def _():
        for part in range(3):
            weight_copy(0, 0, part).start()
        for jj in range(n_chunks):
            for k in range(taps):
                wbc_ref[jj, k] = jnp.broadcast_to(cw_ref[k:k + 1, jj * cn:(jj + 1) * cn], (SUBLANES, cn))

    _pre_norm_to_bf16(x_ref, gpre_ref, u_ref, tm)

    def chunk_body(j, carry):
        slot = j % 2
        for part in range(3):
            weight_copy(j, slot, part).wait()

        @pl.when(jnp.logical_not(last_step & (j == n_chunks - 1)))
        def _():
            nxt = jnp.where(j == n_chunks - 1, 0, j + 1)
            for part in range(3):
                weight_copy(nxt, 1 - slot, part).start()

        @pl.when(i == 0)
        def _():
            cext_ref[j, 0:lbs * HALO, :] = jnp.zeros((lbs * HALO, LANES), F32)

        @pl.when(i > 0)
        def _():
            cext_ref[j, 0:lbs * HALO, :] = cext_ref[j, lbs * tm:lbs * (tm + HALO), :]

        u = u_ref[...]
        c = _dot(u, wbuf_ref[slot, 0]) * jax.nn.sigmoid(_dot(u, wbuf_ref[slot, 1]))
        sg_ref[j] = jax.nn.silu(_dot(u, wbuf_ref[slot, 2])).astype(BF16)
        for lb in range(lbs):
            cext_ref[j, pl.ds(lbs * HALO + lb, tm, stride=lbs), :] = c[:, lb * LANES:(lb + 1) * LANES]
        _conv_chunk(cext_ref, wbc_ref, y_ref, j, tm=tm, cn=cn, taps=taps, grp=grp)
        return carry
    lax.fori_loop(0, n_chunks, chunk_body, 0)

    _conv_finalize(n_chunks, tm, cn, y_ref, sg_ref, act_ref, x_ref, h_ref, cb_ref,
                   lng_ref, lnb_ref, gpost_ref, wout_ref)

    @pl.when(i == pl.num_programs(1) - 1)
    def _():
        tail = lbs * (HALO + tm - (taps - 1))
        for jj in range(n_chunks):
            for lb in range(lbs):
                c0 = jj * cn + lb * LANES
                st_ref[:, c0:c0 + LANES] = cext_ref[jj, pl.ds(tail + lb, taps - 1, stride=lbs), :]


def _conv_sample_kernel(x_ref, st_in_ref, wa_ref, wb_ref, wg_ref, cw_ref, cb_ref, lng_ref, lnb_ref,
                        gpre_ref, gpost_ref, wout_ref, h_ref, st_ref,
                        u_ref, c_ref, full_ref, wbc_ref, y_ref, sg_ref, act_ref, *, tm, cn, taps, t_new):
    j = pl.program_id(1)
    n_j = pl.num_programs(1)
    n_seq = tm // t_new
    hist = taps - 1

    @pl.when(j == 0)
    def _():
        _pre_norm_to_bf16(x_ref, gpre_ref, u_ref, tm)

    c, sg = _glu_chunk(u_ref, wa_ref, wb_ref, wg_ref)
    c_ref[...] = c
    sg_ref[j] = sg
    _broadcast_taps(cw_ref, wbc_ref, taps)

    lbs = cn // LANES
    n_par = full_ref.shape[0]

    def seq_body(sb, carry):
        for q in range(n_par):
            s = sb * n_par + q
            rows = pl.ds(pl.multiple_of(s * t_new, t_new), t_new)
            for lb in range(lbs):
                ls = slice(lb * LANES, (lb + 1) * LANES)
                full_ref[q, pl.ds(lb, hist, stride=lbs), :] = st_in_ref[s, :, ls]
                full_ref[q, pl.ds(lbs * hist + lb, t_new, stride=lbs), :] = c_ref[rows, ls]
        for lb in range(lbs):
            ls = slice(lb * LANES, (lb + 1) * LANES)
            acc = [None] * n_par
            for k in range(taps):
                wv = wbc_ref[k, :, ls]
                for q in range(n_par):
                    t = wv * full_ref[q, pl.ds(lbs * k + lb, t_new, stride=lbs), :]
                    acc[q] = t if acc[q] is None else acc[q] + t
            for q in range(n_par):
                s = sb * n_par + q
                y_ref[j, pl.ds(pl.multiple_of(s * t_new, t_new), t_new), ls] = acc[q]
                st_ref[s, :, ls] = full_ref[q, pl.ds(lbs * t_new + lb, hist, stride=lbs), :]
        return carry
    lax.fori_loop(0, n_seq // n_par, seq_body, 0)

    @pl.when(j == n_j - 1)
    def _():
        _conv_finalize(y_ref.shape[0], tm, cn, y_ref, sg_ref, act_ref, x_ref, h_ref, cb_ref,
                       lng_ref, lnb_ref, gpost_ref, wout_ref)


def _const_spec(shape, n_grid):
    zeros = (0,) * len(shape)
    return pl.BlockSpec(shape, lambda *_: zeros, pipeline_mode=pl.Buffered(1))


def _conv_layer_prompt(x, w_in, cw, cb, lng, lnb, gpre, gpost, w_out, *, tm, cn):
    b, t, d = x.shape
    taps = cw.shape[0]
    n_j = d // cn
    lbs = cn // LANES
    grp = 4
    kern = functools.partial(_conv_prompt_kernel, tm=tm, cn=cn, taps=taps, grp=grp)
    vec = lambda: _const_spec((1, d), 2)
    return pl.pallas_call(
        kern,
        grid=(b, t // tm),
        in_specs=[
            pl.BlockSpec((None, tm, d), lambda bb, i: (bb, i, 0)),
            pl.BlockSpec(memory_space=pl.ANY),
            _const_spec((taps, d), 2),
            vec(), vec(), vec(), vec(), vec(),
            _const_spec((d, d), 2),
        ],
        out_specs=[
            pl.BlockSpec((None, tm, d), lambda bb, i: (bb, i, 0)),
            pl.BlockSpec((None, taps - 1, d), lambda bb, i: (bb, 0, 0)),
        ],
        out_shape=[
            jax.ShapeDtypeStruct((b, t, d), F32),
            jax.ShapeDtypeStruct((b, taps - 1, d), F32),
        ],
        scratch_shapes=[
            pltpu.VMEM((tm, d), BF16),
            pltpu.VMEM((2, 3, d, cn), BF16),
            pltpu.SemaphoreType.DMA((2, 3)),
            pltpu.VMEM((n_j, lbs * (HALO + tm), LANES), F32),
            pltpu.VMEM((n_j, taps, SUBLANES, cn), F32),
            pltpu.VMEM((n_j, tm, cn), F32),
            pltpu.VMEM((n_j, tm, cn), BF16),
            pltpu.VMEM((tm, d), BF16),
        ],
        compiler_params=pltpu.CompilerParams(
            dimension_semantics=("arbitrary", "arbitrary"),
            vmem_limit_bytes=VMEM_LIMIT),
        name="conv_layer_prompt",
    )(x, w_in, cw, cb, lng, lnb, gpre, gpost, w_out)


def _conv_layer_sample(x, state, w_in, cw, cb, lng, lnb, gpre, gpost, w_out, *, n_seq_tile, cn):
    n_seq, t_new, d = x.shape
    taps = cw.shape[0]
    n_j = d // cn
    tm = n_seq_tile * t_new
    x2 = x.reshape(n_seq * t_new, d)
    kern = functools.partial(_conv_sample_kernel, tm=tm, cn=cn, taps=taps, t_new=t_new)
    vec = lambda: _const_spec((1, d), 2)
    h, st = pl.pallas_call(
        kern,
        grid=(n_seq // n_seq_tile, n_j),
        in_specs=[
            pl.BlockSpec((tm, d), lambda i, j: (i, 0)),
            pl.BlockSpec((n_seq_tile, taps - 1, cn), lambda i, j: (i, 0, j)),
            pl.BlockSpec((d, cn), lambda i, j: (0, j)),
            pl.BlockSpec((d, cn), lambda i, j: (0, n_j + j)),
            pl.BlockSpec((d, cn), lambda i, j: (0, 2 * n_j + j)),
            pl.BlockSpec((taps, cn), lambda i, j: (0, j)),
            vec(), vec(), vec(), vec(), vec(),
            _const_spec((d, d), 2),
        ],
        out_specs=[
            pl.BlockSpec((tm, d), lambda i, j: (i, 0)),
            pl.BlockSpec((n_seq_tile, taps - 1, cn), lambda i, j: (i, 0, j)),
        ],
        out_shape=[
            jax.ShapeDtypeStruct((n_seq * t_new, d), F32),
            jax.ShapeDtypeStruct((n_seq, taps - 1, d), F32),
        ],
        scratch_shapes=[
            pltpu.VMEM((tm, d), BF16),
            pltpu.VMEM((tm, cn), F32),
            pltpu.VMEM((4, (cn // LANES) * (HALO + t_new), LANES), F32),
            pltpu.VMEM((taps, SUBLANES, cn), F32),
            pltpu.VMEM((n_j, tm, cn), F32),
            pltpu.VMEM((n_j, tm, cn), BF16),
            pltpu.VMEM((tm, d), BF16),
        ],
        compiler_params=pltpu.CompilerParams(
            dimension_semantics=("arbitrary", "arbitrary"),
            vmem_limit_bytes=VMEM_LIMIT),
        name="conv_layer_sample",
    )(x2, state, w_in, w_in, w_in, cw, cb, lng, lnb, gpre, gpost, w_out)
    return h.reshape(n_seq, t_new, d), st


def _attn_projections(h_ref, gkv_ref, gpre_ref, wkv_ref, win_ref, ukv_ref, u_ref, q_ref, sg_ref, tm,
                      n_chunk):
    d = h_ref.shape[-1]

    def body(r, c):
        rows = _rows(r, ROW_BLK)
        x = h_ref[rows, :]
        xn = x * _rms_scale(x)
        ukv_ref[rows, :] = (xn * gkv_ref[...]).astype(BF16)
        u_ref[rows, :] = (xn * gpre_ref[...]).astype(BF16)
        return c
    lax.fori_loop(0, tm // ROW_BLK, body, 0, unroll=2)

    kv = _dot(ukv_ref[...], wkv_ref[...])
    u = u_ref[...]
    cw = d // n_chunk
    scale = HEAD_DIM ** -0.5 * LOG2E
    for n in range(n_chunk):
        cols = slice(n * cw, (n + 1) * cw)
        q_ref[:, cols] = (_dot(u, win_ref[:, cols]) * scale).astype(BF16)
    for n in range(n_chunk):
        cols = slice(n * cw, (n + 1) * cw)
        sg_ref[:, cols] = jax.nn.silu(_dot(u, win_ref[:, d + n * cw:d + (n + 1) * cw])).astype(BF16)
    return kv


def _attn_output(act_ref, h_ref, gpost_ref, wout_ref, y_ref, tm):
    y_ref[...] = _dot(act_ref[...], wout_ref[...])

    def post_body(r, c):
        rows = [_rows(2 * r + e, ROW_BLK) for e in range(2)]
        os = [y_ref[rw, :] for rw in rows]
        res = [h_ref[rw, :] + o * _rms_scale(o) * gpost_ref[...] for rw, o in zip(rows, os)]
        for rw, v in zip(rows, res):
            y_ref[rw, :] = v
        return c
    lax.fori_loop(0, tm // (2 * ROW_BLK), post_body, 0)


def _attn_block(sink_ref, q_ref, sg_ref, kpad_ref, vpad_ref, act_ref, r0, *, n_kv, first):
    blk = WINDOW
    pair_w = 2 * HEAD_DIM
    n_pair = GROUP // 2
    qi = lax.broadcasted_iota(jnp.int32, (blk, blk), 0)
    kj = lax.broadcasted_iota(jnp.int32, (blk, blk), 1)
    own = kj <= qi
    low = lax.broadcasted_iota(jnp.int32, (blk, pair_w), 1) < HEAD_DIM
    rows = pl.ds(r0, blk)
    krows = pl.ds(r0, 2 * blk)
    for g in range(n_kv):
        qs = jnp.concatenate(
            [q_ref[rows, (g * n_pair + pp) * pair_w:(g * n_pair + pp + 1) * pair_w]
             for pp in range(n_pair)], axis=0)
        probs, stats = [], []
        for e in range(2):
            s = _dot_t(qs, kpad_ref[krows, (2 * g + e) * pair_w:(2 * g + e + 1) * pair_w])
            p_parts, st = [], []
            for pp in range(n_pair):
                sh = s[pp * blk:(pp + 1) * blk]
                sf = jnp.where(own, sh[:, blk:], -jnp.inf if first else sh[:, :blk])
                sink = sink_ref[g * GROUP + 2 * pp + e] * LOG2E
                m = jnp.maximum(jnp.max(sf, axis=-1, keepdims=True), sink)
                p = jnp.exp2(sf - m)
                p_parts.append(jnp.concatenate([jnp.where(own, 0.0, p), jnp.where(own, p, 0.0)],
                                               axis=1).astype(BF16))
                st.append(jnp.exp2(sink - m))
            probs.append(jnp.concatenate(p_parts, axis=0))
            stats.append(st)
        o = (_dot(probs[0], vpad_ref[krows, (2 * g) * 2 * pair_w:(2 * g + 1) * 2 * pair_w])
             + _dot(probs[1], vpad_ref[krows, (2 * g + 1) * 2 * pair_w:(2 * g + 2) * 2 * pair_w]))
        for pp in range(n_pair):
            oh = o[pp * blk:(pp + 1) * blk]
            den = oh[:, pair_w:] + jnp.where(low, stats[0][pp], stats[1][pp])
            c0 = (g * n_pair + pp) * pair_w
            act_ref[rows, c0:c0 + pair_w] = (
                oh[:, :pair_w] / den * sg_ref[rows, c0:c0 + pair_w].astype(F32)).astype(BF16)


def _store_padded_kv(kv, kpad_ref, vpad_ref, row0, n_kv):
    n = kv.shape[0]
    kvw = n_kv * HEAD_DIM
    pair_w = 2 * HEAD_DIM
    low = lax.broadcasted_iota(jnp.int32, (n, pair_w), 1) < HEAD_DIM
    halves = (low, jnp.logical_not(low))
    for gam in range(n_kv // 2):
        kc = kv[:, gam * pair_w:(gam + 1) * pair_w]
        vc = kv[:, kvw + gam * pair_w:kvw + (gam + 1) * pair_w]
        for side in range(2):
            g = 2 * gam + side
            k_here = jnp.where(halves[side], kc, 0.0)
            v_here = jnp.where(halves[side], vc, 0.0)
            k_other = pltpu.roll(k_here, HEAD_DIM, axis=1)
            v_other = pltpu.roll(v_here, HEAD_DIM, axis=1)
            for e in range(2):
                kc0 = (2 * g + e) * pair_w
                vc0 = (2 * g + e) * 2 * pair_w
                kpad_ref[row0:row0 + n, kc0:kc0 + pair_w] = (k_here if e == side else k_other).astype(BF16)
                vpad_ref[row0:row0 + n, vc0:vc0 + pair_w] = (v_here if e == side else v_other).astype(BF16)
                vpad_ref[row0:row0 + n, vc0 + pair_w:vc0 + 2 * pair_w] = (
                    jnp.where(halves[e], 1.0, 0.0).astype(BF16))


def _attn_prompt_kernel(sink_ref, h_ref, gkv_ref, gpre_ref, gpost_ref, wkv_ref, win_ref, wout_ref,
                        y_ref, ck_ref, cv_ref,
                        ukv_ref, u_ref, q_ref, sg_ref, kpad_ref, vpad_ref, act_ref, *, tm, n_kv):
    i = pl.program_id(1)
    n_i = pl.num_programs(1)
    kvw = n_kv * HEAD_DIM
    blk = WINDOW

    kv = _attn_projections(h_ref, gkv_ref, gpre_ref, wkv_ref, win_ref, ukv_ref, u_ref, q_ref, sg_ref,
                           tm, 4)

    @pl.when(i == 0)
    def _():
        kpad_ref[0:blk, :] = jnp.zeros((blk, kpad_ref.shape[1]), BF16)
        vpad_ref[0:blk, :] = jnp.zeros((blk, vpad_ref.shape[1]), BF16)

    @pl.when(i > 0)
    def _():
        kpad_ref[0:blk, :] = kpad_ref[tm:tm + blk, :]
        vpad_ref[0:blk, :] = vpad_ref[tm:tm + blk, :]

    _store_padded_kv(kv, kpad_ref, vpad_ref, blk, n_kv)

    @pl.when(i == n_i - 1)
    def _():
        ck_ref[...] = kv[tm - blk:, :kvw]
        cv_ref[...] = kv[tm - blk:, kvw:]

    block = functools.partial(_attn_block, sink_ref, q_ref, sg_ref, kpad_ref, vpad_ref, act_ref, n_kv=n_kv)

    @pl.when(i == 0)
    def _():
        block(0, first=True)

    @pl.when(i > 0)
    def _():
        block(0, first=False)

    def blk_body(bi, carry):
        block(pl.multiple_of(bi * blk, blk), first=False)
        return carry
    lax.fori_loop(1, tm // blk, blk_body, 0)

    _attn_output(act_ref, h_ref, gpost_ref, wout_ref, y_ref, tm)


def _attn_layer_prompt(h, sinks, gkv, gpre, gpost, w_kv, w_in, w_out, *, tm):
    b, t, d = h.shape
    kvw = w_kv.shape[1] // 2
    n_kv = kvw // HEAD_DIM
    kern = functools.partial(_attn_prompt_kernel, tm=tm, n_kv=n_kv)
    vec = lambda: _const_spec((1, d), 2)
    return pl.pallas_call(
        kern,
        grid=(b, t // tm),
        in_specs=[
            pl.BlockSpec(memory_space=pltpu.SMEM),
            pl.BlockSpec((None, tm, d), lambda bb, i: (bb, i, 0)),
            vec(), vec(), vec(),
            _const_spec(w_kv.shape, 2),
            _const_spec(w_in.shape, 2),
            _const_spec(w_out.shape, 2),
        ],
        out_specs=[
            pl.BlockSpec((None, tm, d), lambda bb, i: (bb, i, 0)),
            pl.BlockSpec((None, WINDOW, kvw), lambda bb, i: (bb, 0, 0)),
            pl.BlockSpec((None, WINDOW, kvw), lambda bb, i: (bb, 0, 0)),
        ],
        out_shape=[
            jax.ShapeDtypeStruct((b, t, d), F32),
            jax.ShapeDtypeStruct((b, WINDOW, kvw), F32),
            jax.ShapeDtypeStruct((b, WINDOW, kvw), F32),
        ],
        scratch_shapes=[
            pltpu.VMEM((tm, d), BF16),
            pltpu.VMEM((tm, d), BF16),
            pltpu.VMEM((tm, d), BF16),
            pltpu.VMEM((tm, d), BF16),
            pltpu.VMEM((WINDOW + tm, 4 * kvw), BF16),
            pltpu.VMEM((WINDOW + tm, 8 * kvw), BF16),
            pltpu.VMEM((tm, d), BF16),
        ],
        compiler_params=pltpu.CompilerParams(
            dimension_semantics=("arbitrary", "arbitrary"),
            vmem_limit_bytes=VMEM_LIMIT),
        name="attn_layer_prompt",
    )(sinks, h, gkv, gpre, gpost, w_kv, w_in, w_out)


def _attn_sample_proj_kernel(h_ref, gkv_ref, gpre_ref, wkv_ref, win_ref, kv_ref, q_ref, sg_ref,
                             ukv_ref, u_ref, *, tm):
    kv_ref[...] = _attn_projections(h_ref, gkv_ref, gpre_ref, wkv_ref, win_ref, ukv_ref, u_ref, q_ref,
                                    sg_ref, tm, 4)


def _attn_sample_core_kernel(sink_ref, q_ref, sg_ref, kv_ref, ck_in_ref, cv_in_ref,
                             act_ref, ck_ref, cv_ref, *, n_seq, t_new, n_kv, n_par):
    kvw = n_kv * HEAD_DIM
    w_buf = ck_in_ref.shape[1]
    n_keys = w_buf + 2 * t_new
    n_rows = GROUP * n_kv * t_new
    tq = lax.broadcasted_iota(jnp.int32, (n_rows, n_keys), 0) % t_new
    kj = lax.broadcasted_iota(jnp.int32, (n_rows, n_keys), 1)
    mask = ((kj < w_buf) & (kj > tq)) | ((kj >= w_buf) & (kj - w_buf <= tq))
    row_head = lax.broadcasted_iota(jnp.int32, (n_rows, 1), 0) // t_new
    sinkv = jnp.zeros((n_rows, 1), F32)
    for h in range(GROUP * n_kv):
        sinkv = jnp.where(row_head == h, sink_ref[h] * LOG2E, sinkv)
    pair_w = 2 * HEAD_DIM
    low = lax.broadcasted_iota(jnp.int32, (t_new, pair_w), 1) < HEAD_DIM
    halves = (low, jnp.logical_not(low))
    zero_blk = jnp.zeros((t_new, pair_w), F32)

    def to_half(block, src_half, dst_half):
        moved = block if src_half == dst_half else pltpu.roll(block, HEAD_DIM, axis=1)
        return jnp.where(halves[dst_half], moved, 0.0)

    def one_seq(s):
        rows = pl.ds(pl.multiple_of(s * t_new, t_new), t_new)
        kc = ck_in_ref[s]
        vc = cv_in_ref[s]
        kvn = kv_ref[rows, :]
        ck_ref[s, 0:w_buf - t_new, :] = kc[t_new:]
        cv_ref[s, 0:w_buf - t_new, :] = vc[t_new:]
        ck_ref[s, w_buf - t_new:w_buf, :] = kvn[:, :kvw]
        cv_ref[s, w_buf - t_new:w_buf, :] = kvn[:, kvw:]
        pad = jnp.zeros((t_new, kvw), F32)
        k_all = jnp.concatenate([kc, kvn[:, :kvw], pad], axis=0).astype(BF16)
        v_all = jnp.concatenate([vc, kvn[:, kvw:], pad], axis=0).astype(BF16)
        qparts = []
        for cb in range(GROUP * n_kv // 2):
            blk = q_ref[rows, cb * pair_w:(cb + 1) * pair_w].astype(F32)
            g = 2 * cb // GROUP
            for e in range(2):
                piece = to_half(blk, e, g % 2)
                qparts.append(jnp.concatenate(
                    [piece if kb == g // 2 else zero_blk for kb in range(kvw // pair_w)], axis=1))
        qbd = jnp.concatenate(qparts, axis=0).astype(BF16)
        sc = jnp.where(mask, _dot_t(qbd, k_all), -jnp.inf)
        m = jnp.maximum(jnp.max(sc, axis=-1, keepdims=True), sinkv)
        p = jnp.exp2(sc - m)
        den = jnp.sum(p, axis=-1, keepdims=True) + jnp.exp2(sinkv - m)
        o = _dot(p.astype(BF16), v_all) * (1.0 / den)
        for cb in range(GROUP * n_kv // 2):
            g = 2 * cb // GROUP
            kcols = slice((g // 2) * pair_w, (g // 2 + 1) * pair_w)
            pieces = [to_half(o[(2 * cb + e) * t_new:(2 * cb + e + 1) * t_new, kcols], g % 2, e)
                      for e in range(2)]
            cols = slice(cb * pair_w, (cb + 1) * pair_w)
            act_ref[rows, cols] = ((pieces[0] + pieces[1]) * sg_ref[rows, cols].astype(F32)).astype(BF16)

    def seq_body(sb, carry):
        for q in range(n_par):
            one_seq(sb * n_par + q)
        return carry
    lax.fori_loop(0, n_seq // n_par, seq_body, 0)


def _attn_sample_out_kernel(act_ref, h_ref, gpost_ref, wout_ref, y_ref, *, tm):
    _attn_output(act_ref, h_ref, gpost_ref, wout_ref, y_ref, tm)


def _attn_layer_sample(h, cache_k, cache_v, sinks, gkv, gpre, gpost, w_kv, w_in, w_out, *, tm,
                       n_seq_tile):
    n_seq, t_new, d = h.shape
    kvw = w_kv.shape[1] // 2
    n_kv = kvw // HEAD_DIM
    w_buf = cache_k.shape[1]
    n_tok = n_seq * t_new
    h2 = h.reshape(n_tok, d)
    vec = lambda: _const_spec((1, d), 1)
    params = pltpu.CompilerParams(dimension_semantics=("arbitrary",), vmem_limit_bytes=VMEM_LIMIT)

    kv, q, sg = pl.pallas_call(
        functools.partial(_attn_sample_proj_kernel, tm=tm),
        grid=(n_tok // tm,),
        in_specs=[
            pl.BlockSpec((tm, d), lambda i: (i, 0)),
            vec(), vec(),
            _const_spec(w_kv.shape, 1),
            _const_spec(w_in.shape, 1),
        ],
        out_specs=[
            pl.BlockSpec((tm, 2 * kvw), lambda i: (i, 0)),
            pl.BlockSpec((tm, d), lambda i: (i, 0)),
            pl.BlockSpec((tm, d), lambda i: (i, 0)),
        ],
        out_shape=[
            jax.ShapeDtypeStruct((n_tok, 2 * kvw), F32),
            jax.ShapeDtypeStruct((n_tok, d), BF16),
            jax.ShapeDtypeStruct((n_tok, d), BF16),
        ],
        scratch_shapes=[pltpu.VMEM((tm, d), BF16), pltpu.VMEM((tm, d), BF16)],
        compiler_params=params,
        name="attn_sample_proj",
    )(h2, gkv, gpre, w_kv, w_in)

    rows = n_seq_tile * t_new
    act, ck, cv = pl.pallas_call(
        functools.partial(_attn_sample_core_kernel, n_seq=n_seq_tile, t_new=t_new, n_kv=n_kv, n_par=4),
        grid=(n_seq // n_seq_tile,),
        in_specs=[
            pl.BlockSpec(memory_space=pltpu.SMEM),
            pl.BlockSpec((rows, d), lambda i: (i, 0)),
            pl.BlockSpec((rows, d), lambda i: (i, 0)),
            pl.BlockSpec((rows, 2 * kvw), lambda i: (i, 0)),
            pl.BlockSpec((n_seq_tile, w_buf, kvw), lambda i: (i, 0, 0)),
            pl.BlockSpec((n_seq_tile, w_buf, kvw), lambda i: (i, 0, 0)),
        ],
        out_specs=[
            pl.BlockSpec((rows, d), lambda i: (i, 0)),
            pl.BlockSpec((n_seq_tile, w_buf, kvw), lambda i: (i, 0, 0)),
            pl.BlockSpec((n_seq_tile, w_buf, kvw), lambda i: (i, 0, 0)),
        ],
        out_shape=[
            jax.ShapeDtypeStruct((n_tok, d), BF16),
            jax.ShapeDtypeStruct((n_seq, w_buf, kvw), F32),
            jax.ShapeDtypeStruct((n_seq, w_buf, kvw), F32),
        ],
        compiler_params=params,
        name="attn_sample_core",
    )(sinks, q, sg, kv, cache_k, cache_v)

    y = pl.pallas_call(
        functools.partial(_attn_sample_out_kernel, tm=tm),
        grid=(n_tok // tm,),
        in_specs=[
            pl.BlockSpec((tm, d), lambda i: (i, 0)),
            pl.BlockSpec((tm, d), lambda i: (i, 0)),
            vec(),
            _const_spec(w_out.shape, 1),
        ],
        out_specs=pl.BlockSpec((tm, d), lambda i: (i, 0)),
        out_shape=jax.ShapeDtypeStruct((n_tok, d), F32),
        compiler_params=params,
        name="attn_sample_out",
    )(act, h2, gpost, w_out)
    return y.reshape(n_seq, t_new, d), ck, cv


def kernel(x_prompt, x_sample, state_conv, cache_k, cache_v, norm_pre, norm_post, w_in_a, conv_w, conv_b, ln_g, ln_b, w_out_a, kv_norm, w_kv, w_in_b, sinks, w_out_b):
    n_a = w_in_a.shape[0]
    assert n_a == 1 and w_in_b.shape[0] == 1 and norm_pre.shape[0] == 2
    d = x_prompt.shape[-1]
    n_seq, w_buf, n_kv, hd = cache_k.shape
    assert hd == HEAD_DIM and w_buf == WINDOW

    row = lambda v: v.reshape(1, -1)
    w_in_a_bf = w_in_a[0].astype(BF16)
    w_out_a_bf = w_out_a[0].astype(BF16)
    w_kv_bf = w_kv.astype(BF16)
    w_in_b_bf = w_in_b[0].astype(BF16)
    w_out_b_bf = w_out_b[0].astype(BF16)
    conv_args = (w_in_a_bf, conv_w[0], row(conv_b[0]), row(ln_g[0]), row(ln_b[0]), row(norm_pre[0]),
                 row(norm_post[0]), w_out_a_bf)
    attn_args = (sinks[0], row(kv_norm), row(norm_pre[1]), row(norm_post[1]), w_kv_bf, w_in_b_bf,
                 w_out_b_bf)

    h_p, st_p = _conv_layer_prompt(x_prompt, *conv_args, tm=512, cn=256)
    y_p, ck_p, cv_p = _attn_layer_prompt(h_p, *attn_args, tm=256)

    h_s, st_s = _conv_layer_sample(x_sample, state_conv[0], *conv_args, n_seq_tile=64, cn=256)
    y_s, ck_s, cv_s = _attn_layer_sample(
        h_s, cache_k.reshape(n_seq, w_buf, n_kv * hd), cache_v.reshape(n_seq, w_buf, n_kv * hd),
        *attn_args, tm=512, n_seq_tile=16)

    b = x_prompt.shape[0]
    return (y_p, y_s, st_p[None], ck_p.reshape(b, w_buf, n_kv, hd), cv_p.reshape(b, w_buf, n_kv, hd),
            st_s[None], ck_s.reshape(n_seq, w_buf, n_kv, hd), cv_s.reshape(n_seq, w_buf, n_kv, hd))
```

```python
import functools

import jax
import jax.numpy as jnp
from jax import lax
from jax.experimental import pallas as pl
from jax.experimental.pallas import tpu as pltpu

RMS_EPS = 1e-6
LN_EPS = 1e-5
HEAD_DIM = 64
GROUP = 8
WINDOW = 128
SUBLANES = 8
LANES = 128
LOG2E = 1.4426950408889634
HALO = 32
VMEM_LIMIT = 56 * 1024 * 1024
ROW_BLK = 64

BF16 = jnp.bfloat16
F32 = jnp.float32


def _rows(i, n):
    return pl.ds(pl.multiple_of(i * n, n), n)


def _rms_scale(x):
    return lax.rsqrt(jnp.mean(x * x, axis=-1, keepdims=True) + RMS_EPS)


def _dot(a, b):
    return jnp.dot(a, b, preferred_element_type=F32)


def _dot_t(a, b):
    return lax.dot_general(a, b, (((1,), (1,)), ((), ())), preferred_element_type=F32)


def _pre_norm_to_bf16(x_ref, g_ref, u_ref, tm):
    def body(r, c):
        rows = _rows(r, ROW_BLK)
        x = x_ref[rows, :]
        u_ref[rows, :] = (x * _rms_scale(x) * g_ref[...]).astype(BF16)
        return c
    lax.fori_loop(0, tm // ROW_BLK, body, 0, unroll=2)


def _glu_chunk(u_ref, wa_ref, wb_ref, wg_ref):
    u = u_ref[...]
    c = _dot(u, wa_ref[...]) * jax.nn.sigmoid(_dot(u, wb_ref[...]))
    sg = jax.nn.silu(_dot(u, wg_ref[...])).astype(BF16)
    return c, sg


def _broadcast_taps(cw_ref, wbc_ref, taps):
    for k in range(taps):
        wbc_ref[k] = jnp.broadcast_to(cw_ref[k:k + 1, :], wbc_ref.shape[1:])


def _conv_finalize(j_chunks, tm, cn, y_ref, sg_ref, act_ref, x_ref, h_ref, cb_ref, lng_ref, lnb_ref,
                   gpost_ref, wout_ref):
    d = j_chunks * cn

    ln_rows = ROW_BLK

    def ln_body(r, c):
        rows = _rows(r, ln_rows)
        ys = [y_ref[jj, rows, :] + cb_ref[:, jj * cn:(jj + 1) * cn] for jj in range(j_chunks)]
        mu = jnp.sum(sum(ys), axis=-1, keepdims=True) * (1.0 / d)
        yc = [y - mu for y in ys]
        var = jnp.sum(sum(y * y for y in yc), axis=-1, keepdims=True) * (1.0 / d)
        rstd = lax.rsqrt(var + LN_EPS)
        for jj in range(j_chunks):
            cols = slice(jj * cn, (jj + 1) * cn)
            t = jax.nn.silu(yc[jj] * rstd * lng_ref[:, cols] + lnb_ref[:, cols])
            act_ref[rows, cols] = (t * sg_ref[jj, rows, :].astype(F32)).astype(BF16)
        return c
    lax.fori_loop(0, tm // ln_rows, ln_body, 0, unroll=2)

    h_ref[...] = _dot(act_ref[...], wout_ref[...])

    def post_body(r, c):
        rows = [_rows(2 * r + e, ROW_BLK) for e in range(2)]
        os = [h_ref[rw, :] for rw in rows]
        res = [x_ref[rw, :] + o * _rms_scale(o) * gpost_ref[...] for rw, o in zip(rows, os)]
        for rw, v in zip(rows, res):
            h_ref[rw, :] = v
        return c
    lax.fori_loop(0, tm // (2 * ROW_BLK), post_body, 0)


def _conv_chunk(cext_ref, wbc_ref, y_ref, jc, *, tm, cn, taps, grp):
    lbs = cn // LANES
    assert lbs >= 2, "the row-interleaved layout needs at least two lane blocks per chunk"
    lead = HALO - (taps - 1)
    for g in range(tm // (grp * SUBLANES)):
        base = g * grp * SUBLANES
        for lb in range(lbs):
            ls = slice(lb * LANES, (lb + 1) * LANES)
            acc = [None] * grp
            for k in range(taps):
                wv = wbc_ref[jc, k, :, ls]
                for gi in range(grp):
                    r0 = base + SUBLANES * gi + k + lead
                    t = wv * cext_ref[jc, pl.ds(lbs * r0 + lb, SUBLANES, stride=lbs), :]
                    acc[gi] = t if acc[gi] is None else acc[gi] + t
            for gi in range(grp):
                y_ref[jc, base + SUBLANES * gi:base + SUBLANES * (gi + 1), ls] = acc[gi]


def _conv_prompt_kernel(x_ref, w_hbm, cw_ref, cb_ref, lng_ref, lnb_ref, gpre_ref, gpost_ref, wout_ref,
                        h_ref, st_ref,
                        u_ref, wbuf_ref, wsem, cext_ref, wbc_ref, y_ref, sg_ref, act_ref,
                        *, tm, cn, taps, grp):
    bb = pl.program_id(0)
    i = pl.program_id(1)
    n_chunks = cext_ref.shape[0]
    lbs = cn // LANES
    first_step = (bb == 0) & (i == 0)
    last_step = (bb == pl.num_programs(0) - 1) & (i == pl.num_programs(1) - 1)

    def weight_copy(jc, slot, part):
        col0 = pl.multiple_of((part * n_chunks + jc) * cn, cn)
        return pltpu.make_async_copy(w_hbm.at[:, pl.ds(col0, cn)], wbuf_ref.at[slot, part],
                                     wsem.at[slot, part])

    @pl.when(first_step)
    def _():
        for part in range(3):
            weight_copy(0, 0, part).start()
        for jj in range(n_chunks):
            for k in range(taps):
                wbc_ref[jj, k] = jnp.broadcast_to(cw_ref[k:k + 1, jj * cn:(jj + 1) * cn], (SUBLANES, cn))

    _pre_norm_to_bf16(x_ref, gpre_ref, u_ref, tm)

    def chunk_body(j, carry):
        slot = j % 2
        for part in range(3):
            weight_copy(j, slot, part).wait()

        @pl.when(jnp.logical_not(last_step & (j == n_chunks - 1)))
        def _():
            nxt = jnp.where(j == n_chunks - 1, 0, j + 1)
            for part in range(3):
                weight_copy(nxt, 1 - slot, part).start()

        @pl.when(i == 0)
        def _():
            cext_ref[j, 0:lbs * HALO, :] = jnp.zeros((lbs * HALO, LANES), F32)

        @pl.when(i > 0)
        def _():
            cext_ref[j, 0:lbs * HALO, :] = cext_ref[j, lbs * tm:lbs * (tm + HALO), :]

        u = u_ref[...]
        c = _dot(u, wbuf_ref[slot, 0]) * jax.nn.sigmoid(_dot(u, wbuf_ref[slot, 1]))
        sg_ref[j] = jax.nn.silu(_dot(u, wbuf_ref[slot, 2])).astype(BF16)
        for lb in range(lbs):
            cext_ref[j, pl.ds(lbs * HALO + lb, tm, stride=lbs), :] = c[:, lb * LANES:(lb + 1) * LANES]
        _conv_chunk(cext_ref, wbc_ref, y_ref, j, tm=tm, cn=cn, taps=taps, grp=grp)
        return carry
    lax.fori_loop(0, n_chunks, chunk_body, 0)

    _conv_finalize(n_chunks, tm, cn, y_ref, sg_ref, act_ref, x_ref, h_ref, cb_ref,
                   lng_ref, lnb_ref, gpost_ref, wout_ref)

    @pl.when(i == pl.num_programs(1) - 1)
    def _():
        tail = lbs * (HALO + tm - (taps - 1))
        for jj in range(n_chunks):
            for lb in range(lbs):
                c0 = jj * cn + lb * LANES
                st_ref[:, c0:c0 + LANES] = cext_ref[jj, pl.ds(tail + lb, taps - 1, stride=lbs), :]


def _conv_sample_kernel(x_ref, st_in_ref, wa_ref, wb_ref, wg_ref, cw_ref, cb_ref, lng_ref, lnb_ref,
                        gpre_ref, gpost_ref, wout_ref, h_ref, st_ref,
                        u_ref, c_ref, full_ref, wbc_ref, y_ref, sg_ref, act_ref, *, tm, cn, taps, t_new):
    j = pl.program_id(1)
    n_j = pl.num_programs(1)
    n_seq = tm // t_new
    hist = taps - 1

    @pl.when(j == 0)
    def _():
        _pre_norm_to_bf16(x_ref, gpre_ref, u_ref, tm)

    c, sg = _glu_chunk(u_ref, wa_ref, wb_ref, wg_ref)
    c_ref[...] = c
    sg_ref[j] = sg
    _broadcast_taps(cw_ref, wbc_ref, taps)

    lbs = cn // LANES
    n_par = full_ref.shape[0]

    def seq_body(sb, carry):
        for q in range(n_par):
            s = sb * n_par + q
            rows = pl.ds(pl.multiple_of(s * t_new, t_new), t_new)
            for lb in range(lbs):
                ls = slice(lb * LANES, (lb + 1) * LANES)
                full_ref[q, pl.ds(lb, hist, stride=lbs), :] = st_in_ref[s, :, ls]
                full_ref[q, pl.ds(lbs * hist + lb, t_new, stride=lbs), :] = c_ref[rows, ls]
        for lb in range(lbs):
            ls = slice(lb * LANES, (lb + 1) * LANES)
            acc = [None] * n_par
            for k in range(taps):
                wv = wbc_ref[k, :, ls]
                for q in range(n_par):
                    t = wv * full_ref[q, pl.ds(lbs * k + lb, t_new, stride=lbs), :]
                    acc[q] = t if acc[q] is None else acc[q] + t
            for q in range(n_par):
                s = sb * n_par + q
                y_ref[j, pl.ds(pl.multiple_of(s * t_new, t_new), t_new), ls] = acc[q]
                st_ref[s, :, ls] = full_ref[q, pl.ds(lbs * t_new + lb, hist, stride=lbs), :]
        return carry
    lax.fori_loop(0, n_seq // n_par, seq_body, 0)

    @pl.when(j == n_j - 1)
    def _():
        _conv_finalize(y_ref.shape[0], tm, cn, y_ref, sg_ref, act_ref, x_ref, h_ref, cb_ref,
                       lng_ref, lnb_ref, gpost_ref, wout_ref)


def _const_spec(shape, n_grid):
    zeros = (0,) * len(shape)
    return pl.BlockSpec(shape, lambda *_: zeros, pipeline_mode=pl.Buffered(1))


def _conv_layer_prompt(x, w_in, cw, cb, lng, lnb, gpre, gpost, w_out, *, tm, cn):
    b, t, d = x.shape
    taps = cw.shape[0]
    n_j = d // cn
    lbs = cn // LANES
    grp = 4
    kern = functools.partial(_conv_prompt_kernel, tm=tm, cn=cn, taps=taps, grp=grp)
    vec = lambda: _const_spec((1, d), 2)
    return pl.pallas_call(
        kern,
        grid=(b, t // tm),
        in_specs=[
            pl.BlockSpec((None, tm, d), lambda bb, i: (bb, i, 0)),
            pl.BlockSpec(memory_space=pl.ANY),
            _const_spec((taps, d), 2),
            vec(), vec(), vec(), vec(), vec(),
            _const_spec((d, d), 2),
        ],
        out_specs=[
            pl.BlockSpec((None, tm, d), lambda bb, i: (bb, i, 0)),
            pl.BlockSpec((None, taps - 1, d), lambda bb, i: (bb, 0, 0)),
        ],
        out_shape=[
            jax.ShapeDtypeStruct((b, t, d), F32),
            jax.ShapeDtypeStruct((b, taps - 1, d), F32),
        ],
        scratch_shapes=[
            pltpu.VMEM((tm, d), BF16),
            pltpu.VMEM((2, 3, d, cn), BF16),
            pltpu.SemaphoreType.DMA((2, 3)),
            pltpu.VMEM((n_j, lbs * (HALO + tm), LANES), F32),
            pltpu.VMEM((n_j, taps, SUBLANES, cn), F32),
            pltpu.VMEM((n_j, tm, cn), F32),
            pltpu.VMEM((n_j, tm, cn), BF16),
            pltpu.VMEM((tm, d), BF16),
        ],
        compiler_params=pltpu.CompilerParams(
            dimension_semantics=("arbitrary", "arbitrary"),
            vmem_limit_bytes=VMEM_LIMIT),
        name="conv_layer_prompt",
    )(x, w_in, cw, cb, lng, lnb, gpre, gpost, w_out)


def _conv_layer_sample(x, state, w_in, cw, cb, lng, lnb, gpre, gpost, w_out, *, n_seq_tile, cn):
    n_seq, t_new, d = x.shape
    taps = cw.shape[0]
    n_j = d // cn
    tm = n_seq_tile * t_new
    x2 = x.reshape(n_seq * t_new, d)
    kern = functools.partial(_conv_sample_kernel, tm=tm, cn=cn, taps=taps, t_new=t_new)
    vec = lambda: _const_spec((1, d), 2)
    h, st = pl.pallas_call(
        kern,
        grid=(n_seq // n_seq_tile, n_j),
        in_specs=[
            pl.BlockSpec((tm, d), lambda i, j: (i, 0)),
            pl.BlockSpec((n_seq_tile, taps - 1, cn), lambda i, j: (i, 0, j)),
            pl.BlockSpec((d, cn), lambda i, j: (0, j)),
            pl.BlockSpec((d, cn), lambda i, j: (0, n_j + j)),
            pl.BlockSpec((d, cn), lambda i, j: (0, 2 * n_j + j)),
            pl.BlockSpec((taps, cn), lambda i, j: (0, j)),
            vec(), vec(), vec(), vec(), vec(),
            _const_spec((d, d), 2),
        ],
        out_specs=[
            pl.BlockSpec((tm, d), lambda i, j: (i, 0)),
            pl.BlockSpec((n_seq_tile, taps - 1, cn), lambda i, j: (i, 0, j)),
        ],
        out_shape=[
            jax.ShapeDtypeStruct((n_seq * t_new, d), F32),
            jax.ShapeDtypeStruct((n_seq, taps - 1, d), F32),
        ],
        scratch_shapes=[
            pltpu.VMEM((tm, d), BF16),
            pltpu.VMEM((tm, cn), F32),
            pltpu.VMEM((4, (cn // LANES) * (HALO + t_new), LANES), F32),
            pltpu.VMEM((taps, SUBLANES, cn), F32),
            pltpu.VMEM((n_j, tm, cn), F32),
            pltpu.VMEM((n_j, tm, cn), BF16),
            pltpu.VMEM((tm, d), BF16),
        ],
        compiler_params=pltpu.CompilerParams(
            dimension_semantics=("arbitrary", "arbitrary"),
            vmem_limit_bytes=VMEM_LIMIT),
        name="conv_layer_sample",
    )(x2, state, w_in, w_in, w_in, cw, cb, lng, lnb, gpre, gpost, w_out)
    return h.reshape(n_seq, t_new, d), st


def _attn_projections(h_ref, gkv_ref, gpre_ref, wkv_ref, win_ref, ukv_ref, u_ref, q_ref, sg_ref, tm,
                      n_chunk):
    d = h_ref.shape[-1]

    def body(r, c):
        rows = _rows(r, ROW_BLK)
        x = h_ref[rows, :]
        xn = x * _rms_scale(x)
        ukv_ref[rows, :] = (xn * gkv_ref[...]).astype(BF16)
        u_ref[rows, :] = (xn * gpre_ref[...]).astype(BF16)
        return c
    lax.fori_loop(0, tm // ROW_BLK, body, 0, unroll=2)

    kv = _dot(ukv_ref[...], wkv_ref[...])
    u = u_ref[...]
    cw = d // n_chunk
    scale = HEAD_DIM ** -0.5 * LOG2E
    for n in range(n_chunk):
        cols = slice(n * cw, (n + 1) * cw)
        q_ref[:, cols] = (_dot(u, win_ref[:, cols]) * scale).astype(BF16)
    for n in range(n_chunk):
        cols = slice(n * cw, (n + 1) * cw)
        sg_ref[:, cols] = jax.nn.silu(_dot(u, win_ref[:, d + n * cw:d + (n + 1) * cw])).astype(BF16)
    return kv


def _attn_output(act_ref, h_ref, gpost_ref, wout_ref, y_ref, tm):
    y_ref[...] = _dot(act_ref[...], wout_ref[...])

    def post_body(r, c):
        rows = [_rows(2 * r + e, ROW_BLK) for e in range(2)]
        os = [y_ref[rw, :] for rw in rows]
        res = [h_ref[rw, :] + o * _rms_scale(o) * gpost_ref[...] for rw, o in zip(rows, os)]
        for rw, v in zip(rows, res):
            y_ref[rw, :] = v
        return c
    lax.fori_loop(0, tm // (2 * ROW_BLK), post_body, 0)


def _attn_block(sink_ref, q_ref, sg_ref, kpad_ref, vpad_ref, act_ref, r0, *, n_kv, first):
    blk = WINDOW
    pair_w = 2 * HEAD_DIM
    n_pair = GROUP // 2
    qi = lax.broadcasted_iota(jnp.int32, (blk, blk), 0)
    kj = lax.broadcasted_iota(jnp.int32, (blk, blk), 1)
    own = kj <= qi
    low = lax.broadcasted_iota(jnp.int32, (blk, pair_w), 1) < HEAD_DIM
    rows = pl.ds(r0, blk)
    krows = pl.ds(r0, 2 * blk)
    for g in range(n_kv):
        qs = jnp.concatenate(
            [q_ref[rows, (g * n_pair + pp) * pair_w:(g * n_pair + pp + 1) * pair_w]
             for pp in range(n_pair)], axis=0)
        probs, stats = [], []
        for e in range(2):
            s = _dot_t(qs, kpad_ref[krows, (2 * g + e) * pair_w:(2 * g + e + 1) * pair_w])
            p_parts, st = [], []
            for pp in range(n_pair):
                sh = s[pp * blk:(pp + 1) * blk]
                sf = jnp.where(own, sh[:, blk:], -jnp.inf if first else sh[:, :blk])
                sink = sink_ref[g * GROUP + 2 * pp + e] * LOG2E
                m = jnp.maximum(jnp.max(sf, axis=-1, keepdims=True), sink)
                p = jnp.exp2(sf - m)
                p_parts.append(jnp.concatenate([jnp.where(own, 0.0, p), jnp.where(own, p, 0.0)],
                                               axis=1).astype(BF16))
                st.append(jnp.exp2(sink - m))
            probs.append(jnp.concatenate(p_parts, axis=0))
            stats.append(st)
        o = (_dot(probs[0], vpad_ref[krows, (2 * g) * 2 * pair_w:(2 * g + 1) * 2 * pair_w])
             + _dot(probs[1], vpad_ref[krows, (2 * g + 1) * 2 * pair_w:(2 * g + 2) * 2 * pair_w]))
        for pp in range(n_pair):
            oh = o[pp * blk:(pp + 1) * blk]
            den = oh[:, pair_w:] + jnp.where(low, stats[0][pp], stats[1][pp])
            c0 = (g * n_pair + pp) * pair_w
            act_ref[rows, c0:c0 + pair_w] = (
                oh[:, :pair_w] / den * sg_ref[rows, c0:c0 + pair_w].astype(F32)).astype(BF16)


def _store_padded_kv(kv, kpad_ref, vpad_ref, row0, n_kv):
    n = kv.shape[0]
    kvw = n_kv * HEAD_DIM
    pair_w = 2 * HEAD_DIM
    low = lax.broadcasted_iota(jnp.int32, (n, pair_w), 1) < HEAD_DIM
    halves = (low, jnp.logical_not(low))
    for gam in range(n_kv // 2):
        kc = kv[:, gam * pair_w:(gam + 1) * pair_w]
        vc = kv[:, kvw + gam * pair_w:kvw + (gam + 1) * pair_w]
        for side in range(2):
            g = 2 * gam + side
            k_here = jnp.where(halves[side], kc, 0.0)
            v_here = jnp.where(halves[side], vc, 0.0)
            k_other = pltpu.roll(k_here, HEAD_DIM, axis=1)
            v_other = pltpu.roll(v_here, HEAD_DIM, axis=1)
            for e in range(2):
                kc0 = (2 * g + e) * pair_w
                vc0 = (2 * g + e) * 2 * pair_w
                kpad_ref[row0:row0 + n, kc0:kc0 + pair_w] = (k_here if e == side else k_other).astype(BF16)
                vpad_ref[row0:row0 + n, vc0:vc0 + pair_w] = (v_here if e == side else v_other).astype(BF16)
                vpad_ref[row0:row0 + n, vc0 + pair_w:vc0 + 2 * pair_w] = (
                    jnp.where(halves[e], 1.0, 0.0).astype(BF16))


def _attn_prompt_kernel(sink_ref, h_ref, gkv_ref, gpre_ref, gpost_ref, wkv_ref, win_ref, wout_ref,
                        y_ref, ck_ref, cv_ref,
                        ukv_ref, u_ref, q_ref, sg_ref, kpad_ref, vpad_ref, act_ref, *, tm, n_kv):
    i = pl.program_id(1)
    n_i = pl.num_programs(1)
    kvw = n_kv * HEAD_DIM
    blk = WINDOW

    kv = _attn_projections(h_ref, gkv_ref, gpre_ref, wkv_ref, win_ref, ukv_ref, u_ref, q_ref, sg_ref,
                           tm, 4)

    @pl.when(i == 0)
    def _():
        kpad_ref[0:blk, :] = jnp.zeros((blk, kpad_ref.shape[1]), BF16)
        vpad_ref[0:blk, :] = jnp.zeros((blk, vpad_ref.shape[1]), BF16)

    @pl.when(i > 0)
    def _():
        kpad_ref[0:blk, :] = kpad_ref[tm:tm + blk, :]
        vpad_ref[0:blk, :] = vpad_ref[tm:tm + blk, :]

    _store_padded_kv(kv, kpad_ref, vpad_ref, blk, n_kv)

    @pl.when(i == n_i - 1)
    def _():
        ck_ref[...] = kv[tm - blk:, :kvw]
        cv_ref[...] = kv[tm - blk:, kvw:]

    block = functools.partial(_attn_block, sink_ref, q_ref, sg_ref, kpad_ref, vpad_ref, act_ref, n_kv=n_kv)

    @pl.when(i == 0)
    def _():
        block(0, first=True)

    @pl.when(i > 0)
    def _():
        block(0, first=False)

    def blk_body(bi, carry):
        block(pl.multiple_of(bi * blk, blk), first=False)
        return carry
    lax.fori_loop(1, tm // blk, blk_body, 0)

    _attn_output(act_ref, h_ref, gpost_ref, wout_ref, y_ref, tm)


def _attn_layer_prompt(h, sinks, gkv, gpre, gpost, w_kv, w_in, w_out, *, tm):
    b, t, d = h.shape
    kvw = w_kv.shape[1] // 2
    n_kv = kvw // HEAD_DIM
    kern = functools.partial(_attn_prompt_kernel, tm=tm, n_kv=n_kv)
    vec = lambda: _const_spec((1, d), 2)
    return pl.pallas_call(
        kern,
        grid=(b, t // tm),
        in_specs=[
            pl.BlockSpec(memory_space=pltpu.SMEM),
            pl.BlockSpec((None, tm, d), lambda bb, i: (bb, i, 0)),
            vec(), vec(), vec(),
            _const_spec(w_kv.shape, 2),
            _const_spec(w_in.shape, 2),
            _const_spec(w_out.shape, 2),
        ],
        out_specs=[
            pl.BlockSpec((None, tm, d), lambda bb, i: (bb, i, 0)),
            pl.BlockSpec((None, WINDOW, kvw), lambda bb, i: (bb, 0, 0)),
            pl.BlockSpec((None, WINDOW, kvw), lambda bb, i: (bb, 0, 0)),
        ],
        out_shape=[
            jax.ShapeDtypeStruct((b, t, d), F32),
            jax.ShapeDtypeStruct((b, WINDOW, kvw), F32),
            jax.ShapeDtypeStruct((b, WINDOW, kvw), F32),
        ],
        scratch_shapes=[
            pltpu.VMEM((tm, d), BF16),
            pltpu.VMEM((tm, d), BF16),
            pltpu.VMEM((tm, d), BF16),
            pltpu.VMEM((tm, d), BF16),
            pltpu.VMEM((WINDOW + tm, 4 * kvw), BF16),
            pltpu.VMEM((WINDOW + tm, 8 * kvw), BF16),
            pltpu.VMEM((tm, d), BF16),
        ],
        compiler_params=pltpu.CompilerParams(
            dimension_semantics=("arbitrary", "arbitrary"),
            vmem_limit_bytes=VMEM_LIMIT),
        name="attn_layer_prompt",
    )(sinks, h, gkv, gpre, gpost, w_kv, w_in, w_out)


def _attn_sample_proj_kernel(h_ref, gkv_ref, gpre_ref, wkv_ref, win_ref, kv_ref, q_ref, sg_ref,
                             ukv_ref, u_ref, *, tm):
    kv_ref[...] = _attn_projections(h_ref, gkv_ref, gpre_ref, wkv_ref, win_ref, ukv_ref, u_ref, q_ref,
                                    sg_ref, tm, 4)


def _attn_sample_core_kernel(sink_ref, q_ref, sg_ref, kvt_ref, ckt_in_ref, cvt_in_ref,
                             act_ref, ckt_ref, cvt_ref, *, n_seq, t_new, n_kv, n_par):
    kvw = n_kv * HEAD_DIM
    w_buf = ckt_in_ref.shape[-1]
    first_new = w_buf - t_new
    assert kvt_ref.shape[-1] == w_buf, "the new K/V rows of one grid step must fill one lane block"
    n_rows = GROUP * n_kv * t_new
    tq = lax.broadcasted_iota(jnp.int32, (n_rows, w_buf), 0) % t_new
    kj = lax.broadcasted_iota(jnp.int32, (n_rows, w_buf), 1)
    mask_c = kj > tq
    mask_n = (kj >= first_new) & (kj - first_new <= tq)
    is_new = lax.broadcasted_iota(jnp.int32, (kvw, w_buf), 1) >= first_new
    row_head = lax.broadcasted_iota(jnp.int32, (n_rows, 1), 0) // t_new
    sinkv = jnp.zeros((n_rows, 1), F32)
    for h in range(GROUP * n_kv):
        sinkv = jnp.where(row_head == h, sink_ref[h] * LOG2E, sinkv)
    pair_w = 2 * HEAD_DIM
    low = lax.broadcasted_iota(jnp.int32, (t_new, pair_w), 1) < HEAD_DIM
    halves = (low, jnp.logical_not(low))
    zero_blk = jnp.zeros((t_new, pair_w), F32)

    def to_half(block, src_half, dst_half):
        moved = block if src_half == dst_half else pltpu.roll(block, HEAD_DIM, axis=1)
        return jnp.where(halves[dst_half], moved, 0.0)

    def one_seq(s):
        rows = pl.ds(pl.multiple_of(s * t_new, t_new), t_new)
        kct = ckt_in_ref[s].reshape(kvw, w_buf)
        vct = cvt_in_ref[s].reshape(kvw, w_buf)
        newt = pltpu.roll(kvt_ref[...], first_new - s * t_new, axis=1)
        knt = jnp.where(is_new, newt[:kvw], 0.0)
        vnt = jnp.where(is_new, newt[kvw:], 0.0)
        ckt_ref[s] = jnp.where(is_new, knt, pltpu.roll(kct, first_new, axis=1)).reshape(ckt_ref.shape[1:])
        cvt_ref[s] = jnp.where(is_new, vnt, pltpu.roll(vct, first_new, axis=1)).reshape(cvt_ref.shape[1:])
        qparts = []
        for cb in range(GROUP * n_kv // 2):
            blk = q_ref[rows, cb * pair_w:(cb + 1) * pair_w].astype(F32)
            g = 2 * cb // GROUP
            for e in range(2):
                piece = to_half(blk, e, g % 2)
                qparts.append(jnp.concatenate(
                    [piece if kb == g // 2 else zero_blk for kb in range(kvw // pair_w)], axis=1))
        qbd = jnp.concatenate(qparts, axis=0).astype(BF16)
        s_c = jnp.where(mask_c, _dot(qbd, kct.astype(BF16)), -jnp.inf)
        s_n = jnp.where(mask_n, _dot(qbd, knt.astype(BF16)), -jnp.inf)
        m = jnp.maximum(jnp.max(jnp.maximum(s_c, s_n), axis=-1, keepdims=True), sinkv)
        p_c = jnp.exp2(s_c - m)
        p_n = jnp.exp2(s_n - m)
        den = jnp.sum(p_c + p_n, axis=-1, keepdims=True) + jnp.exp2(sinkv - m)
        o = (_dot_t(p_c.astype(BF16), vct.astype(BF16))
             + _dot_t(p_n.astype(BF16), vnt.astype(BF16))) * (1.0 / den)
        for cb in range(GROUP * n_kv // 2):
            g = 2 * cb // GROUP
            kcols = slice((g // 2) * pair_w, (g // 2 + 1) * pair_w)
            pieces = [to_half(o[(2 * cb + e) * t_new:(2 * cb + e + 1) * t_new, kcols], g % 2, e)
                      for e in range(2)]
            cols = slice(cb * pair_w, (cb + 1) * pair_w)
            act_ref[rows, cols] = ((pieces[0] + pieces[1]) * sg_ref[rows, cols].astype(F32)).astype(BF16)

    def seq_body(sb, carry):
        for q in range(n_par):
            one_seq(sb * n_par + q)
        return carry
    lax.fori_loop(0, n_seq // n_par, seq_body, 0)


def _attn_sample_out_kernel(act_ref, h_ref, gpost_ref, wout_ref, y_ref, *, tm):
    _attn_output(act_ref, h_ref, gpost_ref, wout_ref, y_ref, tm)


def _attn_layer_sample(h, cache_k, cache_v, sinks, gkv, gpre, gpost, w_kv, w_in, w_out, *, tm,
                       n_seq_tile):
    n_seq, t_new, d = h.shape
    kvw = w_kv.shape[1] // 2
    n_kv = kvw // HEAD_DIM
    w_buf = cache_k.shape[1]
    n_tok = n_seq * t_new
    h2 = h.reshape(n_tok, d)
    vec = lambda: _const_spec((1, d), 1)
    params = pltpu.CompilerParams(dimension_semantics=("arbitrary",), vmem_limit_bytes=VMEM_LIMIT)

    kv, q, sg = pl.pallas_call(
        functools.partial(_attn_sample_proj_kernel, tm=tm),
        grid=(n_tok // tm,),
        in_specs=[
            pl.BlockSpec((tm, d), lambda i: (i, 0)),
            vec(), vec(),
            _const_spec(w_kv.shape, 1),
            _const_spec(w_in.shape, 1),
        ],
        out_specs=[
            pl.BlockSpec((tm, 2 * kvw), lambda i: (i, 0)),
            pl.BlockSpec((tm, d), lambda i: (i, 0)),
            pl.BlockSpec((tm, d), lambda i: (i, 0)),
        ],
        out_shape=[
            jax.ShapeDtypeStruct((n_tok, 2 * kvw), F32),
            jax.ShapeDtypeStruct((n_tok, d), BF16),
            jax.ShapeDtypeStruct((n_tok, d), BF16),
        ],
        scratch_shapes=[pltpu.VMEM((tm, d), BF16), pltpu.VMEM((tm, d), BF16)],
        compiler_params=params,
        name="attn_sample_proj",
    )(h2, gkv, gpre, w_kv, w_in)

    ckt_in = cache_k.transpose(0, 2, 3, 1)
    cvt_in = cache_v.transpose(0, 2, 3, 1)
    rows = n_seq_tile * t_new
    cache_blk = pl.BlockSpec((n_seq_tile, n_kv, HEAD_DIM, w_buf), lambda i: (i, 0, 0, 0))
    act, ckt, cvt = pl.pallas_call(
        functools.partial(_attn_sample_core_kernel, n_seq=n_seq_tile, t_new=t_new, n_kv=n_kv, n_par=4),
        grid=(n_seq // n_seq_tile,),
        in_specs=[
            pl.BlockSpec(memory_space=pltpu.SMEM),
            pl.BlockSpec((rows, d), lambda i: (i, 0)),
            pl.BlockSpec((rows, d), lambda i: (i, 0)),
            pl.BlockSpec((2 * kvw, rows), lambda i: (0, i)),
            cache_blk, cache_blk,
        ],
        out_specs=[pl.BlockSpec((rows, d), lambda i: (i, 0)), cache_blk, cache_blk],
        out_shape=[
            jax.ShapeDtypeStruct((n_tok, d), BF16),
            jax.ShapeDtypeStruct(ckt_in.shape, F32),
            jax.ShapeDtypeStruct(cvt_in.shape, F32),
        ],
        compiler_params=params,
        name="attn_sample_core",
    )(sinks, q, sg, kv.T, ckt_in, cvt_in)
    ck = ckt.transpose(0, 3, 1, 2)
    cv = cvt.transpose(0, 3, 1, 2)

    y = pl.pallas_call(
        functools.partial(_attn_sample_out_kernel, tm=tm),
        grid=(n_tok // tm,),
        in_specs=[
            pl.BlockSpec((tm, d), lambda i: (i, 0)),
            pl.BlockSpec((tm, d), lambda i: (i, 0)),
            vec(),
            _const_spec(w_out.shape, 1),
        ],
        out_specs=pl.BlockSpec((tm, d), lambda i: (i, 0)),
        out_shape=jax.ShapeDtypeStruct((n_tok, d), F32),
        compiler_params=params,
        name="attn_sample_out",
    )(act, h2, gpost, w_out)
    return y.reshape(n_seq, t_new, d), ck, cv


def kernel(x_prompt, x_sample, state_conv, cache_k, cache_v, norm_pre, norm_post, w_in_a, conv_w, conv_b, ln_g, ln_b, w_out_a, kv_norm, w_kv, w_in_b, sinks, w_out_b):
    n_a = w_in_a.shape[0]
    assert n_a == 1 and w_in_b.shape[0] == 1 and norm_pre.shape[0] == 2
    d = x_prompt.shape[-1]
    n_seq, w_buf, n_kv, hd = cache_k.shape
    assert hd == HEAD_DIM and w_buf == WINDOW

    row = lambda v: v.reshape(1, -1)
    w_in_a_bf = w_in_a[0].astype(BF16)
    w_out_a_bf = w_out_a[0].astype(BF16)
    w_kv_bf = w_kv.astype(BF16)
    w_in_b_bf = w_in_b[0].astype(BF16)
    w_out_b_bf = w_out_b[0].astype(BF16)
    conv_args = (w_in_a_bf, conv_w[0], row(conv_b[0]), row(ln_g[0]), row(ln_b[0]), row(norm_pre[0]),
                 row(norm_post[0]), w_out_a_bf)
    attn_args = (sinks[0], row(kv_norm), row(norm_pre[1]), row(norm_post[1]), w_kv_bf, w_in_b_bf,
                 w_out_b_bf)

    h_p, st_p = _conv_layer_prompt(x_prompt, *conv_args, tm=512, cn=256)
    y_p, ck_p, cv_p = _attn_layer_prompt(h_p, *attn_args, tm=256)

    h_s, st_s = _conv_layer_sample(x_sample, state_conv[0], *conv_args, n_seq_tile=64, cn=256)
    y_s, ck_s, cv_s = _attn_layer_sample(
        h_s, cache_k, cache_v,
        *attn_args, tm=512, n_seq_tile=16)

    b = x_prompt.shape[0]
    return (y_p, y_s, st_p[None], ck_p.reshape(b, w_buf, n_kv, hd), cv_p.reshape(b, w_buf, n_kv, hd),
            st_s[None], ck_s, cv_s)
```

```python
import functools

import jax
import jax.numpy as jnp
from jax import lax
from jax.experimental import pallas as pl
from jax.experimental.pallas import tpu as pltpu

RMS_EPS = 1e-6
LN_EPS = 1e-5
HEAD_DIM = 64
GROUP = 8
WINDOW = 128
SUBLANES = 8
LANES = 128
LOG2E = 1.4426950408889634
HALO = 32
VMEM_LIMIT = 56 * 1024 * 1024
ROW_BLK = 64

BF16 = jnp.bfloat16
F32 = jnp.float32


def _rows(i, n):
    return pl.ds(pl.multiple_of(i * n, n), n)


def _rms_scale(x):
    return lax.rsqrt(jnp.mean(x * x, axis=-1, keepdims=True) + RMS_EPS)


def _dot(a, b):
    return jnp.dot(a, b, preferred_element_type=F32)


def _dot_t(a, b):
    return lax.dot_general(a, b, (((1,), (1,)), ((), ())), preferred_element_type=F32)


def _pre_norm_to_bf16(x_ref, g_ref, u_ref, tm):
    def body(r, c):
        rows = _rows(r, ROW_BLK)
        x = x_ref[rows, :]
        u_ref[rows, :] = (x * _rms_scale(x) * g_ref[...]).astype(BF16)
        return c
    lax.fori_loop(0, tm // ROW_BLK, body, 0, unroll=2)


def _glu_chunk(u_ref, wa_ref, wb_ref, wg_ref):
    u = u_ref[...]
    c = _dot(u, wa_ref[...]) * jax.nn.sigmoid(_dot(u, wb_ref[...]))
    sg = jax.nn.silu(_dot(u, wg_ref[...])).astype(BF16)
    return c, sg


def _broadcast_taps(cw_ref, wbc_ref, taps):
    for k in range(taps):
        wbc_ref[k] = jnp.broadcast_to(cw_ref[k:k + 1, :], wbc_ref.shape[1:])


def _conv_finalize(j_chunks, tm, cn, y_ref, sg_ref, act_ref, x_ref, h_ref, cb_ref, lng_ref, lnb_ref,
                   gpost_ref, wout_ref):
    d = j_chunks * cn

    ln_rows = ROW_BLK

    def ln_body(r, c):
        rows = _rows(r, ln_rows)
        ys = [y_ref[jj, rows, :] + cb_ref[:, jj * cn:(jj + 1) * cn] for jj in range(j_chunks)]
        mu = jnp.sum(sum(ys), axis=-1, keepdims=True) * (1.0 / d)
        yc = [y - mu for y in ys]
        var = jnp.sum(sum(y * y for y in yc), axis=-1, keepdims=True) * (1.0 / d)
        rstd = lax.rsqrt(var + LN_EPS)
        for jj in range(j_chunks):
            cols = slice(jj * cn, (jj + 1) * cn)
            t = jax.nn.silu(yc[jj] * rstd * lng_ref[:, cols] + lnb_ref[:, cols])
            act_ref[rows, cols] = (t * sg_ref[jj, rows, :].astype(F32)).astype(BF16)
        return c
    lax.fori_loop(0, tm // ln_rows, ln_body, 0, unroll=2)

    h_ref[...] = _dot(act_ref[...], wout_ref[...])

    def post_body(r, c):
        rows = [_rows(2 * r + e, ROW_BLK) for e in range(2)]
        os = [h_ref[rw, :] for rw in rows]
        res = [x_ref[rw, :] + o * _rms_scale(o) * gpost_ref[...] for rw, o in zip(rows, os)]
        for rw, v in zip(rows, res):
            h_ref[rw, :] = v
        return c
    lax.fori_loop(0, tm // (2 * ROW_BLK), post_body, 0)


def _conv_chunk(cext_ref, wbc_ref, y_ref, jc, *, tm, cn, taps, grp):
    lbs = cn // LANES
    assert lbs >= 2, "the row-interleaved layout needs at least two lane blocks per chunk"
    lead = HALO - (taps - 1)
    for g in range(tm // (grp * SUBLANES)):
        base = g * grp * SUBLANES
        for lb in range(lbs):
            ls = slice(lb * LANES, (lb + 1) * LANES)
            acc = [None] * grp
            for k in range(taps):
                wv = wbc_ref[jc, k, :, ls]
                for gi in range(grp):
                    r0 = base + SUBLANES * gi + k + lead
                    t = wv * cext_ref[jc, pl.ds(lbs * r0 + lb, SUBLANES, stride=lbs), :]
                    acc[gi] = t if acc[gi] is None else acc[gi] + t
            for gi in range(grp):
                y_ref[jc, base + SUBLANES * gi:base + SUBLANES * (gi + 1), ls] = acc[gi]


def _conv_prompt_kernel(x_ref, w_hbm, cw_ref, cb_ref, lng_ref, lnb_ref, gpre_ref, gpost_ref, wout_ref,
                        h_ref, st_ref,
                        u_ref, wbuf_ref, wsem, cext_ref, wbc_ref, y_ref, sg_ref, act_ref,
                        *, tm, cn, taps, grp):
    bb = pl.program_id(0)
    i = pl.program_id(1)
    n_chunks = cext_ref.shape[0]
    lbs = cn // LANES
    first_step = (bb == 0) & (i == 0)
    last_step = (bb == pl.num_programs(0) - 1) & (i == pl.num_programs(1) - 1)

    def weight_copy(jc, slot, part):
        col0 = pl.multiple_of((part * n_chunks + jc) * cn, cn)
        return pltpu.make_async_copy(w_hbm.at[:, pl.ds(col0, cn)], wbuf_ref.at[slot, part],
                                     wsem.at[slot, part])

    @pl.when(first_step)
    def _():
        for part in range(3):
            weight_copy(0, 0, part).start()
        for jj in range(n_chunks):
            for k in range(taps):
                wbc_ref[jj, k] = jnp.broadcast_to(cw_ref[k:k + 1, jj * cn:(jj + 1) * cn], (SUBLANES, cn))

    _pre_norm_to_bf16(x_ref, gpre_ref, u_ref, tm)

    def chunk_body(j, carry):
        slot = j % 2
        for part in range(3):
            weight_copy(j, slot, part).wait()

        @pl.when(jnp.logical_not(last_step & (j == n_chunks - 1)))
        def _():
            nxt = jnp.where(j == n_chunks - 1, 0, j + 1)
            for part in range(3):
                weight_copy(nxt, 1 - slot, part).start()

        @pl.when(i == 0)
        def _():
            cext_ref[j, 0:lbs * HALO, :] = jnp.zeros((lbs * HALO, LANES), F32)

        @pl.when(i > 0)
        def _():
            cext_ref[j, 0:lbs * HALO, :] = cext_ref[j, lbs * tm:lbs * (tm + HALO), :]

        u = u_ref[...]
        c = _dot(u, wbuf_ref[slot, 0]) * jax.nn.sigmoid(_dot(u, wbuf_ref[slot, 1]))
        sg_ref[j] = jax.nn.silu(_dot(u, wbuf_ref[slot, 2])).astype(BF16)
        for lb in range(lbs):
            cext_ref[j, pl.ds(lbs * HALO + lb, tm, stride=lbs), :] = c[:, lb * LANES:(lb + 1) * LANES]
        _conv_chunk(cext_ref, wbc_ref, y_ref, j, tm=tm, cn=cn, taps=taps, grp=grp)
        return carry
    lax.fori_loop(0, n_chunks, chunk_body, 0)

    _conv_finalize(n_chunks, tm, cn, y_ref, sg_ref, act_ref, x_ref, h_ref, cb_ref,
                   lng_ref, lnb_ref, gpost_ref, wout_ref)

    @pl.when(i == pl.num_programs(1) - 1)
    def _():
        tail = lbs * (HALO + tm - (taps - 1))
        for jj in range(n_chunks):
            for lb in range(lbs):
                c0 = jj * cn + lb * LANES
                st_ref[:, c0:c0 + LANES] = cext_ref[jj, pl.ds(tail + lb, taps - 1, stride=lbs), :]


def _conv_sample_kernel(x_ref, st_in_ref, wa_ref, wb_ref, wg_ref, cw_ref, cb_ref, lng_ref, lnb_ref,
                        gpre_ref, gpost_ref, wout_ref, h_ref, st_ref,
                        u_ref, c_ref, ystg_ref, wbc_ref, y_ref, sg_ref, act_ref, *, tm, cn, taps, t_new):
    j = pl.program_id(1)
    n_j = pl.num_programs(1)
    n_seq = tm // t_new
    hist = taps - 1

    @pl.when(j == 0)
    def _():
        _pre_norm_to_bf16(x_ref, gpre_ref, u_ref, tm)

    c, sg = _glu_chunk(u_ref, wa_ref, wb_ref, wg_ref)
    sg_ref[j] = sg
    _broadcast_taps(cw_ref, wbc_ref, taps)

    for lb in range(cn // LANES):
        ls = slice(lb * LANES, (lb + 1) * LANES)
        c_ref[lb] = c[:, ls]
        new_rows = [c_ref[lb, pl.ds(t, n_seq, stride=t_new), :] for t in range(t_new)]
        for r in range(hist):
            st_ref[r, :, ls] = st_in_ref[r + t_new, :, ls] if r + t_new < hist else new_rows[r + t_new - hist]
        for sb in range(n_seq // SUBLANES):
            seqs = slice(sb * SUBLANES, (sb + 1) * SUBLANES)
            full = [st_in_ref[r, seqs, ls] for r in range(hist)] + [nr[seqs] for nr in new_rows]
            acc = [None] * t_new
            for k in range(taps):
                wv = wbc_ref[k, :, ls]
                for t in range(t_new):
                    term = wv * full[t + k]
                    acc[t] = term if acc[t] is None else acc[t] + term
            for t in range(t_new):
                ystg_ref[lb, pl.ds(sb * SUBLANES * t_new + t, SUBLANES, stride=t_new), :] = acc[t]
        y_ref[j, :, ls] = ystg_ref[lb]

    @pl.when(j == n_j - 1)
    def _():
        _conv_finalize(y_ref.shape[0], tm, cn, y_ref, sg_ref, act_ref, x_ref, h_ref, cb_ref,
                       lng_ref, lnb_ref, gpost_ref, wout_ref)


def _const_spec(shape, n_grid):
    zeros = (0,) * len(shape)
    return pl.BlockSpec(shape, lambda *_: zeros, pipeline_mode=pl.Buffered(1))


def _conv_layer_prompt(x, w_in, cw, cb, lng, lnb, gpre, gpost, w_out, *, tm, cn):
    b, t, d = x.shape
    taps = cw.shape[0]
    n_j = d // cn
    lbs = cn // LANES
    grp = 4
    kern = functools.partial(_conv_prompt_kernel, tm=tm, cn=cn, taps=taps, grp=grp)
    vec = lambda: _const_spec((1, d), 2)
    return pl.pallas_call(
        kern,
        grid=(b, t // tm),
        in_specs=[
            pl.BlockSpec((None, tm, d), lambda bb, i: (bb, i, 0)),
            pl.BlockSpec(memory_space=pl.ANY),
            _const_spec((taps, d), 2),
            vec(), vec(), vec(), vec(), vec(),
            _const_spec((d, d), 2),
        ],
        out_specs=[
            pl.BlockSpec((None, tm, d), lambda bb, i: (bb, i, 0)),
            pl.BlockSpec((None, taps - 1, d), lambda bb, i: (bb, 0, 0)),
        ],
        out_shape=[
            jax.ShapeDtypeStruct((b, t, d), F32),
            jax.ShapeDtypeStruct((b, taps - 1, d), F32),
        ],
        scratch_shapes=[
            pltpu.VMEM((tm, d), BF16),
            pltpu.VMEM((2, 3, d, cn), BF16),
            pltpu.SemaphoreType.DMA((2, 3)),
            pltpu.VMEM((n_j, lbs * (HALO + tm), LANES), F32),
            pltpu.VMEM((n_j, taps, SUBLANES, cn), F32),
            pltpu.VMEM((n_j, tm, cn), F32),
            pltpu.VMEM((n_j, tm, cn), BF16),
            pltpu.VMEM((tm, d), BF16),
        ],
        compiler_params=pltpu.CompilerParams(
            dimension_semantics=("arbitrary", "arbitrary"),
            vmem_limit_bytes=VMEM_LIMIT),
        name="conv_layer_prompt",
    )(x, w_in, cw, cb, lng, lnb, gpre, gpost, w_out)


def _conv_layer_sample(x, state, w_in, cw, cb, lng, lnb, gpre, gpost, w_out, *, n_seq_tile, cn):
    n_seq, t_new, d = x.shape
    taps = cw.shape[0]
    n_j = d // cn
    tm = n_seq_tile * t_new
    x2 = x.reshape(n_seq * t_new, d)
    kern = functools.partial(_conv_sample_kernel, tm=tm, cn=cn, taps=taps, t_new=t_new)
    vec = lambda: _const_spec((1, d), 2)
    h, st = pl.pallas_call(
        kern,
        grid=(n_seq // n_seq_tile, n_j),
        in_specs=[
            pl.BlockSpec((tm, d), lambda i, j: (i, 0)),
            pl.BlockSpec((taps - 1, n_seq_tile, cn), lambda i, j: (0, i, j)),
            pl.BlockSpec((d, cn), lambda i, j: (0, j)),
            pl.BlockSpec((d, cn), lambda i, j: (0, n_j + j)),
            pl.BlockSpec((d, cn), lambda i, j: (0, 2 * n_j + j)),
            pl.BlockSpec((taps, cn), lambda i, j: (0, j)),
            vec(), vec(), vec(), vec(), vec(),
            _const_spec((d, d), 2),
        ],
        out_specs=[
            pl.BlockSpec((tm, d), lambda i, j: (i, 0)),
            pl.BlockSpec((taps - 1, n_seq_tile, cn), lambda i, j: (0, i, j)),
        ],
        out_shape=[
            jax.ShapeDtypeStruct((n_seq * t_new, d), F32),
            jax.ShapeDtypeStruct((taps - 1, n_seq, d), F32),
        ],
        scratch_shapes=[
            pltpu.VMEM((tm, d), BF16),
            pltpu.VMEM((cn // LANES, tm, LANES), F32),
            pltpu.VMEM((cn // LANES, tm, LANES), F32),
            pltpu.VMEM((taps, SUBLANES, cn), F32),
            pltpu.VMEM((n_j, tm, cn), F32),
            pltpu.VMEM((n_j, tm, cn), BF16),
            pltpu.VMEM((tm, d), BF16),
        ],
        compiler_params=pltpu.CompilerParams(
            dimension_semantics=("arbitrary", "arbitrary"),
            vmem_limit_bytes=VMEM_LIMIT),
        name="conv_layer_sample",
    )(x2, state.transpose(1, 0, 2), w_in, w_in, w_in, cw, cb, lng, lnb, gpre, gpost, w_out)
    return h.reshape(n_seq, t_new, d), st.transpose(1, 0, 2)


def _attn_projections(h_ref, gkv_ref, gpre_ref, wkv_ref, win_ref, ukv_ref, u_ref, q_ref, sg_ref, tm,
                      n_chunk):
    d = h_ref.shape[-1]

    def body(r, c):
        rows = _rows(r, ROW_BLK)
        x = h_ref[rows, :]
        xn = x * _rms_scale(x)
        ukv_ref[rows, :] = (xn * gkv_ref[...]).astype(BF16)
        u_ref[rows, :] = (xn * gpre_ref[...]).astype(BF16)
        return c
    lax.fori_loop(0, tm // ROW_BLK, body, 0, unroll=2)

    kv = _dot(ukv_ref[...], wkv_ref[...])
    u = u_ref[...]
    cw = d // n_chunk
    scale = HEAD_DIM ** -0.5 * LOG2E
    for n in range(n_chunk):
        cols = slice(n * cw, (n + 1) * cw)
        q_ref[:, cols] = (_dot(u, win_ref[:, cols]) * scale).astype(BF16)
    for n in range(n_chunk):
        cols = slice(n * cw, (n + 1) * cw)
        sg_ref[:, cols] = jax.nn.silu(_dot(u, win_ref[:, d + n * cw:d + (n + 1) * cw])).astype(BF16)
    return kv


def _attn_output(act_ref, h_ref, gpost_ref, wout_ref, y_ref, tm):
    y_ref[...] = _dot(act_ref[...], wout_ref[...])

    def post_body(r, c):
        rows = [_rows(2 * r + e, ROW_BLK) for e in range(2)]
        os = [y_ref[rw, :] for rw in rows]
        res = [h_ref[rw, :] + o * _rms_scale(o) * gpost_ref[...] for rw, o in zip(rows, os)]
        for rw, v in zip(rows, res):
            y_ref[rw, :] = v
        return c
    lax.fori_loop(0, tm // (2 * ROW_BLK), post_body, 0)


def _attn_block(sink_ref, q_ref, sg_ref, kpad_ref, vpad_ref, act_ref, r0, *, n_kv, first):
    blk = WINDOW
    pair_w = 2 * HEAD_DIM
    n_pair = GROUP // 2
    qi = lax.broadcasted_iota(jnp.int32, (blk, blk), 0)
    kj = lax.broadcasted_iota(jnp.int32, (blk, blk), 1)
    own = kj <= qi
    low = lax.broadcasted_iota(jnp.int32, (blk, pair_w), 1) < HEAD_DIM
    rows = pl.ds(r0, blk)
    krows = pl.ds(r0, 2 * blk)
    for g in range(n_kv):
        qs = jnp.concatenate(
            [q_ref[rows, (g * n_pair + pp) * pair_w:(g * n_pair + pp + 1) * pair_w]
             for pp in range(n_pair)], axis=0)
        probs, stats = [], []
        for e in range(2):
            s = _dot_t(qs, kpad_ref[krows, (2 * g + e) * pair_w:(2 * g + e + 1) * pair_w])
            p_parts, st = [], []
            for pp in range(n_pair):
                sh = s[pp * blk:(pp + 1) * blk]
                sf = jnp.where(own, sh[:, blk:], -jnp.inf if first else sh[:, :blk])
                sink = sink_ref[g * GROUP + 2 * pp + e] * LOG2E
                m = jnp.maximum(jnp.max(sf, axis=-1, keepdims=True), sink)
                p = jnp.exp2(sf - m)
                p_parts.append(jnp.concatenate([jnp.where(own, 0.0, p), jnp.where(own, p, 0.0)],
                                               axis=1).astype(BF16))
                st.append(jnp.exp2(sink - m))
            probs.append(jnp.concatenate(p_parts, axis=0))
            stats.append(st)
        o = (_dot(probs[0], vpad_ref[krows, (2 * g) * 2 * pair_w:(2 * g + 1) * 2 * pair_w])
             + _dot(probs[1], vpad_ref[krows, (2 * g + 1) * 2 * pair_w:(2 * g + 2) * 2 * pair_w]))
        for pp in range(n_pair):
            oh = o[pp * blk:(pp + 1) * blk]
            den = oh[:, pair_w:] + jnp.where(low, stats[0][pp], stats[1][pp])
            c0 = (g * n_pair + pp) * pair_w
            act_ref[rows, c0:c0 + pair_w] = (
                oh[:, :pair_w] / den * sg_ref[rows, c0:c0 + pair_w].astype(F32)).astype(BF16)


def _store_padded_kv(kv, kpad_ref, vpad_ref, row0, n_kv):
    n = kv.shape[0]
    kvw = n_kv * HEAD_DIM
    pair_w = 2 * HEAD_DIM
    low = lax.broadcasted_iota(jnp.int32, (n, pair_w), 1) < HEAD_DIM
    halves = (low, jnp.logical_not(low))
    for gam in range(n_kv // 2):
        kc = kv[:, gam * pair_w:(gam + 1) * pair_w]
        vc = kv[:, kvw + gam * pair_w:kvw + (gam + 1) * pair_w]
        for side in range(2):
            g = 2 * gam + side
            k_here = jnp.where(halves[side], kc, 0.0)
            v_here = jnp.where(halves[side], vc, 0.0)
            k_other = pltpu.roll(k_here, HEAD_DIM, axis=1)
            v_other = pltpu.roll(v_here, HEAD_DIM, axis=1)
            for e in range(2):
                kc0 = (2 * g + e) * pair_w
                vc0 = (2 * g + e) * 2 * pair_w
                kpad_ref[row0:row0 + n, kc0:kc0 + pair_w] = (k_here if e == side else k_other).astype(BF16)
                vpad_ref[row0:row0 + n, vc0:vc0 + pair_w] = (v_here if e == side else v_other).astype(BF16)
                vpad_ref[row0:row0 + n, vc0 + pair_w:vc0 + 2 * pair_w] = (
                    jnp.where(halves[e], 1.0, 0.0).astype(BF16))


def _attn_prompt_kernel(sink_ref, h_ref, gkv_ref, gpre_ref, gpost_ref, wkv_ref, win_ref, wout_ref,
                        y_ref, ck_ref, cv_ref,
                        ukv_ref, u_ref, q_ref, sg_ref, kpad_ref, vpad_ref, act_ref, *, tm, n_kv):
    i = pl.program_id(1)
    n_i = pl.num_programs(1)
    kvw = n_kv * HEAD_DIM
    blk = WINDOW

    kv = _attn_projections(h_ref, gkv_ref, gpre_ref, wkv_ref, win_ref, ukv_ref, u_ref, q_ref, sg_ref,
                           tm, 4)

    @pl.when(i == 0)
    def _():
        kpad_ref[0:blk, :] = jnp.zeros((blk, kpad_ref.shape[1]), BF16)
        vpad_ref[0:blk, :] = jnp.zeros((blk, vpad_ref.shape[1]), BF16)

    @pl.when(i > 0)
    def _():
        kpad_ref[0:blk, :] = kpad_ref[tm:tm + blk, :]
        vpad_ref[0:blk, :] = vpad_ref[tm:tm + blk, :]

    _store_padded_kv(kv, kpad_ref, vpad_ref, blk, n_kv)

    @pl.when(i == n_i - 1)
    def _():
        ck_ref[...] = kv[tm - blk:, :kvw]
        cv_ref[...] = kv[tm - blk:, kvw:]

    block = functools.partial(_attn_block, sink_ref, q_ref, sg_ref, kpad_ref, vpad_ref, act_ref, n_kv=n_kv)

    @pl.when(i == 0)
    def _():
        block(0, first=True)

    @pl.when(i > 0)
    def _():
        block(0, first=False)

    def blk_body(bi, carry):
        block(pl.multiple_of(bi * blk, blk), first=False)
        return carry
    lax.fori_loop(1, tm // blk, blk_body, 0)

    _attn_output(act_ref, h_ref, gpost_ref, wout_ref, y_ref, tm)


def _attn_layer_prompt(h, sinks, gkv, gpre, gpost, w_kv, w_in, w_out, *, tm):
    b, t, d = h.shape
    kvw = w_kv.shape[1] // 2
    n_kv = kvw // HEAD_DIM
    kern = functools.partial(_attn_prompt_kernel, tm=tm, n_kv=n_kv)
    vec = lambda: _const_spec((1, d), 2)
    return pl.pallas_call(
        kern,
        grid=(b, t // tm),
        in_specs=[
            pl.BlockSpec(memory_space=pltpu.SMEM),
            pl.BlockSpec((None, tm, d), lambda bb, i: (bb, i, 0)),
            vec(), vec(), vec(),
            _const_spec(w_kv.shape, 2),
            _const_spec(w_in.shape, 2),
            _const_spec(w_out.shape, 2),
        ],
        out_specs=[
            pl.BlockSpec((None, tm, d), lambda bb, i: (bb, i, 0)),
            pl.BlockSpec((None, WINDOW, kvw), lambda bb, i: (bb, 0, 0)),
            pl.BlockSpec((None, WINDOW, kvw), lambda bb, i: (bb, 0, 0)),
        ],
        out_shape=[
            jax.ShapeDtypeStruct((b, t, d), F32),
            jax.ShapeDtypeStruct((b, WINDOW, kvw), F32),
            jax.ShapeDtypeStruct((b, WINDOW, kvw), F32),
        ],
        scratch_shapes=[
            pltpu.VMEM((tm, d), BF16),
            pltpu.VMEM((tm, d), BF16),
            pltpu.VMEM((tm, d), BF16),
            pltpu.VMEM((tm, d), BF16),
            pltpu.VMEM((WINDOW + tm, 4 * kvw), BF16),
            pltpu.VMEM((WINDOW + tm, 8 * kvw), BF16),
            pltpu.VMEM((tm, d), BF16),
        ],
        compiler_params=pltpu.CompilerParams(
            dimension_semantics=("arbitrary", "arbitrary"),
            vmem_limit_bytes=VMEM_LIMIT),
        name="attn_layer_prompt",
    )(sinks, h, gkv, gpre, gpost, w_kv, w_in, w_out)


def _attn_sample_proj_kernel(h_ref, gkv_ref, gpre_ref, wkv_ref, win_ref, kv_ref, q_ref, sg_ref,
                             ukv_ref, u_ref, *, tm):
    kv_ref[...] = _attn_projections(h_ref, gkv_ref, gpre_ref, wkv_ref, win_ref, ukv_ref, u_ref, q_ref,
                                    sg_ref, tm, 4)


def _attn_sample_core_kernel(sink_ref, q_ref, sg_ref, kvt_ref, ckt_in_ref, cvt_in_ref,
                             act_ref, ckt_ref, cvt_ref, *, n_seq, t_new, n_kv, n_par):
    kvw = n_kv * HEAD_DIM
    w_buf = ckt_in_ref.shape[-1]
    first_new = w_buf - t_new
    assert kvt_ref.shape[-1] == w_buf, "the new K/V rows of one grid step must fill one lane block"
    n_rows = GROUP * n_kv * t_new
    tq = lax.broadcasted_iota(jnp.int32, (n_rows, w_buf), 0) % t_new
    kj = lax.broadcasted_iota(jnp.int32, (n_rows, w_buf), 1)
    mask_c = kj > tq
    mask_n = (kj >= first_new) & (kj - first_new <= tq)
    is_new = lax.broadcasted_iota(jnp.int32, (kvw, w_buf), 1) >= first_new
    row_head = lax.broadcasted_iota(jnp.int32, (n_rows, 1), 0) // t_new
    sinkv = jnp.zeros((n_rows, 1), F32)
    for h in range(GROUP * n_kv):
        sinkv = jnp.where(row_head == h, sink_ref[h] * LOG2E, sinkv)
    pair_w = 2 * HEAD_DIM
    low = lax.broadcasted_iota(jnp.int32, (t_new, pair_w), 1) < HEAD_DIM
    halves = (low, jnp.logical_not(low))
    zero_blk = jnp.zeros((t_new, pair_w), F32)

    def to_half(block, src_half, dst_half):
        moved = block if src_half == dst_half else pltpu.roll(block, HEAD_DIM, axis=1)
        return jnp.where(halves[dst_half], moved, 0.0)

    def one_seq(s):
        rows = pl.ds(pl.multiple_of(s * t_new, t_new), t_new)
        kct = ckt_in_ref[s].reshape(kvw, w_buf)
        vct = cvt_in_ref[s].reshape(kvw, w_buf)
        newt = pltpu.roll(kvt_ref[...], first_new - s * t_new, axis=1)
        knt = jnp.where(is_new, newt[:kvw], 0.0)
        vnt = jnp.where(is_new, newt[kvw:], 0.0)
        ckt_ref[s] = jnp.where(is_new, knt, pltpu.roll(kct, first_new, axis=1)).reshape(ckt_ref.shape[1:])
        cvt_ref[s] = jnp.where(is_new, vnt, pltpu.roll(vct, first_new, axis=1)).reshape(cvt_ref.shape[1:])
        qparts = []
        for cb in range(GROUP * n_kv // 2):
            blk = q_ref[rows, cb * pair_w:(cb + 1) * pair_w].astype(F32)
            g = 2 * cb // GROUP
            for e in range(2):
                piece = to_half(blk, e, g % 2)
                qparts.append(jnp.concatenate(
                    [piece if kb == g // 2 else zero_blk for kb in range(kvw // pair_w)], axis=1))
        qbd = jnp.concatenate(qparts, axis=0).astype(BF16)
        s_c = jnp.where(mask_c, _dot(qbd, kct.astype(BF16)), -jnp.inf)
        s_n = jnp.where(mask_n, _dot(qbd, knt.astype(BF16)), -jnp.inf)
        m = jnp.maximum(jnp.max(jnp.maximum(s_c, s_n), axis=-1, keepdims=True), sinkv)
        p_c = jnp.exp2(s_c - m)
        p_n = jnp.exp2(s_n - m)
        den = jnp.sum(p_c + p_n, axis=-1, keepdims=True) + jnp.exp2(sinkv - m)
        o = (_dot_t(p_c.astype(BF16), vct.astype(BF16))
             + _dot_t(p_n.astype(BF16), vnt.astype(BF16))) * (1.0 / den)
        for cb in range(GROUP * n_kv // 2):
            g = 2 * cb // GROUP
            kcols = slice((g // 2) * pair_w, (g // 2 + 1) * pair_w)
            pieces = [to_half(o[(2 * cb + e) * t_new:(2 * cb + e + 1) * t_new, kcols], g % 2, e)
                      for e in range(2)]
            cols = slice(cb * pair_w, (cb + 1) * pair_w)
            act_ref[rows, cols] = ((pieces[0] + pieces[1]) * sg_ref[rows, cols].astype(F32)).astype(BF16)

    def seq_body(sb, carry):
        for q in range(n_par):
            one_seq(sb * n_par + q)
        return carry
    lax.fori_loop(0, n_seq // n_par, seq_body, 0)


def _attn_sample_out_kernel(act_ref, h_ref, gpost_ref, wout_ref, y_ref, *, tm):
    _attn_output(act_ref, h_ref, gpost_ref, wout_ref, y_ref, tm)


def _attn_layer_sample(h, cache_k, cache_v, sinks, gkv, gpre, gpost, w_kv, w_in, w_out, *, tm,
                       n_seq_tile):
    n_seq, t_new, d = h.shape
    kvw = w_kv.shape[1] // 2
    n_kv = kvw // HEAD_DIM
    w_buf = cache_k.shape[1]
    n_tok = n_seq * t_new
    h2 = h.reshape(n_tok, d)
    vec = lambda: _const_spec((1, d), 1)
    params = pltpu.CompilerParams(dimension_semantics=("arbitrary",), vmem_limit_bytes=VMEM_LIMIT)

    kv, q, sg = pl.pallas_call(
        functools.partial(_attn_sample_proj_kernel, tm=tm),
        grid=(n_tok // tm,),
        in_specs=[
            pl.BlockSpec((tm, d), lambda i: (i, 0)),
            vec(), vec(),
            _const_spec(w_kv.shape, 1),
            _const_spec(w_in.shape, 1),
        ],
        out_specs=[
            pl.BlockSpec((tm, 2 * kvw), lambda i: (i, 0)),
            pl.BlockSpec((tm, d), lambda i: (i, 0)),
            pl.BlockSpec((tm, d), lambda i: (i, 0)),
        ],
        out_shape=[
            jax.ShapeDtypeStruct((n_tok, 2 * kvw), F32),
            jax.ShapeDtypeStruct((n_tok, d), BF16),
            jax.ShapeDtypeStruct((n_tok, d), BF16),
        ],
        scratch_shapes=[pltpu.VMEM((tm, d), BF16), pltpu.VMEM((tm, d), BF16)],
        compiler_params=params,
        name="attn_sample_proj",
    )(h2, gkv, gpre, w_kv, w_in)

    ckt_in = cache_k.transpose(0, 2, 3, 1)
    cvt_in = cache_v.transpose(0, 2, 3, 1)
    rows = n_seq_tile * t_new
    cache_blk = pl.BlockSpec((n_seq_tile, n_kv, HEAD_DIM, w_buf), lambda i: (i, 0, 0, 0))
    act, ckt, cvt = pl.pallas_call(
        functools.partial(_attn_sample_core_kernel, n_seq=n_seq_tile, t_new=t_new, n_kv=n_kv, n_par=4),
        grid=(n_seq // n_seq_tile,),
        in_specs=[
            pl.BlockSpec(memory_space=pltpu.SMEM),
            pl.BlockSpec((rows, d), lambda i: (i, 0)),
            pl.BlockSpec((rows, d), lambda i: (i, 0)),
            pl.BlockSpec((2 * kvw, rows), lambda i: (0, i)),
            cache_blk, cache_blk,
        ],
        out_specs=[pl.BlockSpec((rows, d), lambda i: (i, 0)), cache_blk, cache_blk],
        out_shape=[
            jax.ShapeDtypeStruct((n_tok, d), BF16),
            jax.ShapeDtypeStruct(ckt_in.shape, F32),
            jax.ShapeDtypeStruct(cvt_in.shape, F32),
        ],
        compiler_params=params,
        name="attn_sample_core",
    )(sinks, q, sg, kv.T, ckt_in, cvt_in)
    ck = ckt.transpose(0, 3, 1, 2)
    cv = cvt.transpose(0, 3, 1, 2)

    y = pl.pallas_call(
        functools.partial(_attn_sample_out_kernel, tm=tm),
        grid=(n_tok // tm,),
        in_specs=[
            pl.BlockSpec((tm, d), lambda i: (i, 0)),
            pl.BlockSpec((tm, d), lambda i: (i, 0)),
            vec(),
            _const_spec(w_out.shape, 1),
        ],
        out_specs=pl.BlockSpec((tm, d), lambda i: (i, 0)),
        out_shape=jax.ShapeDtypeStruct((n_tok, d), F32),
        compiler_params=params,
        name="attn_sample_out",
    )(act, h2, gpost, w_out)
    return y.reshape(n_seq, t_new, d), ck, cv


def kernel(x_prompt, x_sample, state_conv, cache_k, cache_v, norm_pre, norm_post, w_in_a, conv_w, conv_b, ln_g, ln_b, w_out_a, kv_norm, w_kv, w_in_b, sinks, w_out_b):
    n_a = w_in_a.shape[0]
    assert n_a == 1 and w_in_b.shape[0] == 1 and norm_pre.shape[0] == 2
    d = x_prompt.shape[-1]
    n_seq, w_buf, n_kv, hd = cache_k.shape
    assert hd == HEAD_DIM and w_buf == WINDOW

    row = lambda v: v.reshape(1, -1)
    w_in_a_bf = w_in_a[0].astype(BF16)
    w_out_a_bf = w_out_a[0].astype(BF16)
    w_kv_bf = w_kv.astype(BF16)
    w_in_b_bf = w_in_b[0].astype(BF16)
    w_out_b_bf = w_out_b[0].astype(BF16)
    conv_args = (w_in_a_bf, conv_w[0], row(conv_b[0]), row(ln_g[0]), row(ln_b[0]), row(norm_pre[0]),
                 row(norm_post[0]), w_out_a_bf)
    attn_args = (sinks[0], row(kv_norm), row(norm_pre[1]), row(norm_post[1]), w_kv_bf, w_in_b_bf,
                 w_out_b_bf)

    h_p, st_p = _conv_layer_prompt(x_prompt, *conv_args, tm=512, cn=256)
    y_p, ck_p, cv_p = _attn_layer_prompt(h_p, *attn_args, tm=256)

    h_s, st_s = _conv_layer_sample(x_sample, state_conv[0], *conv_args, n_seq_tile=64, cn=256)
    y_s, ck_s, cv_s = _attn_layer_sample(
        h_s, cache_k, cache_v,
        *attn_args, tm=512, n_seq_tile=16)

    b = x_prompt.shape[0]
    return (y_p, y_s, st_p[None], ck_p.reshape(b, w_buf, n_kv, hd), cv_p.reshape(b, w_buf, n_kv, hd),
            st_s[None], ck_s, cv_s)
```

```python
import functools

import jax
import jax.numpy as jnp
from jax import lax
from jax.experimental import pallas as pl
from jax.experimental.pallas import tpu as pltpu

RMS_EPS = 1e-6
LN_EPS = 1e-5
HEAD_DIM = 64
GROUP = 8
WINDOW = 128
SUBLANES = 8
LANES = 128
LOG2E = 1.4426950408889634
HALO = 32
VMEM_LIMIT = 56 * 1024 * 1024
ROW_BLK = 64

BF16 = jnp.bfloat16
F32 = jnp.float32


def _rows(i, n):
    return pl.ds(pl.multiple_of(i * n, n), n)


def _rms_scale(x):
    return lax.rsqrt(jnp.mean(x * x, axis=-1, keepdims=True) + RMS_EPS)


def _dot(a, b):
    return jnp.dot(a, b, preferred_element_type=F32)


def _dot_t(a, b):
    return lax.dot_general(a, b, (((1,), (1,)), ((), ())), preferred_element_type=F32)


def _pre_norm_to_bf16(x_ref, g_ref, u_ref, tm):
    def body(r, c):
        rows = _rows(r, ROW_BLK)
        x = x_ref[rows, :]
        u_ref[rows, :] = (x * _rms_scale(x) * g_ref[...]).astype(BF16)
        return c
    lax.fori_loop(0, tm // ROW_BLK, body, 0, unroll=2)


def _glu_chunk(u_ref, wa_ref, wb_ref, wg_ref):
    u = u_ref[...]
    c = _dot(u, wa_ref[...]) * jax.nn.sigmoid(_dot(u, wb_ref[...]))
    sg = jax.nn.silu(_dot(u, wg_ref[...])).astype(BF16)
    return c, sg


def _broadcast_taps(cw_ref, wbc_ref, taps):
    for k in range(taps):
        wbc_ref[k] = jnp.broadcast_to(cw_ref[k:k + 1, :], wbc_ref.shape[1:])


def _conv_finalize(j_chunks, tm, cn, y_ref, sg_ref, act_ref, x_ref, h_ref, cb_ref, lng_ref, lnb_ref,
                   gpost_ref, wout_ref):
    d = j_chunks * cn

    ln_rows = ROW_BLK

    def ln_body(r, c):
        rows = _rows(r, ln_rows)
        ys = [y_ref[jj, rows, :] + cb_ref[:, jj * cn:(jj + 1) * cn] for jj in range(j_chunks)]
        mu = jnp.sum(sum(ys), axis=-1, keepdims=True) * (1.0 / d)
        yc = [y - mu for y in ys]
        var = jnp.sum(sum(y * y for y in yc), axis=-1, keepdims=True) * (1.0 / d)
        rstd = lax.rsqrt(var + LN_EPS)
        for jj in range(j_chunks):
            cols = slice(jj * cn, (jj + 1) * cn)
            t = jax.nn.silu(yc[jj] * rstd * lng_ref[:, cols] + lnb_ref[:, cols])
            act_ref[rows, cols] = (t * sg_ref[jj, rows, :].astype(F32)).astype(BF16)
        return c
    lax.fori_loop(0, tm // ln_rows, ln_body, 0, unroll=2)

    h_ref[...] = _dot(act_ref[...], wout_ref[...])

    def post_body(r, c):
        rows = [_rows(2 * r + e, ROW_BLK) for e in range(2)]
        os = [h_ref[rw, :] for rw in rows]
        res = [x_ref[rw, :] + o * _rms_scale(o) * gpost_ref[...] for rw, o in zip(rows, os)]
        for rw, v in zip(rows, res):
            h_ref[rw, :] = v
        return c
    lax.fori_loop(0, tm // (2 * ROW_BLK), post_body, 0)


def _conv_chunk(cext_ref, wbc_ref, y_ref, jc, *, tm, cn, taps, grp):
    lbs = cn // LANES
    assert lbs >= 2, "the row-interleaved layout needs at least two lane blocks per chunk"
    lead = HALO - (taps - 1)
    for g in range(tm // (grp * SUBLANES)):
        base = g * grp * SUBLANES
        for lb in range(lbs):
            ls = slice(lb * LANES, (lb + 1) * LANES)
            acc = [None] * grp
            for k in range(taps):
                wv = wbc_ref[jc, k, :, ls]
                for gi in range(grp):
                    r0 = base + SUBLANES * gi + k + lead
                    t = wv * cext_ref[jc, pl.ds(lbs * r0 + lb, SUBLANES, stride=lbs), :]
                    acc[gi] = t if acc[gi] is None else acc[gi] + t
            for gi in range(grp):
                y_ref[jc, base + SUBLANES * gi:base + SUBLANES * (gi + 1), ls] = acc[gi]


def _conv_prompt_kernel(x_ref, w_hbm, cw_ref, cb_ref, lng_ref, lnb_ref, gpre_ref, gpost_ref, wout_ref,
                        h_ref, st_ref,
                        u_ref, wbuf_ref, wsem, cext_ref, wbc_ref, y_ref, sg_ref, act_ref,
                        *, tm, cn, taps, grp):
    bb = pl.program_id(0)
    i = pl.program_id(1)
    n_chunks = cext_ref.shape[0]
    lbs = cn // LANES
    first_step = (bb == 0) & (i == 0)
    last_step = (bb == pl.num_programs(0) - 1) & (i == pl.num_programs(1) - 1)

    def weight_copy(jc, slot, part):
        col0 = pl.multiple_of((part * n_chunks + jc) * cn, cn)
        return pltpu.make_async_copy(w_hbm.at[:, pl.ds(col0, cn)], wbuf_ref.at[slot, part],
                                     wsem.at[slot, part])

    @pl.when(first_step)
    def _():
        for part in range(3):
            weight_copy(0, 0, part).start()
        for jj in range(n_chunks):
            for k in range(taps):
                wbc_ref[jj, k] = jnp.broadcast_to(cw_ref[k:k + 1, jj * cn:(jj + 1) * cn], (SUBLANES, cn))

    _pre_norm_to_bf16(x_ref, gpre_ref, u_ref, tm)

    def chunk_body(j, carry):
        slot = j % 2
        for part in range(3):
            weight_copy(j, slot, part).wait()

        @pl.when(jnp.logical_not(last_step & (j == n_chunks - 1)))
        def _():
            nxt = jnp.where(j == n_chunks - 1, 0, j + 1)
            for part in range(3):
                weight_copy(nxt, 1 - slot, part).start()

        @pl.when(i == 0)
        def _():
            cext_ref[j, 0:lbs * HALO, :] = jnp.zeros((lbs * HALO, LANES), F32)

        @pl.when(i > 0)
        def _():
            cext_ref[j, 0:lbs * HALO, :] = cext_ref[j, lbs * tm:lbs * (tm + HALO), :]

        u = u_ref[...]
        c = _dot(u, wbuf_ref[slot, 0]) * jax.nn.sigmoid(_dot(u, wbuf_ref[slot, 1]))
        sg_ref[j] = jax.nn.silu(_dot(u, wbuf_ref[slot, 2])).astype(BF16)
        for lb in range(lbs):
            cext_ref[j, pl.ds(lbs * HALO + lb, tm, stride=lbs), :] = c[:, lb * LANES:(lb + 1) * LANES]
        _conv_chunk(cext_ref, wbc_ref, y_ref, j, tm=tm, cn=cn, taps=taps, grp=grp)
        return carry
    lax.fori_loop(0, n_chunks, chunk_body, 0)

    _conv_finalize(n_chunks, tm, cn, y_ref, sg_ref, act_ref, x_ref, h_ref, cb_ref,
                   lng_ref, lnb_ref, gpost_ref, wout_ref)

    @pl.when(i == pl.num_programs(1) - 1)
    def _():
        tail = lbs * (HALO + tm - (taps - 1))
        for jj in range(n_chunks):
            for lb in range(lbs):
                c0 = jj * cn + lb * LANES
                st_ref[:, c0:c0 + LANES] = cext_ref[jj, pl.ds(tail + lb, taps - 1, stride=lbs), :]


def _conv_sample_kernel(x_ref, st_in_ref, wa_ref, wb_ref, wg_ref, cw_ref, cb_ref, lng_ref, lnb_ref,
                        gpre_ref, gpost_ref, wout_ref, h_ref, st_ref,
                        u_ref, c_ref, ystg_ref, wbc_ref, y_ref, sg_ref, act_ref, *, tm, cn, taps, t_new):
    j = pl.program_id(1)
    n_j = pl.num_programs(1)
    n_seq = tm // t_new
    hist = taps - 1

    @pl.when(j == 0)
    def _():
        _pre_norm_to_bf16(x_ref, gpre_ref, u_ref, tm)

    c, sg = _glu_chunk(u_ref, wa_ref, wb_ref, wg_ref)
    sg_ref[j] = sg
    _broadcast_taps(cw_ref, wbc_ref, taps)

    for lb in range(cn // LANES):
        ls = slice(lb * LANES, (lb + 1) * LANES)
        c_ref[lb] = c[:, ls]
        new_rows = [c_ref[lb, pl.ds(t, n_seq, stride=t_new), :] for t in range(t_new)]
        for r in range(hist):
            st_ref[r, :, ls] = st_in_ref[r + t_new, :, ls] if r + t_new < hist else new_rows[r + t_new - hist]
        for sb in range(n_seq // SUBLANES):
            seqs = slice(sb * SUBLANES, (sb + 1) * SUBLANES)
            full = [st_in_ref[r, seqs, ls] for r in range(hist)] + [nr[seqs] for nr in new_rows]
            acc = [None] * t_new
            for k in range(taps):
                wv = wbc_ref[k, :, ls]
                for t in range(t_new):
                    term = wv * full[t + k]
                    acc[t] = term if acc[t] is None else acc[t] + term
            for t in range(t_new):
                ystg_ref[lb, pl.ds(sb * SUBLANES * t_new + t, SUBLANES, stride=t_new), :] = acc[t]
        y_ref[j, :, ls] = ystg_ref[lb]

    @pl.when(j == n_j - 1)
    def _():
        _conv_finalize(y_ref.shape[0], tm, cn, y_ref, sg_ref, act_ref, x_ref, h_ref, cb_ref,
                       lng_ref, lnb_ref, gpost_ref, wout_ref)


def _const_spec(shape, n_grid):
    zeros = (0,) * len(shape)
    return pl.BlockSpec(shape, lambda *_: zeros, pipeline_mode=pl.Buffered(1))


def _conv_layer_prompt(x, w_in, cw, cb, lng, lnb, gpre, gpost, w_out, *, tm, cn):
    b, t, d = x.shape
    taps = cw.shape[0]
    n_j = d // cn
    lbs = cn // LANES
    grp = 4
    kern = functools.partial(_conv_prompt_kernel, tm=tm, cn=cn, taps=taps, grp=grp)
    vec = lambda: _const_spec((1, d), 2)
    return pl.pallas_call(
        kern,
        grid=(b, t // tm),
        in_specs=[
            pl.BlockSpec((None, tm, d), lambda bb, i: (bb, i, 0)),
            pl.BlockSpec(memory_space=pl.ANY),
            _const_spec((taps, d), 2),
            vec(), vec(), vec(), vec(), vec(),
            _const_spec((d, d), 2),
        ],
        out_specs=[
            pl.BlockSpec((None, tm, d), lambda bb, i: (bb, i, 0)),
            pl.BlockSpec((None, taps - 1, d), lambda bb, i: (bb, 0, 0)),
        ],
        out_shape=[
            jax.ShapeDtypeStruct((b, t, d), F32),
            jax.ShapeDtypeStruct((b, taps - 1, d), F32),
        ],
        scratch_shapes=[
            pltpu.VMEM((tm, d), BF16),
            pltpu.VMEM((2, 3, d, cn), BF16),
            pltpu.SemaphoreType.DMA((2, 3)),
            pltpu.VMEM((n_j, lbs * (HALO + tm), LANES), F32),
            pltpu.VMEM((n_j, taps, SUBLANES, cn), F32),
            pltpu.VMEM((n_j, tm, cn), F32),
            pltpu.VMEM((n_j, tm, cn), BF16),
            pltpu.VMEM((tm, d), BF16),
        ],
        compiler_params=pltpu.CompilerParams(
            dimension_semantics=("arbitrary", "arbitrary"),
            vmem_limit_bytes=VMEM_LIMIT),
        name="conv_layer_prompt",
    )(x, w_in, cw, cb, lng, lnb, gpre, gpost, w_out)


def _conv_layer_sample(x, state, w_in, cw, cb, lng, lnb, gpre, gpost, w_out, *, n_seq_tile, cn):
    n_seq, t_new, d = x.shape
    taps = cw.shape[0]
    n_j = d // cn
    tm = n_seq_tile * t_new
    x2 = x.reshape(n_seq * t_new, d)
    kern = functools.partial(_conv_sample_kernel, tm=tm, cn=cn, taps=taps, t_new=t_new)
    vec = lambda: _const_spec((1, d), 2)
    h, st = pl.pallas_call(
        kern,
        grid=(n_seq // n_seq_tile, n_j),
        in_specs=[
            pl.BlockSpec((tm, d), lambda i, j: (i, 0)),
            pl.BlockSpec((taps - 1, n_seq_tile, cn), lambda i, j: (0, i, j)),
            pl.BlockSpec((d, cn), lambda i, j: (0, j)),
            pl.BlockSpec((d, cn), lambda i, j: (0, n_j + j)),
            pl.BlockSpec((d, cn), lambda i, j: (0, 2 * n_j + j)),
            pl.BlockSpec((taps, cn), lambda i, j: (0, j)),
            vec(), vec(), vec(), vec(), vec(),
            _const_spec((d, d), 2),
        ],
        out_specs=[
            pl.BlockSpec((tm, d), lambda i, j: (i, 0)),
            pl.BlockSpec((taps - 1, n_seq_tile, cn), lambda i, j: (0, i, j)),
        ],
        out_shape=[
            jax.ShapeDtypeStruct((n_seq * t_new, d), F32),
            jax.ShapeDtypeStruct((taps - 1, n_seq, d), F32),
        ],
        scratch_shapes=[
            pltpu.VMEM((tm, d), BF16),
            pltpu.VMEM((cn // LANES, tm, LANES), F32),
            pltpu.VMEM((cn // LANES, tm, LANES), F32),
            pltpu.VMEM((taps, SUBLANES, cn), F32),
            pltpu.VMEM((n_j, tm, cn), F32),
            pltpu.VMEM((n_j, tm, cn), BF16),
            pltpu.VMEM((tm, d), BF16),
        ],
        compiler_params=pltpu.CompilerParams(
            dimension_semantics=("arbitrary", "arbitrary"),
            vmem_limit_bytes=VMEM_LIMIT),
        name="conv_layer_sample",
    )(x2, state.transpose(1, 0, 2), w_in, w_in, w_in, cw, cb, lng, lnb, gpre, gpost, w_out)
    return h.reshape(n_seq, t_new, d), st.transpose(1, 0, 2)


def _attn_projections(h_ref, gkv_ref, gpre_ref, wkv_ref, win_ref, ukv_ref, u_ref, q_ref, sg_ref, tm,
                      n_chunk):
    d = h_ref.shape[-1]

    def body(r, c):
        rows = _rows(r, ROW_BLK)
        x = h_ref[rows, :]
        xn = x * _rms_scale(x)
        ukv_ref[rows, :] = (xn * gkv_ref[...]).astype(BF16)
        u_ref[rows, :] = (xn * gpre_ref[...]).astype(BF16)
        return c
    lax.fori_loop(0, tm // ROW_BLK, body, 0, unroll=2)

    kv = _dot(ukv_ref[...], wkv_ref[...])
    u = u_ref[...]
    cw = d // n_chunk
    scale = HEAD_DIM ** -0.5 * LOG2E
    for n in range(n_chunk):
        cols = slice(n * cw, (n + 1) * cw)
        q_ref[:, cols] = (_dot(u, win_ref[:, cols]) * scale).astype(BF16)
    for n in range(n_chunk):
        cols = slice(n * cw, (n + 1) * cw)
        sg_ref[:, cols] = jax.nn.silu(_dot(u, win_ref[:, d + n * cw:d + (n + 1) * cw])).astype(BF16)
    return kv


def _attn_output(act_ref, h_ref, gpost_ref, wout_ref, y_ref, tm):
    y_ref[...] = _dot(act_ref[...], wout_ref[...])

    def post_body(r, c):
        rows = [_rows(2 * r + e, ROW_BLK) for e in range(2)]
        os = [y_ref[rw, :] for rw in rows]
        res = [h_ref[rw, :] + o * _rms_scale(o) * gpost_ref[...] for rw, o in zip(rows, os)]
        for rw, v in zip(rows, res):
            y_ref[rw, :] = v
        return c
    lax.fori_loop(0, tm // (2 * ROW_BLK), post_body, 0)


def _attn_block(sink_ref, q_ref, sg_ref, kpad_ref, vpad_ref, act_ref, r0, *, n_kv, first):
    blk = WINDOW
    pair_w = 2 * HEAD_DIM
    n_pair = GROUP // 2
    qi = lax.broadcasted_iota(jnp.int32, (blk, blk), 0)
    kj = lax.broadcasted_iota(jnp.int32, (blk, blk), 1)
    own = kj <= qi
    low = lax.broadcasted_iota(jnp.int32, (blk, pair_w), 1) < HEAD_DIM
    rows = pl.ds(r0, blk)
    krows = pl.ds(r0, 2 * blk)
    for g in range(n_kv):
        qs = jnp.concatenate(
            [q_ref[rows, (g * n_pair + pp) * pair_w:(g * n_pair + pp + 1) * pair_w]
             for pp in range(n_pair)], axis=0)
        probs, stats = [], []
        for e in range(2):
            s = _dot_t(qs, kpad_ref[krows, (2 * g + e) * pair_w:(2 * g + e + 1) * pair_w])
            p_parts, st = [], []
            for pp in range(n_pair):
                sh = s[pp * blk:(pp + 1) * blk]
                prev = sh[:, :blk] if first is False else sh[:, :blk] + jnp.where(first, -jnp.inf, 0.0)
                sf = jnp.where(own, sh[:, blk:], prev)
                sink = sink_ref[g * GROUP + 2 * pp + e] * LOG2E
                m = jnp.maximum(jnp.max(sf, axis=-1, keepdims=True), sink)
                p = jnp.exp2(sf - m)
                p_parts.append(jnp.concatenate([jnp.where(own, 0.0, p), jnp.where(own, p, 0.0)],
                                               axis=1).astype(BF16))
                st.append(jnp.exp2(sink - m))
            probs.append(jnp.concatenate(p_parts, axis=0))
            stats.append(st)
        o = (_dot(probs[0], vpad_ref[krows, (2 * g) * 2 * pair_w:(2 * g + 1) * 2 * pair_w])
             + _dot(probs[1], vpad_ref[krows, (2 * g + 1) * 2 * pair_w:(2 * g + 2) * 2 * pair_w]))
        for pp in range(n_pair):
            oh = o[pp * blk:(pp + 1) * blk]
            den = oh[:, pair_w:] + jnp.where(low, stats[0][pp], stats[1][pp])
            c0 = (g * n_pair + pp) * pair_w
            act_ref[rows, c0:c0 + pair_w] = (
                oh[:, :pair_w] / den * sg_ref[rows, c0:c0 + pair_w].astype(F32)).astype(BF16)


def _store_padded_kv(kv, kpad_ref, vpad_ref, row0, n_kv):
    n = kv.shape[0]
    kvw = n_kv * HEAD_DIM
    pair_w = 2 * HEAD_DIM
    low = lax.broadcasted_iota(jnp.int32, (n, pair_w), 1) < HEAD_DIM
    halves = (low, jnp.logical_not(low))
    for gam in range(n_kv // 2):
        kc = kv[:, gam * pair_w:(gam + 1) * pair_w]
        vc = kv[:, kvw + gam * pair_w:kvw + (gam + 1) * pair_w]
        for side in range(2):
            g = 2 * gam + side
            k_here = jnp.where(halves[side], kc, 0.0)
            v_here = jnp.where(halves[side], vc, 0.0)
            k_other = pltpu.roll(k_here, HEAD_DIM, axis=1)
            v_other = pltpu.roll(v_here, HEAD_DIM, axis=1)
            for e in range(2):
                kc0 = (2 * g + e) * pair_w
                vc0 = (2 * g + e) * 2 * pair_w
                kpad_ref[row0:row0 + n, kc0:kc0 + pair_w] = (k_here if e == side else k_other).astype(BF16)
                vpad_ref[row0:row0 + n, vc0:vc0 + pair_w] = (v_here if e == side else v_other).astype(BF16)
                vpad_ref[row0:row0 + n, vc0 + pair_w:vc0 + 2 * pair_w] = (
                    jnp.where(halves[e], 1.0, 0.0).astype(BF16))


def _attn_prompt_kernel(sink_ref, h_ref, gkv_ref, gpre_ref, gpost_ref, wkv_ref, win_ref, wout_ref,
                        y_ref, ck_ref, cv_ref,
                        ukv_ref, u_ref, q_ref, sg_ref, kpad_ref, vpad_ref, act_ref, *, tm, n_kv):
    i = pl.program_id(1)
    n_i = pl.num_programs(1)
    kvw = n_kv * HEAD_DIM
    blk = WINDOW

    kv = _attn_projections(h_ref, gkv_ref, gpre_ref, wkv_ref, win_ref, ukv_ref, u_ref, q_ref, sg_ref,
                           tm, 4)

    @pl.when(i == 0)
    def _():
        kpad_ref[0:blk, :] = jnp.zeros((blk, kpad_ref.shape[1]), BF16)
        vpad_ref[0:blk, :] = jnp.zeros((blk, vpad_ref.shape[1]), BF16)

    @pl.when(i > 0)
    def _():
        kpad_ref[0:blk, :] = kpad_ref[tm:tm + blk, :]
        vpad_ref[0:blk, :] = vpad_ref[tm:tm + blk, :]

    _store_padded_kv(kv, kpad_ref, vpad_ref, blk, n_kv)

    @pl.when(i == n_i - 1)
    def _():
        ck_ref[...] = kv[tm - blk:, :kvw]
        cv_ref[...] = kv[tm - blk:, kvw:]

    block = functools.partial(_attn_block, sink_ref, q_ref, sg_ref, kpad_ref, vpad_ref, act_ref, n_kv=n_kv)

    for bi in range(tm // blk):
        block(bi * blk, first=(i == 0) if bi == 0 else False)

    _attn_output(act_ref, h_ref, gpost_ref, wout_ref, y_ref, tm)


def _attn_layer_prompt(h, sinks, gkv, gpre, gpost, w_kv, w_in, w_out, *, tm):
    b, t, d = h.shape
    kvw = w_kv.shape[1] // 2
    n_kv = kvw // HEAD_DIM
    kern = functools.partial(_attn_prompt_kernel, tm=tm, n_kv=n_kv)
    vec = lambda: _const_spec((1, d), 2)
    return pl.pallas_call(
        kern,
        grid=(b, t // tm),
        in_specs=[
            pl.BlockSpec(memory_space=pltpu.SMEM),
            pl.BlockSpec((None, tm, d), lambda bb, i: (bb, i, 0)),
            vec(), vec(), vec(),
            _const_spec(w_kv.shape, 2),
            _const_spec(w_in.shape, 2),
            _const_spec(w_out.shape, 2),
        ],
        out_specs=[
            pl.BlockSpec((None, tm, d), lambda bb, i: (bb, i, 0)),
            pl.BlockSpec((None, WINDOW, kvw), lambda bb, i: (bb, 0, 0)),
            pl.BlockSpec((None, WINDOW, kvw), lambda bb, i: (bb, 0, 0)),
        ],
        out_shape=[
            jax.ShapeDtypeStruct((b, t, d), F32),
            jax.ShapeDtypeStruct((b, WINDOW, kvw), F32),
            jax.ShapeDtypeStruct((b, WINDOW, kvw), F32),
        ],
        scratch_shapes=[
            pltpu.VMEM((tm, d), BF16),
            pltpu.VMEM((tm, d), BF16),
            pltpu.VMEM((tm, d), BF16),
            pltpu.VMEM((tm, d), BF16),
            pltpu.VMEM((WINDOW + tm, 4 * kvw), BF16),
            pltpu.VMEM((WINDOW + tm, 8 * kvw), BF16),
            pltpu.VMEM((tm, d), BF16),
        ],
        compiler_params=pltpu.CompilerParams(
            dimension_semantics=("arbitrary", "arbitrary"),
            vmem_limit_bytes=VMEM_LIMIT),
        name="attn_layer_prompt",
    )(sinks, h, gkv, gpre, gpost, w_kv, w_in, w_out)


def _attn_sample_proj_kernel(h_ref, gkv_ref, gpre_ref, wkv_ref, win_ref, kv_ref, q_ref, sg_ref,
                             ukv_ref, u_ref, *, tm):
    kv_ref[...] = _attn_projections(h_ref, gkv_ref, gpre_ref, wkv_ref, win_ref, ukv_ref, u_ref, q_ref,
                                    sg_ref, tm, 4)


def _attn_sample_core_kernel(sink_ref, q_ref, sg_ref, kvt_ref, ckt_in_ref, cvt_in_ref,
                             act_ref, ckt_ref, cvt_ref, *, n_seq, t_new, n_kv, n_par):
    kvw = n_kv * HEAD_DIM
    w_buf = ckt_in_ref.shape[-1]
    first_new = w_buf - t_new
    assert kvt_ref.shape[-1] == w_buf, "the new K/V rows of one grid step must fill one lane block"
    n_rows = GROUP * n_kv * t_new
    tq = lax.broadcasted_iota(jnp.int32, (n_rows, w_buf), 0) % t_new
    kj = lax.broadcasted_iota(jnp.int32, (n_rows, w_buf), 1)
    mask_c = kj > tq
    mask_n = (kj >= first_new) & (kj - first_new <= tq)
    is_new = lax.broadcasted_iota(jnp.int32, (kvw, w_buf), 1) >= first_new
    row_head = lax.broadcasted_iota(jnp.int32, (n_rows, 1), 0) // t_new
    sinkv = jnp.zeros((n_rows, 1), F32)
    for h in range(GROUP * n_kv):
        sinkv = jnp.where(row_head == h, sink_ref[h] * LOG2E, sinkv)
    pair_w = 2 * HEAD_DIM
    low = lax.broadcasted_iota(jnp.int32, (t_new, pair_w), 1) < HEAD_DIM
    halves = (low, jnp.logical_not(low))
    zero_blk = jnp.zeros((t_new, pair_w), F32)

    def to_half(block, src_half, dst_half):
        moved = block if src_half == dst_half else pltpu.roll(block, HEAD_DIM, axis=1)
        return jnp.where(halves[dst_half], moved, 0.0)

    def one_seq(s):
        rows = pl.ds(pl.multiple_of(s * t_new, t_new), t_new)
        kct = ckt_in_ref[s].reshape(kvw, w_buf)
        vct = cvt_in_ref[s].reshape(kvw, w_buf)
        newt = pltpu.roll(kvt_ref[...], first_new - s * t_new, axis=1)
        knt = jnp.where(is_new, newt[:kvw], 0.0)
        vnt = jnp.where(is_new, newt[kvw:], 0.0)
        ckt_ref[s] = jnp.where(is_new, knt, pltpu.roll(kct, first_new, axis=1)).reshape(ckt_ref.shape[1:])
        cvt_ref[s] = jnp.where(is_new, vnt, pltpu.roll(vct, first_new, axis=1)).reshape(cvt_ref.shape[1:])
        qparts = []
        for cb in range(GROUP * n_kv // 2):
            blk = q_ref[rows, cb * pair_w:(cb + 1) * pair_w].astype(F32)
            g = 2 * cb // GROUP
            for e in range(2):
                piece = to_half(blk, e, g % 2)
                qparts.append(jnp.concatenate(
                    [piece if kb == g // 2 else zero_blk for kb in range(kvw // pair_w)], axis=1))
        qbd = jnp.concatenate(qparts, axis=0).astype(BF16)
        s_c = jnp.where(mask_c, _dot(qbd, kct.astype(BF16)), -jnp.inf)
        s_n = jnp.where(mask_n, _dot(qbd, knt.astype(BF16)), -jnp.inf)
        m = jnp.maximum(jnp.max(jnp.maximum(s_c, s_n), axis=-1, keepdims=True), sinkv)
        p_c = jnp.exp2(s_c - m)
        p_n = jnp.exp2(s_n - m)
        den = jnp.sum(p_c + p_n, axis=-1, keepdims=True) + jnp.exp2(sinkv - m)
        o = (_dot_t(p_c.astype(BF16), vct.astype(BF16))
             + _dot_t(p_n.astype(BF16), vnt.astype(BF16))) * (1.0 / den)
        for cb in range(GROUP * n_kv // 2):
            g = 2 * cb // GROUP
            kcols = slice((g // 2) * pair_w, (g // 2 + 1) * pair_w)
            pieces = [to_half(o[(2 * cb + e) * t_new:(2 * cb + e + 1) * t_new, kcols], g % 2, e)
                      for e in range(2)]
            cols = slice(cb * pair_w, (cb + 1) * pair_w)
            act_ref[rows, cols] = ((pieces[0] + pieces[1]) * sg_ref[rows, cols].astype(F32)).astype(BF16)

    def seq_body(sb, carry):
        for q in range(n_par):
            one_seq(sb * n_par + q)
        return carry
    lax.fori_loop(0, n_seq // n_par, seq_body, 0)


def _attn_sample_out_kernel(act_ref, h_ref, gpost_ref, wout_ref, y_ref, *, tm):
    _attn_output(act_ref, h_ref, gpost_ref, wout_ref, y_ref, tm)


def _attn_layer_sample(h, cache_k, cache_v, sinks, gkv, gpre, gpost, w_kv, w_in, w_out, *, tm,
                       n_seq_tile):
    n_seq, t_new, d = h.shape
    kvw = w_kv.shape[1] // 2
    n_kv = kvw // HEAD_DIM
    w_buf = cache_k.shape[1]
    n_tok = n_seq * t_new
    h2 = h.reshape(n_tok, d)
    vec = lambda: _const_spec((1, d), 1)
    params = pltpu.CompilerParams(dimension_semantics=("arbitrary",), vmem_limit_bytes=VMEM_LIMIT)

    kv, q, sg = pl.pallas_call(
        functools.partial(_attn_sample_proj_kernel, tm=tm),
        grid=(n_tok // tm,),
        in_specs=[
            pl.BlockSpec((tm, d), lambda i: (i, 0)),
            vec(), vec(),
            _const_spec(w_kv.shape, 1),
            _const_spec(w_in.shape, 1),
        ],
        out_specs=[
            pl.BlockSpec((tm, 2 * kvw), lambda i: (i, 0)),
            pl.BlockSpec((tm, d), lambda i: (i, 0)),
            pl.BlockSpec((tm, d), lambda i: (i, 0)),
        ],
        out_shape=[
            jax.ShapeDtypeStruct((n_tok, 2 * kvw), F32),
            jax.ShapeDtypeStruct((n_tok, d), BF16),
            jax.ShapeDtypeStruct((n_tok, d), BF16),
        ],
        scratch_shapes=[pltpu.VMEM((tm, d), BF16), pltpu.VMEM((tm, d), BF16)],
        compiler_params=params,
        name="attn_sample_proj",
    )(h2, gkv, gpre, w_kv, w_in)

    ckt_in = cache_k.transpose(0, 2, 3, 1)
    cvt_in = cache_v.transpose(0, 2, 3, 1)
    rows = n_seq_tile * t_new
    cache_blk = pl.BlockSpec((n_seq_tile, n_kv, HEAD_DIM, w_buf), lambda i: (i, 0, 0, 0))
    act, ckt, cvt = pl.pallas_call(
        functools.partial(_attn_sample_core_kernel, n_seq=n_seq_tile, t_new=t_new, n_kv=n_kv, n_par=4),
        grid=(n_seq // n_seq_tile,),
        in_specs=[
            pl.BlockSpec(memory_space=pltpu.SMEM),
            pl.BlockSpec((rows, d), lambda i: (i, 0)),
            pl.BlockSpec((rows, d), lambda i: (i, 0)),
            pl.BlockSpec((2 * kvw, rows), lambda i: (0, i)),
            cache_blk, cache_blk,
        ],
        out_specs=[pl.BlockSpec((rows, d), lambda i: (i, 0)), cache_blk, cache_blk],
        out_shape=[
            jax.ShapeDtypeStruct((n_tok, d), BF16),
            jax.ShapeDtypeStruct(ckt_in.shape, F32),
            jax.ShapeDtypeStruct(cvt_in.shape, F32),
        ],
        compiler_params=params,
        name="attn_sample_core",
    )(sinks, q, sg, kv.T, ckt_in, cvt_in)
    ck = ckt.transpose(0, 3, 1, 2)
    cv = cvt.transpose(0, 3, 1, 2)

    y = pl.pallas_call(
        functools.partial(_attn_sample_out_kernel, tm=tm),
        grid=(n_tok // tm,),
        in_specs=[
            pl.BlockSpec((tm, d), lambda i: (i, 0)),
            pl.BlockSpec((tm, d), lambda i: (i, 0)),
            vec(),
            _const_spec(w_out.shape, 1),
        ],
        out_specs=pl.BlockSpec((tm, d), lambda i: (i, 0)),
        out_shape=jax.ShapeDtypeStruct((n_tok, d), F32),
        compiler_params=params,
        name="attn_sample_out",
    )(act, h2, gpost, w_out)
    return y.reshape(n_seq, t_new, d), ck, cv


def kernel(x_prompt, x_sample, state_conv, cache_k, cache_v, norm_pre, norm_post, w_in_a, conv_w, conv_b, ln_g, ln_b, w_out_a, kv_norm, w_kv, w_in_b, sinks, w_out_b):
    n_a = w_in_a.shape[0]
    assert n_a == 1 and w_in_b.shape[0] == 1 and norm_pre.shape[0] == 2
    d = x_prompt.shape[-1]
    n_seq, w_buf, n_kv, hd = cache_k.shape
    assert hd == HEAD_DIM and w_buf == WINDOW

    row = lambda v: v.reshape(1, -1)
    w_in_a_bf = w_in_a[0].astype(BF16)
    w_out_a_bf = w_out_a[0].astype(BF16)
    w_kv_bf = w_kv.astype(BF16)
    w_in_b_bf = w_in_b[0].astype(BF16)
    w_out_b_bf = w_out_b[0].astype(BF16)
    conv_args = (w_in_a_bf, conv_w[0], row(conv_b[0]), row(ln_g[0]), row(ln_b[0]), row(norm_pre[0]),
                 row(norm_post[0]), w_out_a_bf)
    attn_args = (sinks[0], row(kv_norm), row(norm_pre[1]), row(norm_post[1]), w_kv_bf, w_in_b_bf,
                 w_out_b_bf)

    h_p, st_p = _conv_layer_prompt(x_prompt, *conv_args, tm=512, cn=256)
    y_p, ck_p, cv_p = _attn_layer_prompt(h_p, *attn_args, tm=256)

    h_s, st_s = _conv_layer_sample(x_sample, state_conv[0], *conv_args, n_seq_tile=64, cn=256)
    y_s, ck_s, cv_s = _attn_layer_sample(
        h_s, cache_k, cache_v,
        *attn_args, tm=512, n_seq_tile=16)

    b = x_prompt.shape[0]
    return (y_p, y_s, st_p[None], ck_p.reshape(b, w_buf, n_kv, hd), cv_p.reshape(b, w_buf, n_kv, hd),
            st_s[None], ck_s, cv_s)
```

```python
import functools

import jax
import jax.numpy as jnp
from jax import lax
from jax.experimental import pallas as pl
from jax.experimental.pallas import tpu as pltpu

RMS_EPS = 1e-6
LN_EPS = 1e-5
HEAD_DIM = 64
GROUP = 8
WINDOW = 128
SUBLANES = 8
LANES = 128
LOG2E = 1.4426950408889634
HALO = 32
VMEM_LIMIT = 56 * 1024 * 1024
ROW_BLK = 64

BF16 = jnp.bfloat16
F32 = jnp.float32


def _rows(i, n):
    return pl.ds(pl.multiple_of(i * n, n), n)


def _rms_scale(x):
    return lax.rsqrt(jnp.mean(x * x, axis=-1, keepdims=True) + RMS_EPS)


def _dot(a, b):
    return jnp.dot(a, b, preferred_element_type=F32)


def _dot_t(a, b):
    return lax.dot_general(a, b, (((1,), (1,)), ((), ())), preferred_element_type=F32)


def _pre_norm_to_bf16(x_ref, g_ref, u_ref, tm):
    def body(r, c):
        rows = _rows(r, ROW_BLK)
        x = x_ref[rows, :]
        u_ref[rows, :] = (x * _rms_scale(x) * g_ref[...]).astype(BF16)
        return c
    lax.fori_loop(0, tm // ROW_BLK, body, 0, unroll=2)


def _glu_chunk(u_ref, wa_ref, wb_ref, wg_ref):
    u = u_ref[...]
    c = _dot(u, wa_ref[...]) * jax.nn.sigmoid(_dot(u, wb_ref[...]))
    sg = jax.nn.silu(_dot(u, wg_ref[...])).astype(BF16)
    return c, sg


def _broadcast_taps(cw_ref, wbc_ref, taps):
    for k in range(taps):
        wbc_ref[k] = jnp.broadcast_to(cw_ref[k:k + 1, :], wbc_ref.shape[1:])


def _conv_finalize(j_chunks, tm, cn, y_ref, sg_ref, act_ref, x_ref, h_ref, cb_ref, lng_ref, lnb_ref,
                   gpost_ref, wout_ref):
    d = j_chunks * cn

    ln_rows = ROW_BLK

    def ln_body(r, c):
        rows = _rows(r, ln_rows)
        ys = [y_ref[jj, rows, :] + cb_ref[:, jj * cn:(jj + 1) * cn] for jj in range(j_chunks)]
        mu = jnp.sum(sum(ys), axis=-1, keepdims=True) * (1.0 / d)
        yc = [y - mu for y in ys]
        var = jnp.sum(sum(y * y for y in yc), axis=-1, keepdims=True) * (1.0 / d)
        rstd = lax.rsqrt(var + LN_EPS)
        for jj in range(j_chunks):
            cols = slice(jj * cn, (jj + 1) * cn)
            t = jax.nn.silu(yc[jj] * rstd * lng_ref[:, cols] + lnb_ref[:, cols])
            act_ref[rows, cols] = (t * sg_ref[jj, rows, :].astype(F32)).astype(BF16)
        return c
    lax.fori_loop(0, tm // ln_rows, ln_body, 0, unroll=2)

    h_ref[...] = _dot(act_ref[...], wout_ref[...])

    def post_body(r, c):
        rows = [_rows(2 * r + e, ROW_BLK) for e in range(2)]
        os = [h_ref[rw, :] for rw in rows]
        res = [x_ref[rw, :] + o * _rms_scale(o) * gpost_ref[...] for rw, o in zip(rows, os)]
        for rw, v in zip(rows, res):
            h_ref[rw, :] = v
        return c
    lax.fori_loop(0, tm // (2 * ROW_BLK), post_body, 0)


def _conv_chunk(cext_ref, wbc_ref, y_ref, jc, *, tm, cn, taps, grp):
    lbs = cn // LANES
    assert lbs >= 2, "the row-interleaved layout needs at least two lane blocks per chunk"
    lead = HALO - (taps - 1)
    for g in range(tm // (grp * SUBLANES)):
        base = g * grp * SUBLANES
        for lb in range(lbs):
            ls = slice(lb * LANES, (lb + 1) * LANES)
            acc = [None] * grp
            for k in range(taps):
                wv = wbc_ref[jc, k, :, ls]
                for gi in range(grp):
                    r0 = base + SUBLANES * gi + k + lead
                    t = wv * cext_ref[jc, pl.ds(lbs * r0 + lb, SUBLANES, stride=lbs), :]
                    acc[gi] = t if acc[gi] is None else acc[gi] + t
            for gi in range(grp):
                y_ref[jc, base + SUBLANES * gi:base + SUBLANES * (gi + 1), ls] = acc[gi]


def _conv_prompt_kernel(x_ref, w_hbm, cw_ref, cb_ref, lng_ref, lnb_ref, gpre_ref, gpost_ref, wout_ref,
                        h_ref, st_ref,
                        u_ref, wbuf_ref, wsem, cext_ref, wbc_ref, y_ref, sg_ref, act_ref,
                        *, tm, cn, taps, grp):
    bb = pl.program_id(0)
    i = pl.program_id(1)
    n_chunks = cext_ref.shape[0]
    lbs = cn // LANES
    first_step = (bb == 0) & (i == 0)
    last_step = (bb == pl.num_programs(0) - 1) & (i == pl.num_programs(1) - 1)

    def weight_copy(jc, slot, part):
        col0 = pl.multiple_of((part * n_chunks + jc) * cn, cn)
        return pltpu.make_async_copy(w_hbm.at[:, pl.ds(col0, cn)], wbuf_ref.at[slot, part],
                                     wsem.at[slot, part])

    @pl.when(first_step)
    def _():
        for part in range(3):
            weight_copy(0, 0, part).start()
        for jj in range(n_chunks):
            for k in range(taps):
                wbc_ref[jj, k] = jnp.broadcast_to(cw_ref[k:k + 1, jj * cn:(jj + 1) * cn], (SUBLANES, cn))

    _pre_norm_to_bf16(x_ref, gpre_ref, u_ref, tm)

    def chunk_body(j, carry):
        slot = j % 2
        for part in range(3):
            weight_copy(j, slot, part).wait()

        @pl.when(jnp.logical_not(last_step & (j == n_chunks - 1)))
        def _():
            nxt = jnp.where(j == n_chunks - 1, 0, j + 1)
            for part in range(3):
                weight_copy(nxt, 1 - slot, part).start()

        @pl.when(i == 0)
        def _():
            cext_ref[j, 0:lbs * HALO, :] = jnp.zeros((lbs * HALO, LANES), F32)

        @pl.when(i > 0)
        def _():
            cext_ref[j, 0:lbs * HALO, :] = cext_ref[j, lbs * tm:lbs * (tm + HALO), :]

        u = u_ref[...]
        c = _dot(u, wbuf_ref[slot, 0]) * jax.nn.sigmoid(_dot(u, wbuf_ref[slot, 1]))
        sg_ref[j] = jax.nn.silu(_dot(u, wbuf_ref[slot, 2])).astype(BF16)
        for lb in range(lbs):
            cext_ref[j, pl.ds(lbs * HALO + lb, tm, stride=lbs), :] = c[:, lb * LANES:(lb + 1) * LANES]
        _conv_chunk(cext_ref, wbc_ref, y_ref, j, tm=tm, cn=cn, taps=taps, grp=grp)
        return carry
    lax.fori_loop(0, n_chunks, chunk_body, 0)

    _conv_finalize(n_chunks, tm, cn, y_ref, sg_ref, act_ref, x_ref, h_ref, cb_ref,
                   lng_ref, lnb_ref, gpost_ref, wout_ref)

    @pl.when(i == pl.num_programs(1) - 1)
    def _():
        tail = lbs * (HALO + tm - (taps - 1))
        for jj in range(n_chunks):
            for lb in range(lbs):
                c0 = jj * cn + lb * LANES
                st_ref[:, c0:c0 + LANES] = cext_ref[jj, pl.ds(tail + lb, taps - 1, stride=lbs), :]


def _conv_sample_kernel(x_ref, st_in_ref, wa_ref, wb_ref, wg_ref, cw_ref, cb_ref, lng_ref, lnb_ref,
                        gpre_ref, gpost_ref, wout_ref, h_ref, st_ref,
                        u_ref, c_ref, ystg_ref, wbc_ref, y_ref, sg_ref, act_ref, *, tm, cn, taps, t_new):
    j = pl.program_id(1)
    n_j = pl.num_programs(1)
    n_seq = tm // t_new
    hist = taps - 1

    @pl.when(j == 0)
    def _():
        _pre_norm_to_bf16(x_ref, gpre_ref, u_ref, tm)

    c, sg = _glu_chunk(u_ref, wa_ref, wb_ref, wg_ref)
    sg_ref[j] = sg
    _broadcast_taps(cw_ref, wbc_ref, taps)

    for lb in range(cn // LANES):
        ls = slice(lb * LANES, (lb + 1) * LANES)
        c_ref[lb] = c[:, ls]
        new_rows = [c_ref[lb, pl.ds(t, n_seq, stride=t_new), :] for t in range(t_new)]
        for r in range(hist):
            st_ref[r, :, ls] = st_in_ref[r + t_new, :, ls] if r + t_new < hist else new_rows[r + t_new - hist]
        for sb in range(n_seq // SUBLANES):
            seqs = slice(sb * SUBLANES, (sb + 1) * SUBLANES)
            full = [st_in_ref[r, seqs, ls] for r in range(hist)] + [nr[seqs] for nr in new_rows]
            acc = [None] * t_new
            for k in range(taps):
                wv = wbc_ref[k, :, ls]
                for t in range(t_new):
                    term = wv * full[t + k]
                    acc[t] = term if acc[t] is None else acc[t] + term
            for t in range(t_new):
                ystg_ref[lb, pl.ds(sb * SUBLANES * t_new + t, SUBLANES, stride=t_new), :] = acc[t]
        y_ref[j, :, ls] = ystg_ref[lb]

    @pl.when(j == n_j - 1)
    def _():
        _conv_finalize(y_ref.shape[0], tm, cn, y_ref, sg_ref, act_ref, x_ref, h_ref, cb_ref,
                       lng_ref, lnb_ref, gpost_ref, wout_ref)


def _const_spec(shape, n_grid):
    zeros = (0,) * len(shape)
    return pl.BlockSpec(shape, lambda *_: zeros, pipeline_mode=pl.Buffered(1))


def _conv_layer_prompt(x, w_in, cw, cb, lng, lnb, gpre, gpost, w_out, *, tm, cn):
    b, t, d = x.shape
    taps = cw.shape[0]
    n_j = d // cn
    lbs = cn // LANES
    grp = 4
    kern = functools.partial(_conv_prompt_kernel, tm=tm, cn=cn, taps=taps, grp=grp)
    vec = lambda: _const_spec((1, d), 2)
    return pl.pallas_call(
        kern,
        grid=(b, t // tm),
        in_specs=[
            pl.BlockSpec((None, tm, d), lambda bb, i: (bb, i, 0)),
            pl.BlockSpec(memory_space=pl.ANY),
            _const_spec((taps, d), 2),
            vec(), vec(), vec(), vec(), vec(),
            _const_spec((d, d), 2),
        ],
        out_specs=[
            pl.BlockSpec((None, tm, d), lambda bb, i: (bb, i, 0)),
            pl.BlockSpec((None, taps - 1, d), lambda bb, i: (bb, 0, 0)),
        ],
        out_shape=[
            jax.ShapeDtypeStruct((b, t, d), F32),
            jax.ShapeDtypeStruct((b, taps - 1, d), F32),
        ],
        scratch_shapes=[
            pltpu.VMEM((tm, d), BF16),
            pltpu.VMEM((2, 3, d, cn), BF16),
            pltpu.SemaphoreType.DMA((2, 3)),
            pltpu.VMEM((n_j, lbs * (HALO + tm), LANES), F32),
            pltpu.VMEM((n_j, taps, SUBLANES, cn), F32),
            pltpu.VMEM((n_j, tm, cn), F32),
            pltpu.VMEM((n_j, tm, cn), BF16),
            pltpu.VMEM((tm, d), BF16),
        ],
        compiler_params=pltpu.CompilerParams(
            dimension_semantics=("arbitrary", "arbitrary"),
            vmem_limit_bytes=VMEM_LIMIT),
        name="conv_layer_prompt",
    )(x, w_in, cw, cb, lng, lnb, gpre, gpost, w_out)


def _conv_layer_sample(x, state, w_in, cw, cb, lng, lnb, gpre, gpost, w_out, *, n_seq_tile, cn):
    n_seq, t_new, d = x.shape
    taps = cw.shape[0]
    n_j = d // cn
    tm = n_seq_tile * t_new
    x2 = x.reshape(n_seq * t_new, d)
    kern = functools.partial(_conv_sample_kernel, tm=tm, cn=cn, taps=taps, t_new=t_new)
    vec = lambda: _const_spec((1, d), 2)
    h, st = pl.pallas_call(
        kern,
        grid=(n_seq // n_seq_tile, n_j),
        in_specs=[
            pl.BlockSpec((tm, d), lambda i, j: (i, 0)),
            pl.BlockSpec((taps - 1, n_seq_tile, cn), lambda i, j: (0, i, j)),
            pl.BlockSpec((d, cn), lambda i, j: (0, j)),
            pl.BlockSpec((d, cn), lambda i, j: (0, n_j + j)),
            pl.BlockSpec((d, cn), lambda i, j: (0, 2 * n_j + j)),
            pl.BlockSpec((taps, cn), lambda i, j: (0, j)),
            vec(), vec(), vec(), vec(), vec(),
            _const_spec((d, d), 2),
        ],
        out_specs=[
            pl.BlockSpec((tm, d), lambda i, j: (i, 0)),
            pl.BlockSpec((taps - 1, n_seq_tile, cn), lambda i, j: (0, i, j)),
        ],
        out_shape=[
            jax.ShapeDtypeStruct((n_seq * t_new, d), F32),
            jax.ShapeDtypeStruct((taps - 1, n_seq, d), F32),
        ],
        scratch_shapes=[
            pltpu.VMEM((tm, d), BF16),
            pltpu.VMEM((cn // LANES, tm, LANES), F32),
            pltpu.VMEM((cn // LANES, tm, LANES), F32),
            pltpu.VMEM((taps, SUBLANES, cn), F32),
            pltpu.VMEM((n_j, tm, cn), F32),
            pltpu.VMEM((n_j, tm, cn), BF16),
            pltpu.VMEM((tm, d), BF16),
        ],
        compiler_params=pltpu.CompilerParams(
            dimension_semantics=("arbitrary", "arbitrary"),
            vmem_limit_bytes=VMEM_LIMIT),
        name="conv_layer_sample",
    )(x2, state.transpose(1, 0, 2), w_in, w_in, w_in, cw, cb, lng, lnb, gpre, gpost, w_out)
    return h.reshape(n_seq, t_new, d), st.transpose(1, 0, 2)


def _attn_projections(h_ref, gkv_ref, gpre_ref, wkv_ref, win_ref, ukv_ref, u_ref, q_ref, sg_ref, tm,
                      n_chunk):
    d = h_ref.shape[-1]

    def body(r, c):
        rows = _rows(r, ROW_BLK)
        x = h_ref[rows, :]
        xn = x * _rms_scale(x)
        ukv_ref[rows, :] = (xn * gkv_ref[...]).astype(BF16)
        u_ref[rows, :] = (xn * gpre_ref[...]).astype(BF16)
        return c
    lax.fori_loop(0, tm // ROW_BLK, body, 0, unroll=2)

    kv = _dot(ukv_ref[...], wkv_ref[...])
    u = u_ref[...]
    cw = d // n_chunk
    scale = HEAD_DIM ** -0.5 * LOG2E
    for n in range(n_chunk):
        cols = slice(n * cw, (n + 1) * cw)
        q_ref[:, cols] = (_dot(u, win_ref[:, cols]) * scale).astype(BF16)
    for n in range(n_chunk):
        cols = slice(n * cw, (n + 1) * cw)
        sg_ref[:, cols] = jax.nn.silu(_dot(u, win_ref[:, d + n * cw:d + (n + 1) * cw])).astype(BF16)
    return kv


def _attn_output(act_ref, h_ref, gpost_ref, wout_ref, y_ref, tm):
    y_ref[...] = _dot(act_ref[...], wout_ref[...])

    def post_body(r, c):
        rows = [_rows(2 * r + e, ROW_BLK) for e in range(2)]
        os = [y_ref[rw, :] for rw in rows]
        res = [h_ref[rw, :] + o * _rms_scale(o) * gpost_ref[...] for rw, o in zip(rows, os)]
        for rw, v in zip(rows, res):
            y_ref[rw, :] = v
        return c
    lax.fori_loop(0, tm // (2 * ROW_BLK), post_body, 0)


def _attn_block(sink_ref, q_ref, sg_ref, kpad_ref, vpad_ref, act_ref, r0, *, n_kv, first):
    blk = WINDOW
    pair_w = 2 * HEAD_DIM
    n_pair = GROUP // 2
    qi = lax.broadcasted_iota(jnp.int32, (blk, blk), 0)
    kj = lax.broadcasted_iota(jnp.int32, (blk, blk), 1)
    own = kj <= qi
    low = lax.broadcasted_iota(jnp.int32, (blk, pair_w), 1) < HEAD_DIM
    rows = pl.ds(r0, blk)
    krows = pl.ds(r0, 2 * blk)
    for g in range(n_kv):
        qs = jnp.concatenate(
            [q_ref[rows, (g * n_pair + pp) * pair_w:(g * n_pair + pp + 1) * pair_w]
             for pp in range(n_pair)], axis=0)
        probs, stats = [], []
        for e in range(2):
            s = _dot_t(qs, kpad_ref[krows, (2 * g + e) * pair_w:(2 * g + e + 1) * pair_w])
            p_parts, st = [], []
            for pp in range(n_pair):
                sh = s[pp * blk:(pp + 1) * blk]
                prev = sh[:, :blk] if first is False else sh[:, :blk] + jnp.where(first, -jnp.inf, 0.0)
                sf = jnp.where(own, sh[:, blk:], prev)
                sink = sink_ref[g * GROUP + 2 * pp + e] * LOG2E
                m = jnp.maximum(jnp.max(sf, axis=-1, keepdims=True), sink)
                p = jnp.exp2(sf - m)
                p_parts.append(jnp.concatenate([jnp.where(own, 0.0, p), jnp.where(own, p, 0.0)],
                                               axis=1).astype(BF16))
                st.append(jnp.exp2(sink - m))
            probs.append(jnp.concatenate(p_parts, axis=0))
            stats.append(st)
        o = (_dot(probs[0], vpad_ref[krows, (2 * g) * 2 * pair_w:(2 * g + 1) * 2 * pair_w])
             + _dot(probs[1], vpad_ref[krows, (2 * g + 1) * 2 * pair_w:(2 * g + 2) * 2 * pair_w]))
        for pp in range(n_pair):
            oh = o[pp * blk:(pp + 1) * blk]
            den = oh[:, pair_w:] + jnp.where(low, stats[0][pp], stats[1][pp])
            c0 = (g * n_pair + pp) * pair_w
            act_ref[rows, c0:c0 + pair_w] = (
                oh[:, :pair_w] / den * sg_ref[rows, c0:c0 + pair_w].astype(F32)).astype(BF16)


def _store_padded_kv(kv, kpad_ref, vpad_ref, row0, n_kv):
    n = kv.shape[0]
    kvw = n_kv * HEAD_DIM
    pair_w = 2 * HEAD_DIM
    low = lax.broadcasted_iota(jnp.int32, (n, pair_w), 1) < HEAD_DIM
    halves = (low, jnp.logical_not(low))
    for gam in range(n_kv // 2):
        kc = kv[:, gam * pair_w:(gam + 1) * pair_w]
        vc = kv[:, kvw + gam * pair_w:kvw + (gam + 1) * pair_w]
        for side in range(2):
            g = 2 * gam + side
            k_here = jnp.where(halves[side], kc, 0.0)
            v_here = jnp.where(halves[side], vc, 0.0)
            k_other = pltpu.roll(k_here, HEAD_DIM, axis=1)
            v_other = pltpu.roll(v_here, HEAD_DIM, axis=1)
            for e in range(2):
                kc0 = (2 * g + e) * pair_w
                vc0 = (2 * g + e) * 2 * pair_w
                kpad_ref[row0:row0 + n, kc0:kc0 + pair_w] = (k_here if e == side else k_other).astype(BF16)
                vpad_ref[row0:row0 + n, vc0:vc0 + pair_w] = (v_here if e == side else v_other).astype(BF16)
                vpad_ref[row0:row0 + n, vc0 + pair_w:vc0 + 2 * pair_w] = (
                    jnp.where(halves[e], 1.0, 0.0).astype(BF16))


def _attn_prompt_kernel(sink_ref, h_ref, gkv_ref, gpre_ref, gpost_ref, wkv_ref, win_ref, wout_ref,
                        y_ref, ck_ref, cv_ref,
                        ukv_ref, u_ref, q_ref, sg_ref, kpad_ref, vpad_ref, act_ref, *, tm, n_kv):
    i = pl.program_id(1)
    kvw = n_kv * HEAD_DIM
    blk = WINDOW

    @pl.when((pl.program_id(0) == 0) & (i == 0))
    def _():
        kpad_ref[...] = jnp.zeros(kpad_ref.shape, BF16)
        vpad_ref[...] = jnp.zeros(vpad_ref.shape, BF16)

    kv = _attn_projections(h_ref, gkv_ref, gpre_ref, wkv_ref, win_ref, ukv_ref, u_ref, q_ref, sg_ref,
                           tm, 4)

    keep = i > 0
    kpad_ref[0:blk, :] = jnp.where(keep, kpad_ref[tm:tm + blk, :], jnp.zeros((), BF16))
    vpad_ref[0:blk, :] = jnp.where(keep, vpad_ref[tm:tm + blk, :], jnp.zeros((), BF16))

    _store_padded_kv(kv, kpad_ref, vpad_ref, blk, n_kv)

    ck_ref[...] = kv[tm - blk:, :kvw]
    cv_ref[...] = kv[tm - blk:, kvw:]

    block = functools.partial(_attn_block, sink_ref, q_ref, sg_ref, kpad_ref, vpad_ref, act_ref, n_kv=n_kv)

    for bi in range(tm // blk):
        block(bi * blk, first=(i == 0) if bi == 0 else False)

    _attn_output(act_ref, h_ref, gpost_ref, wout_ref, y_ref, tm)


def _attn_layer_prompt(h, sinks, gkv, gpre, gpost, w_kv, w_in, w_out, *, tm):
    b, t, d = h.shape
    kvw = w_kv.shape[1] // 2
    n_kv = kvw // HEAD_DIM
    kern = functools.partial(_attn_prompt_kernel, tm=tm, n_kv=n_kv)
    vec = lambda: _const_spec((1, d), 2)
    return pl.pallas_call(
        kern,
        grid=(b, t // tm),
        in_specs=[
            pl.BlockSpec(memory_space=pltpu.SMEM),
            pl.BlockSpec((None, tm, d), lambda bb, i: (bb, i, 0)),
            vec(), vec(), vec(),
            _const_spec(w_kv.shape, 2),
            _const_spec(w_in.shape, 2),
            _const_spec(w_out.shape, 2),
        ],
        out_specs=[
            pl.BlockSpec((None, tm, d), lambda bb, i: (bb, i, 0)),
            pl.BlockSpec((None, WINDOW, kvw), lambda bb, i: (bb, 0, 0)),
            pl.BlockSpec((None, WINDOW, kvw), lambda bb, i: (bb, 0, 0)),
        ],
        out_shape=[
            jax.ShapeDtypeStruct((b, t, d), F32),
            jax.ShapeDtypeStruct((b, WINDOW, kvw), F32),
            jax.ShapeDtypeStruct((b, WINDOW, kvw), F32),
        ],
        scratch_shapes=[
            pltpu.VMEM((tm, d), BF16),
            pltpu.VMEM((tm, d), BF16),
            pltpu.VMEM((tm, d), BF16),
            pltpu.VMEM((tm, d), BF16),
            pltpu.VMEM((WINDOW + tm, 4 * kvw), BF16),
            pltpu.VMEM((WINDOW + tm, 8 * kvw), BF16),
            pltpu.VMEM((tm, d), BF16),
        ],
        compiler_params=pltpu.CompilerParams(
            dimension_semantics=("arbitrary", "arbitrary"),
            vmem_limit_bytes=VMEM_LIMIT),
        name="attn_layer_prompt",
    )(sinks, h, gkv, gpre, gpost, w_kv, w_in, w_out)


def _attn_sample_proj_kernel(h_ref, gkv_ref, gpre_ref, wkv_ref, win_ref, kv_ref, q_ref, sg_ref,
                             ukv_ref, u_ref, *, tm):
    kv_ref[...] = _attn_projections(h_ref, gkv_ref, gpre_ref, wkv_ref, win_ref, ukv_ref, u_ref, q_ref,
                                    sg_ref, tm, 4)


def _attn_sample_core_kernel(sink_ref, q_ref, sg_ref, kvt_ref, ckt_in_ref, cvt_in_ref,
                             act_ref, ckt_ref, cvt_ref, *, n_seq, t_new, n_kv, n_par):
    kvw = n_kv * HEAD_DIM
    w_buf = ckt_in_ref.shape[-1]
    first_new = w_buf - t_new
    assert kvt_ref.shape[-1] == w_buf, "the new K/V rows of one grid step must fill one lane block"
    n_rows = GROUP * n_kv * t_new
    tq = lax.broadcasted_iota(jnp.int32, (n_rows, w_buf), 0) % t_new
    kj = lax.broadcasted_iota(jnp.int32, (n_rows, w_buf), 1)
    mask_c = kj > tq
    mask_n = (kj >= first_new) & (kj - first_new <= tq)
    is_new = lax.broadcasted_iota(jnp.int32, (kvw, w_buf), 1) >= first_new
    row_head = lax.broadcasted_iota(jnp.int32, (n_rows, 1), 0) // t_new
    sinkv = jnp.zeros((n_rows, 1), F32)
    for h in range(GROUP * n_kv):
        sinkv = jnp.where(row_head == h, sink_ref[h] * LOG2E, sinkv)
    pair_w = 2 * HEAD_DIM
    low = lax.broadcasted_iota(jnp.int32, (t_new, pair_w), 1) < HEAD_DIM
    halves = (low, jnp.logical_not(low))
    zero_blk = jnp.zeros((t_new, pair_w), F32)

    def to_half(block, src_half, dst_half):
        moved = block if src_half == dst_half else pltpu.roll(block, HEAD_DIM, axis=1)
        return jnp.where(halves[dst_half], moved, 0.0)

    def one_seq(s):
        rows = pl.ds(pl.multiple_of(s * t_new, t_new), t_new)
        kct = ckt_in_ref[s].reshape(kvw, w_buf)
        vct = cvt_in_ref[s].reshape(kvw, w_buf)
        newt = pltpu.roll(kvt_ref[...], first_new - s * t_new, axis=1)
        knt = jnp.where(is_new, newt[:kvw], 0.0)
        vnt = jnp.where(is_new, newt[kvw:], 0.0)
        ckt_ref[s] = jnp.where(is_new, knt, pltpu.roll(kct, first_new, axis=1)).reshape(ckt_ref.shape[1:])
        cvt_ref[s] = jnp.where(is_new, vnt, pltpu.roll(vct, first_new, axis=1)).reshape(cvt_ref.shape[1:])
        qparts = []
        for cb in range(GROUP * n_kv // 2):
            blk = q_ref[rows, cb * pair_w:(cb + 1) * pair_w].astype(F32)
            g = 2 * cb // GROUP
            for e in range(2):
                piece = to_half(blk, e, g % 2)
                qparts.append(jnp.concatenate(
                    [piece if kb == g // 2 else zero_blk for kb in range(kvw // pair_w)], axis=1))
        qbd = jnp.concatenate(qparts, axis=0).astype(BF16)
        s_c = jnp.where(mask_c, _dot(qbd, kct.astype(BF16)), -jnp.inf)
        s_n = jnp.where(mask_n, _dot(qbd, knt.astype(BF16)), -jnp.inf)
        m = jnp.maximum(jnp.max(jnp.maximum(s_c, s_n), axis=-1, keepdims=True), sinkv)
        p_c = jnp.exp2(s_c - m)
        p_n = jnp.exp2(s_n - m)
        den = jnp.sum(p_c + p_n, axis=-1, keepdims=True) + jnp.exp2(sinkv - m)
        o = (_dot_t(p_c.astype(BF16), vct.astype(BF16))
             + _dot_t(p_n.astype(BF16), vnt.astype(BF16))) * (1.0 / den)
        for cb in range(GROUP * n_kv // 2):
            g = 2 * cb // GROUP
            kcols = slice((g // 2) * pair_w, (g // 2 + 1) * pair_w)
            pieces = [to_half(o[(2 * cb + e) * t_new:(2 * cb + e + 1) * t_new, kcols], g % 2, e)
                      for e in range(2)]
            cols = slice(cb * pair_w, (cb + 1) * pair_w)
            act_ref[rows, cols] = ((pieces[0] + pieces[1]) * sg_ref[rows, cols].astype(F32)).astype(BF16)

    def seq_body(sb, carry):
        for q in range(n_par):
            one_seq(sb * n_par + q)
        return carry
    lax.fori_loop(0, n_seq // n_par, seq_body, 0)


def _attn_sample_out_kernel(act_ref, h_ref, gpost_ref, wout_ref, y_ref, *, tm):
    _attn_output(act_ref, h_ref, gpost_ref, wout_ref, y_ref, tm)


def _attn_layer_sample(h, cache_k, cache_v, sinks, gkv, gpre, gpost, w_kv, w_in, w_out, *, tm,
                       n_seq_tile):
    n_seq, t_new, d = h.shape
    kvw = w_kv.shape[1] // 2
    n_kv = kvw // HEAD_DIM
    w_buf = cache_k.shape[1]
    n_tok = n_seq * t_new
    h2 = h.reshape(n_tok, d)
    vec = lambda: _const_spec((1, d), 1)
    params = pltpu.CompilerParams(dimension_semantics=("arbitrary",), vmem_limit_bytes=VMEM_LIMIT)

    kv, q, sg = pl.pallas_call(
        functools.partial(_attn_sample_proj_kernel, tm=tm),
        grid=(n_tok // tm,),
        in_specs=[
            pl.BlockSpec((tm, d), lambda i: (i, 0)),
            vec(), vec(),
            _const_spec(w_kv.shape, 1),
            _const_spec(w_in.shape, 1),
        ],
        out_specs=[
            pl.BlockSpec((tm, 2 * kvw), lambda i: (i, 0)),
            pl.BlockSpec((tm, d), lambda i: (i, 0)),
            pl.BlockSpec((tm, d), lambda i: (i, 0)),
        ],
        out_shape=[
            jax.ShapeDtypeStruct((n_tok, 2 * kvw), F32),
            jax.ShapeDtypeStruct((n_tok, d), BF16),
            jax.ShapeDtypeStruct((n_tok, d), BF16),
        ],
        scratch_shapes=[pltpu.VMEM((tm, d), BF16), pltpu.VMEM((tm, d), BF16)],
        compiler_params=params,
        name="attn_sample_proj",
    )(h2, gkv, gpre, w_kv, w_in)

    ckt_in = cache_k.transpose(0, 2, 3, 1)
    cvt_in = cache_v.transpose(0, 2, 3, 1)
    rows = n_seq_tile * t_new
    cache_blk = pl.BlockSpec((n_seq_tile, n_kv, HEAD_DIM, w_buf), lambda i: (i, 0, 0, 0))
    act, ckt, cvt = pl.pallas_call(
        functools.partial(_attn_sample_core_kernel, n_seq=n_seq_tile, t_new=t_new, n_kv=n_kv, n_par=4),
        grid=(n_seq // n_seq_tile,),
        in_specs=[
            pl.BlockSpec(memory_space=pltpu.SMEM),
            pl.BlockSpec((rows, d), lambda i: (i, 0)),
            pl.BlockSpec((rows, d), lambda i: (i, 0)),
            pl.BlockSpec((2 * kvw, rows), lambda i: (0, i)),
            cache_blk, cache_blk,
        ],
        out_specs=[pl.BlockSpec((rows, d), lambda i: (i, 0)), cache_blk, cache_blk],
        out_shape=[
            jax.ShapeDtypeStruct((n_tok, d), BF16),
            jax.ShapeDtypeStruct(ckt_in.shape, F32),
            jax.ShapeDtypeStruct(cvt_in.shape, F32),
        ],
        compiler_params=params,
        name="attn_sample_core",
    )(sinks, q, sg, kv.T, ckt_in, cvt_in)
    ck = ckt.transpose(0, 3, 1, 2)
    cv = cvt.transpose(0, 3, 1, 2)

    y = pl.pallas_call(
        functools.partial(_attn_sample_out_kernel, tm=tm),
        grid=(n_tok // tm,),
        in_specs=[
            pl.BlockSpec((tm, d), lambda i: (i, 0)),
            pl.BlockSpec((tm, d), lambda i: (i, 0)),
            vec(),
            _const_spec(w_out.shape, 1),
        ],
        out_specs=pl.BlockSpec((tm, d), lambda i: (i, 0)),
        out_shape=jax.ShapeDtypeStruct((n_tok, d), F32),
        compiler_params=params,
        name="attn_sample_out",
    )(act, h2, gpost, w_out)
    return y.reshape(n_seq, t_new, d), ck, cv


def kernel(x_prompt, x_sample, state_conv, cache_k, cache_v, norm_pre, norm_post, w_in_a, conv_w, conv_b, ln_g, ln_b, w_out_a, kv_norm, w_kv, w_in_b, sinks, w_out_b):
    n_a = w_in_a.shape[0]
    assert n_a == 1 and w_in_b.shape[0] == 1 and norm_pre.shape[0] == 2
    d = x_prompt.shape[-1]
    n_seq, w_buf, n_kv, hd = cache_k.shape
    assert hd == HEAD_DIM and w_buf == WINDOW

    row = lambda v: v.reshape(1, -1)
    w_in_a_bf = w_in_a[0].astype(BF16)
    w_out_a_bf = w_out_a[0].astype(BF16)
    w_kv_bf = w_kv.astype(BF16)
    w_in_b_bf = w_in_b[0].astype(BF16)
    w_out_b_bf = w_out_b[0].astype(BF16)
    conv_args = (w_in_a_bf, conv_w[0], row(conv_b[0]), row(ln_g[0]), row(ln_b[0]), row(norm_pre[0]),
                 row(norm_post[0]), w_out_a_bf)
    attn_args = (sinks[0], row(kv_norm), row(norm_pre[1]), row(norm_post[1]), w_kv_bf, w_in_b_bf,
                 w_out_b_bf)

    h_p, st_p = _conv_layer_prompt(x_prompt, *conv_args, tm=512, cn=256)
    y_p, ck_p, cv_p = _attn_layer_prompt(h_p, *attn_args, tm=256)

    h_s, st_s = _conv_layer_sample(x_sample, state_conv[0], *conv_args, n_seq_tile=64, cn=256)
    y_s, ck_s, cv_s = _attn_layer_sample(
        h_s, cache_k, cache_v,
        *attn_args, tm=512, n_seq_tile=16)

    b = x_prompt.shape[0]
    return (y_p, y_s, st_p[None], ck_p.reshape(b, w_buf, n_kv, hd), cv_p.reshape(b, w_buf, n_kv, hd),
            st_s[None], ck_s, cv_s)
```

```python
import functools

import jax
import jax.numpy as jnp
from jax import lax
from jax.experimental import pallas as pl
from jax.experimental.pallas import tpu as pltpu

RMS_EPS = 1e-6
LN_EPS = 1e-5
HEAD_DIM = 64
GROUP = 8
WINDOW = 128
SUBLANES = 8
LANES = 128
LOG2E = 1.4426950408889634
HALO = 32
VMEM_LIMIT = 56 * 1024 * 1024
ROW_BLK = 64

BF16 = jnp.bfloat16
F32 = jnp.float32


def _rows(i, n):
    return pl.ds(pl.multiple_of(i * n, n), n)


def _rms_scale(x):
    return lax.rsqrt(jnp.mean(x * x, axis=-1, keepdims=True) + RMS_EPS)


def _dot(a, b):
    return jnp.dot(a, b, preferred_element_type=F32)


def _dot_t(a, b):
    return lax.dot_general(a, b, (((1,), (1,)), ((), ())), preferred_element_type=F32)


def _pre_norm_to_bf16(x_ref, g_ref, u_ref, tm):
    def body(r, c):
        rows = _rows(r, ROW_BLK)
        x = x_ref[rows, :]
        u_ref[rows, :] = (x * _rms_scale(x) * g_ref[...]).astype(BF16)
        return c
    lax.fori_loop(0, tm // ROW_BLK, body, 0, unroll=2)


def _glu_chunk(u_ref, wa_ref, wb_ref, wg_ref):
    u = u_ref[...]
    c = _dot(u, wa_ref[...]) * jax.nn.sigmoid(_dot(u, wb_ref[...]))
    sg = jax.nn.silu(_dot(u, wg_ref[...])).astype(BF16)
    return c, sg


def _broadcast_taps(cw_ref, wbc_ref, taps):
    for k in range(taps):
        wbc_ref[k] = jnp.broadcast_to(cw_ref[k:k + 1, :], wbc_ref.shape[1:])


def _conv_finalize(j_chunks, tm, cn, y_ref, sg_ref, act_ref, x_ref, h_ref, cb_ref, lng_ref, lnb_ref,
                   gpost_ref, wout_ref):
    d = j_chunks * cn

    ln_rows = ROW_BLK

    def ln_body(r, c):
        rows = _rows(r, ln_rows)
        ys = [y_ref[jj, rows, :] + cb_ref[:, jj * cn:(jj + 1) * cn] for jj in range(j_chunks)]
        mu = jnp.sum(sum(ys), axis=-1, keepdims=True) * (1.0 / d)
        yc = [y - mu for y in ys]
        var = jnp.sum(sum(y * y for y in yc), axis=-1, keepdims=True) * (1.0 / d)
        rstd = lax.rsqrt(var + LN_EPS)
        for jj in range(j_chunks):
            cols = slice(jj * cn, (jj + 1) * cn)
            t = jax.nn.silu(yc[jj] * rstd * lng_ref[:, cols] + lnb_ref[:, cols])
            act_ref[rows, cols] = (t * sg_ref[jj, rows, :].astype(F32)).astype(BF16)
        return c
    lax.fori_loop(0, tm // ln_rows, ln_body, 0, unroll=2)

    h_ref[...] = _dot(act_ref[...], wout_ref[...])

    def post_body(r, c):
        rows = [_rows(2 * r + e, ROW_BLK) for e in range(2)]
        os = [h_ref[rw, :] for rw in rows]
        res = [x_ref[rw, :] + o * _rms_scale(o) * gpost_ref[...] for rw, o in zip(rows, os)]
        for rw, v in zip(rows, res):
            h_ref[rw, :] = v
        return c
    lax.fori_loop(0, tm // (2 * ROW_BLK), post_body, 0)


def _conv_chunk(cext_ref, wbc_ref, y_ref, jc, *, tm, cn, taps, grp):
    lbs = cn // LANES
    assert lbs >= 2, "the row-interleaved layout needs at least two lane blocks per chunk"
    lead = HALO - (taps - 1)
    for g in range(tm // (grp * SUBLANES)):
        base = g * grp * SUBLANES
        for lb in range(lbs):
            ls = slice(lb * LANES, (lb + 1) * LANES)
            acc = [None] * grp
            for k in range(taps):
                wv = wbc_ref[jc, k, :, ls]
                for gi in range(grp):
                    r0 = base + SUBLANES * gi + k + lead
                    t = wv * cext_ref[jc, pl.ds(lbs * r0 + lb, SUBLANES, stride=lbs), :]
                    acc[gi] = t if acc[gi] is None else acc[gi] + t
            for gi in range(grp):
                y_ref[jc, base + SUBLANES * gi:base + SUBLANES * (gi + 1), ls] = acc[gi]


def _conv_prompt_kernel(x_ref, w_hbm, cw_ref, cb_ref, lng_ref, lnb_ref, gpre_ref, gpost_ref, wout_ref,
                        h_ref, st_ref,
                        u_ref, wbuf_ref, wsem, cext_ref, wbc_ref, y_ref, sg_ref, act_ref,
                        *, tm, cn, taps, grp):
    bb = pl.program_id(0)
    i = pl.program_id(1)
    n_chunks = cext_ref.shape[0]
    lbs = cn // LANES
    first_step = (bb == 0) & (i == 0)
    last_step = (bb == pl.num_programs(0) - 1) & (i == pl.num_programs(1) - 1)

    def weight_copy(jc, slot, part):
        col0 = pl.multiple_of((part * n_chunks + jc) * cn, cn)
        return pltpu.make_async_copy(w_hbm.at[:, pl.ds(col0, cn)], wbuf_ref.at[slot, part],
                                     wsem.at[slot, part])

    @pl.when(first_step)
    def _():
        for part in range(3):
            weight_copy(0, 0, part).start()
        for jj in range(n_chunks):
            for k in range(taps):
                wbc_ref[jj, k] = jnp.broadcast_to(cw_ref[k:k + 1, jj * cn:(jj + 1) * cn], (SUBLANES, cn))

    _pre_norm_to_bf16(x_ref, gpre_ref, u_ref, tm)

    def chunk_body(j, carry):
        slot = j % 2
        for part in range(3):
            weight_copy(j, slot, part).wait()

        @pl.when(jnp.logical_not(last_step & (j == n_chunks - 1)))
        def _():
            nxt = jnp.where(j == n_chunks - 1, 0, j + 1)
            for part in range(3):
                weight_copy(nxt, 1 - slot, part).start()

        @pl.when(i == 0)
        def _():
            cext_ref[j, 0:lbs * HALO, :] = jnp.zeros((lbs * HALO, LANES), F32)

        @pl.when(i > 0)
        def _():
            cext_ref[j, 0:lbs * HALO, :] = cext_ref[j, lbs * tm:lbs * (tm + HALO), :]

        u = u_ref[...]
        c = _dot(u, wbuf_ref[slot, 0]) * jax.nn.sigmoid(_dot(u, wbuf_ref[slot, 1]))
        sg_ref[j] = jax.nn.silu(_dot(u, wbuf_ref[slot, 2])).astype(BF16)
        for lb in range(lbs):
            cext_ref[j, pl.ds(lbs * HALO + lb, tm, stride=lbs), :] = c[:, lb * LANES:(lb + 1) * LANES]
        _conv_chunk(cext_ref, wbc_ref, y_ref, j, tm=tm, cn=cn, taps=taps, grp=grp)
        return carry
    lax.fori_loop(0, n_chunks, chunk_body, 0)

    _conv_finalize(n_chunks, tm, cn, y_ref, sg_ref, act_ref, x_ref, h_ref, cb_ref,
                   lng_ref, lnb_ref, gpost_ref, wout_ref)

    @pl.when(i == pl.num_programs(1) - 1)
    def _():
        tail = lbs * (HALO + tm - (taps - 1))
        for jj in range(n_chunks):
            for lb in range(lbs):
                c0 = jj * cn + lb * LANES
                st_ref[:, c0:c0 + LANES] = cext_ref[jj, pl.ds(tail + lb, taps - 1, stride=lbs), :]


def _conv_sample_kernel(x_ref, st_in_ref, wa_ref, wb_ref, wg_ref, cw_ref, cb_ref, lng_ref, lnb_ref,
                        gpre_ref, gpost_ref, wout_ref, h_ref, st_ref,
                        u_ref, c_ref, ystg_ref, wbc_ref, y_ref, sg_ref, act_ref, *, tm, cn, taps, t_new):
    j = pl.program_id(1)
    n_j = pl.num_programs(1)
    n_seq = tm // t_new
    hist = taps - 1

    @pl.when(j == 0)
    def _():
        _pre_norm_to_bf16(x_ref, gpre_ref, u_ref, tm)

    c, sg = _glu_chunk(u_ref, wa_ref, wb_ref, wg_ref)
    sg_ref[j] = sg
    _broadcast_taps(cw_ref, wbc_ref, taps)

    for lb in range(cn // LANES):
        ls = slice(lb * LANES, (lb + 1) * LANES)
        c_ref[lb] = c[:, ls]
        new_rows = [c_ref[lb, pl.ds(t, n_seq, stride=t_new), :] for t in range(t_new)]
        for r in range(hist):
            st_ref[r, :, ls] = st_in_ref[r + t_new, :, ls] if r + t_new < hist else new_rows[r + t_new - hist]
        for sb in range(n_seq // SUBLANES):
            seqs = slice(sb * SUBLANES, (sb + 1) * SUBLANES)
            full = [st_in_ref[r, seqs, ls] for r in range(hist)] + [nr[seqs] for nr in new_rows]
            acc = [None] * t_new
            for k in range(taps):
                wv = wbc_ref[k, :, ls]
                for t in range(t_new):
                    term = wv * full[t + k]
                    acc[t] = term if acc[t] is None else acc[t] + term
            for t in range(t_new):
                ystg_ref[lb, pl.ds(sb * SUBLANES * t_new + t, SUBLANES, stride=t_new), :] = acc[t]
        y_ref[j, :, ls] = ystg_ref[lb]

    @pl.when(j == n_j - 1)
    def _():
        _conv_finalize(y_ref.shape[0], tm, cn, y_ref, sg_ref, act_ref, x_ref, h_ref, cb_ref,
                       lng_ref, lnb_ref, gpost_ref, wout_ref)


def _const_spec(shape, n_grid):
    zeros = (0,) * len(shape)
    return pl.BlockSpec(shape, lambda *_: zeros, pipeline_mode=pl.Buffered(1))


def _conv_layer_prompt(x, w_in, cw, cb, lng, lnb, gpre, gpost, w_out, *, tm, cn):
    b, t, d = x.shape
    taps = cw.shape[0]
    n_j = d // cn
    lbs = cn // LANES
    grp = 4
    kern = functools.partial(_conv_prompt_kernel, tm=tm, cn=cn, taps=taps, grp=grp)
    vec = lambda: _const_spec((1, d), 2)
    return pl.pallas_call(
        kern,
        grid=(b, t // tm),
        in_specs=[
            pl.BlockSpec((None, tm, d), lambda bb, i: (bb, i, 0)),
            pl.BlockSpec(memory_space=pl.ANY),
            _const_spec((taps, d), 2),
            vec(), vec(), vec(), vec(), vec(),
            _const_spec((d, d), 2),
        ],
        out_specs=[
            pl.BlockSpec((None, tm, d), lambda bb, i: (bb, i, 0)),
            pl.BlockSpec((None, taps - 1, d), lambda bb, i: (bb, 0, 0)),
        ],
        out_shape=[
            jax.ShapeDtypeStruct((b, t, d), F32),
            jax.ShapeDtypeStruct((b, taps - 1, d), F32),
        ],
        scratch_shapes=[
            pltpu.VMEM((tm, d), BF16),
            pltpu.VMEM((2, 3, d, cn), BF16),
            pltpu.SemaphoreType.DMA((2, 3)),
            pltpu.VMEM((n_j, lbs * (HALO + tm), LANES), F32),
            pltpu.VMEM((n_j, taps, SUBLANES, cn), F32),
            pltpu.VMEM((n_j, tm, cn), F32),
            pltpu.VMEM((n_j, tm, cn), BF16),
            pltpu.VMEM((tm, d), BF16),
        ],
        compiler_params=pltpu.CompilerParams(
            dimension_semantics=("arbitrary", "arbitrary"),
            vmem_limit_bytes=VMEM_LIMIT),
        name="conv_layer_prompt",
    )(x, w_in, cw, cb, lng, lnb, gpre, gpost, w_out)


def _conv_layer_sample(x, state, w_in, cw, cb, lng, lnb, gpre, gpost, w_out, *, n_seq_tile, cn):
    n_seq, t_new, d = x.shape
    taps = cw.shape[0]
    n_j = d // cn
    tm = n_seq_tile * t_new
    x2 = x.reshape(n_seq * t_new, d)
    kern = functools.partial(_conv_sample_kernel, tm=tm, cn=cn, taps=taps, t_new=t_new)
    vec = lambda: _const_spec((1, d), 2)
    h, st = pl.pallas_call(
        kern,
        grid=(n_seq // n_seq_tile, n_j),
        in_specs=[
            pl.BlockSpec((tm, d), lambda i, j: (i, 0)),
            pl.BlockSpec((taps - 1, n_seq_tile, cn), lambda i, j: (0, i, j)),
            pl.BlockSpec((d, cn), lambda i, j: (0, j)),
            pl.BlockSpec((d, cn), lambda i, j: (0, n_j + j)),
            pl.BlockSpec((d, cn), lambda i, j: (0, 2 * n_j + j)),
            pl.BlockSpec((taps, cn), lambda i, j: (0, j)),
            vec(), vec(), vec(), vec(), vec(),
            _const_spec((d, d), 2),
        ],
        out_specs=[
            pl.BlockSpec((tm, d), lambda i, j: (i, 0)),
            pl.BlockSpec((taps - 1, n_seq_tile, cn), lambda i, j: (0, i, j)),
        ],
        out_shape=[
            jax.ShapeDtypeStruct((n_seq * t_new, d), F32),
            jax.ShapeDtypeStruct((taps - 1, n_seq, d), F32),
        ],
        scratch_shapes=[
            pltpu.VMEM((tm, d), BF16),
            pltpu.VMEM((cn // LANES, tm, LANES), F32),
            pltpu.VMEM((cn // LANES, tm, LANES), F32),
            pltpu.VMEM((taps, SUBLANES, cn), F32),
            pltpu.VMEM((n_j, tm, cn), F32),
            pltpu.VMEM((n_j, tm, cn), BF16),
            pltpu.VMEM((tm, d), BF16),
        ],
        compiler_params=pltpu.CompilerParams(
            dimension_semantics=("arbitrary", "arbitrary"),
            vmem_limit_bytes=VMEM_LIMIT),
        name="conv_layer_sample",
    )(x2, state.transpose(1, 0, 2), w_in, w_in, w_in, cw, cb, lng, lnb, gpre, gpost, w_out)
    return h.reshape(n_seq, t_new, d), st.transpose(1, 0, 2)


def _attn_projections(h_ref, gkv_ref, gpre_ref, wkv_ref, win_ref, ukv_ref, u_ref, q_ref, sg_ref, tm,
                      n_chunk):
    d = h_ref.shape[-1]

    def body(r, c):
        rows = _rows(r, ROW_BLK)
        x = h_ref[rows, :]
        xn = x * _rms_scale(x)
        ukv_ref[rows, :] = (xn * gkv_ref[...]).astype(BF16)
        u_ref[rows, :] = (xn * gpre_ref[...]).astype(BF16)
        return c
    lax.fori_loop(0, tm // ROW_BLK, body, 0, unroll=2)

    kv = _dot(ukv_ref[...], wkv_ref[...])
    u = u_ref[...]
    cw = d // n_chunk
    scale = HEAD_DIM ** -0.5 * LOG2E
    for n in range(n_chunk):
        cols = slice(n * cw, (n + 1) * cw)
        q_ref[:, cols] = (_dot(u, win_ref[:, cols]) * scale).astype(BF16)
    for n in range(n_chunk):
        cols = slice(n * cw, (n + 1) * cw)
        sg_ref[:, cols] = jax.nn.silu(_dot(u, win_ref[:, d + n * cw:d + (n + 1) * cw])).astype(BF16)
    return kv


def _attn_output(act_ref, h_ref, gpost_ref, wout_ref, y_ref, tm):
    y_ref[...] = _dot(act_ref[...], wout_ref[...])

    def post_body(r, c):
        rows = [_rows(2 * r + e, ROW_BLK) for e in range(2)]
        os = [y_ref[rw, :] for rw in rows]
        res = [h_ref[rw, :] + o * _rms_scale(o) * gpost_ref[...] for rw, o in zip(rows, os)]
        for rw, v in zip(rows, res):
            y_ref[rw, :] = v
        return c
    lax.fori_loop(0, tm // (2 * ROW_BLK), post_body, 0)


def _attn_block(sink_ref, q_ref, sg_ref, kpad_ref, vpad_ref, act_ref, r0, *, n_kv, first):
    blk = WINDOW
    pair_w = 2 * HEAD_DIM
    n_pair = GROUP // 2
    qi = lax.broadcasted_iota(jnp.int32, (blk, blk), 0)
    kj = lax.broadcasted_iota(jnp.int32, (blk, blk), 1)
    own = kj <= qi
    low = lax.broadcasted_iota(jnp.int32, (blk, pair_w), 1) < HEAD_DIM
    rows = pl.ds(r0, blk)
    krows = pl.ds(r0, 2 * blk)
    for g in range(n_kv):
        qs = jnp.concatenate(
            [q_ref[rows, (g * n_pair + pp) * pair_w:(g * n_pair + pp + 1) * pair_w]
             for pp in range(n_pair)], axis=0)
        probs, stats = [], []
        for e in range(2):
            s = _dot_t(qs, kpad_ref[krows, (2 * g + e) * pair_w:(2 * g + e + 1) * pair_w])
            p_parts, st = [], []
            for pp in range(n_pair):
                sh = s[pp * blk:(pp + 1) * blk]
                prev = sh[:, :blk] if first is False else sh[:, :blk] + jnp.where(first, -jnp.inf, 0.0)
                sf = jnp.where(own, sh[:, blk:], prev)
                sink = sink_ref[g * GROUP + 2 * pp + e] * LOG2E
                m = jnp.maximum(jnp.max(sf, axis=-1, keepdims=True), sink)
                p = jnp.exp2(sf - m)
                p_parts.append(jnp.concatenate([jnp.where(own, 0.0, p), jnp.where(own, p, 0.0)],
                                               axis=1).astype(BF16))
                st.append(jnp.exp2(sink - m))
            probs.append(jnp.concatenate(p_parts, axis=0))
            stats.append(st)
        o = (_dot(probs[0], vpad_ref[krows, (2 * g) * 2 * pair_w:(2 * g + 1) * 2 * pair_w])
             + _dot(probs[1], vpad_ref[krows, (2 * g + 1) * 2 * pair_w:(2 * g + 2) * 2 * pair_w]))
        for pp in range(n_pair):
            oh = o[pp * blk:(pp + 1) * blk]
            den = oh[:, pair_w:] + jnp.where(low, stats[0][pp], stats[1][pp])
            c0 = (g * n_pair + pp) * pair_w
            act_ref[rows, c0:c0 + pair_w] = (
                oh[:, :pair_w] / den * sg_ref[rows, c0:c0 + pair_w].astype(F32)).astype(BF16)


def _store_padded_kv(kv, kpad_ref, vpad_ref, row0, n_kv):
    n = kv.shape[0]
    kvw = n_kv * HEAD_DIM
    pair_w = 2 * HEAD_DIM
    low = lax.broadcasted_iota(jnp.int32, (n, pair_w), 1) < HEAD_DIM
    halves = (low, jnp.logical_not(low))
    for gam in range(n_kv // 2):
        kc = kv[:, gam * pair_w:(gam + 1) * pair_w]
        vc = kv[:, kvw + gam * pair_w:kvw + (gam + 1) * pair_w]
        for side in range(2):
            g = 2 * gam + side
            k_here = jnp.where(halves[side], kc, 0.0)
            v_here = jnp.where(halves[side], vc, 0.0)
            k_other = pltpu.roll(k_here, HEAD_DIM, axis=1)
            v_other = pltpu.roll(v_here, HEAD_DIM, axis=1)
            for e in range(2):
                kc0 = (2 * g + e) * pair_w
                vc0 = (2 * g + e) * 2 * pair_w
                kpad_ref[row0:row0 + n, kc0:kc0 + pair_w] = (k_here if e == side else k_other).astype(BF16)
                vpad_ref[row0:row0 + n, vc0:vc0 + pair_w] = (v_here if e == side else v_other).astype(BF16)
                vpad_ref[row0:row0 + n, vc0 + pair_w:vc0 + 2 * pair_w] = (
                    jnp.where(halves[e], 1.0, 0.0).astype(BF16))


def _attn_prompt_kernel(sink_ref, h_ref, gkv_ref, gpre_ref, gpost_ref, wkv_ref, win_ref, wout_ref,
                        y_ref, ck_ref, cv_ref,
                        ukv_ref, u_ref, q_ref, sg_ref, kpad_ref, vpad_ref, act_ref, *, tm, n_kv):
    i = pl.program_id(1)
    kvw = n_kv * HEAD_DIM
    blk = WINDOW

    @pl.when((pl.program_id(0) == 0) & (i == 0))
    def _():
        kpad_ref[...] = jnp.zeros(kpad_ref.shape, BF16)
        vpad_ref[...] = jnp.zeros(vpad_ref.shape, BF16)

    kv = _attn_projections(h_ref, gkv_ref, gpre_ref, wkv_ref, win_ref, ukv_ref, u_ref, q_ref, sg_ref,
                           tm, 4)

    keep = i > 0
    kpad_ref[0:blk, :] = jnp.where(keep, kpad_ref[tm:tm + blk, :], jnp.zeros((), BF16))
    vpad_ref[0:blk, :] = jnp.where(keep, vpad_ref[tm:tm + blk, :], jnp.zeros((), BF16))

    _store_padded_kv(kv, kpad_ref, vpad_ref, blk, n_kv)

    ck_ref[...] = kv[tm - blk:, :kvw]
    cv_ref[...] = kv[tm - blk:, kvw:]

    block = functools.partial(_attn_block, sink_ref, q_ref, sg_ref, kpad_ref, vpad_ref, act_ref, n_kv=n_kv)

    for bi in range(tm // blk):
        block(bi * blk, first=(i == 0) if bi == 0 else False)

    _attn_output(act_ref, h_ref, gpost_ref, wout_ref, y_ref, tm)


def _attn_layer_prompt(h, sinks, gkv, gpre, gpost, w_kv, w_in, w_out, *, tm):
    b, t, d = h.shape
    kvw = w_kv.shape[1] // 2
    n_kv = kvw // HEAD_DIM
    kern = functools.partial(_attn_prompt_kernel, tm=tm, n_kv=n_kv)
    vec = lambda: _const_spec((1, d), 2)
    return pl.pallas_call(
        kern,
        grid=(b, t // tm),
        in_specs=[
            pl.BlockSpec(memory_space=pltpu.SMEM),
            pl.BlockSpec((None, tm, d), lambda bb, i: (bb, i, 0)),
            vec(), vec(), vec(),
            _const_spec(w_kv.shape, 2),
            _const_spec(w_in.shape, 2),
            _const_spec(w_out.shape, 2),
        ],
        out_specs=[
            pl.BlockSpec((None, tm, d), lambda bb, i: (bb, i, 0)),
            pl.BlockSpec((None, WINDOW, kvw), lambda bb, i: (bb, 0, 0)),
            pl.BlockSpec((None, WINDOW, kvw), lambda bb, i: (bb, 0, 0)),
        ],
        out_shape=[
            jax.ShapeDtypeStruct((b, t, d), F32),
            jax.ShapeDtypeStruct((b, WINDOW, kvw), F32),
            jax.ShapeDtypeStruct((b, WINDOW, kvw), F32),
        ],
        scratch_shapes=[
            pltpu.VMEM((tm, d), BF16),
            pltpu.VMEM((tm, d), BF16),
            pltpu.VMEM((tm, d), BF16),
            pltpu.VMEM((tm, d), BF16),
            pltpu.VMEM((WINDOW + tm, 4 * kvw), BF16),
            pltpu.VMEM((WINDOW + tm, 8 * kvw), BF16),
            pltpu.VMEM((tm, d), BF16),
        ],
        compiler_params=pltpu.CompilerParams(
            dimension_semantics=("arbitrary", "arbitrary"),
            vmem_limit_bytes=VMEM_LIMIT),
        name="attn_layer_prompt",
    )(sinks, h, gkv, gpre, gpost, w_kv, w_in, w_out)


def _attn_sample_proj_kernel(h_ref, gkv_ref, gpre_ref, wkv_ref, win_ref, kv_ref, q_ref, sg_ref,
                             ukv_ref, u_ref, *, tm):
    kv_ref[...] = _attn_projections(h_ref, gkv_ref, gpre_ref, wkv_ref, win_ref, ukv_ref, u_ref, q_ref,
                                    sg_ref, tm, 4)


def _attn_sample_core_kernel(sink_ref, q_ref, sg_ref, kvt_ref, ckt_in_ref, cvt_in_ref,
                             act_ref, ckt_ref, cvt_ref, *, n_seq, t_new, n_kv, n_par):
    kvw = n_kv * HEAD_DIM
    w_buf = ckt_in_ref.shape[-1]
    first_new = w_buf - t_new
    assert kvt_ref.shape[-1] == w_buf, "the new K/V rows of one grid step must fill one lane block"
    n_rows = GROUP * n_kv * t_new
    tq = lax.broadcasted_iota(jnp.int32, (n_rows, w_buf), 0) % t_new
    kj = lax.broadcasted_iota(jnp.int32, (n_rows, w_buf), 1)
    mask_c = kj > tq
    mask_n = (kj >= first_new) & (kj - first_new <= tq)
    is_new = lax.broadcasted_iota(jnp.int32, (kvw, w_buf), 1) >= first_new
    row_head = lax.broadcasted_iota(jnp.int32, (n_rows, 1), 0) // t_new
    sinkv = jnp.zeros((n_rows, 1), F32)
    for h in range(GROUP * n_kv):
        sinkv = jnp.where(row_head == h, sink_ref[h] * LOG2E, sinkv)
    pair_w = 2 * HEAD_DIM
    low = lax.broadcasted_iota(jnp.int32, (t_new, pair_w), 1) < HEAD_DIM
    halves = (low, jnp.logical_not(low))
    zero_blk = jnp.zeros((t_new, pair_w), F32)

    def to_half(block, src_half, dst_half):
        moved = block if src_half == dst_half else pltpu.roll(block, HEAD_DIM, axis=1)
        return jnp.where(halves[dst_half], moved, 0.0)

    def one_seq(s):
        rows = pl.ds(pl.multiple_of(s * t_new, t_new), t_new)
        kct = ckt_in_ref[s].reshape(kvw, w_buf)
        vct = cvt_in_ref[s].reshape(kvw, w_buf)
        newt = pltpu.roll(kvt_ref[...], first_new - s * t_new, axis=1)
        knt = jnp.where(is_new, newt[:kvw], 0.0)
        vnt = jnp.where(is_new, newt[kvw:], 0.0)
        ckt_ref[s] = jnp.where(is_new, knt, pltpu.roll(kct, first_new, axis=1)).reshape(ckt_ref.shape[1:])
        cvt_ref[s] = jnp.where(is_new, vnt, pltpu.roll(vct, first_new, axis=1)).reshape(cvt_ref.shape[1:])
        qparts = []
        for cb in range(GROUP * n_kv // 2):
            blk = q_ref[rows, cb * pair_w:(cb + 1) * pair_w].astype(F32)
            g = 2 * cb // GROUP
            for e in range(2):
                piece = to_half(blk, e, g % 2)
                qparts.append(jnp.concatenate(
                    [piece if kb == g // 2 else zero_blk for kb in range(kvw // pair_w)], axis=1))
        qbd = jnp.concatenate(qparts, axis=0).astype(BF16)
        s_c = jnp.where(mask_c, _dot(qbd, kct.astype(BF16)), -jnp.inf)
        s_n = jnp.where(mask_n, _dot(qbd, knt.astype(BF16)), -jnp.inf)
        m = jnp.maximum(jnp.max(jnp.maximum(s_c, s_n), axis=-1, keepdims=True), sinkv)
        p_c = jnp.exp2(s_c - m)
        p_n = jnp.exp2(s_n - m)
        den = jnp.sum(p_c + p_n, axis=-1, keepdims=True) + jnp.exp2(sinkv - m)
        o = (_dot_t(p_c.astype(BF16), vct.astype(BF16))
             + _dot_t(p_n.astype(BF16), vnt.astype(BF16))) * (1.0 / den)
        for cb in range(GROUP * n_kv // 2):
            g = 2 * cb // GROUP
            kcols = slice((g // 2) * pair_w, (g // 2 + 1) * pair_w)
            pieces = [to_half(o[(2 * cb + e) * t_new:(2 * cb + e + 1) * t_new, kcols], g % 2, e)
                      for e in range(2)]
            cols = slice(cb * pair_w, (cb + 1) * pair_w)
            act_ref[rows, cols] = ((pieces[0] + pieces[1]) * sg_ref[rows, cols].astype(F32)).astype(BF16)

    def seq_body(sb, carry):
        for q in range(n_par):
            one_seq(sb * n_par + q)
        return carry
    lax.fori_loop(0, n_seq // n_par, seq_body, 0)


def _attn_sample_out_kernel(act_ref, h_ref, gpost_ref, wout_ref, y_ref, *, tm):
    _attn_output(act_ref, h_ref, gpost_ref, wout_ref, y_ref, tm)


def _attn_layer_sample(h, cache_k, cache_v, sinks, gkv, gpre, gpost, w_kv, w_in, w_out, *, tm,
                       n_seq_tile):
    n_seq, t_new, d = h.shape
    kvw = w_kv.shape[1] // 2
    n_kv = kvw // HEAD_DIM
    w_buf = cache_k.shape[1]
    n_tok = n_seq * t_new
    h2 = h.reshape(n_tok, d)
    vec = lambda: _const_spec((1, d), 1)
    params = pltpu.CompilerParams(dimension_semantics=("arbitrary",), vmem_limit_bytes=VMEM_LIMIT)

    kv, q, sg = pl.pallas_call(
        functools.partial(_attn_sample_proj_kernel, tm=tm),
        grid=(n_tok // tm,),
        in_specs=[
            pl.BlockSpec((tm, d), lambda i: (i, 0)),
            vec(), vec(),
            _const_spec(w_kv.shape, 1),
            _const_spec(w_in.shape, 1),
        ],
        out_specs=[
            pl.BlockSpec((tm, 2 * kvw), lambda i: (i, 0)),
            pl.BlockSpec((tm, d), lambda i: (i, 0)),
            pl.BlockSpec((tm, d), lambda i: (i, 0)),
        ],
        out_shape=[
            jax.ShapeDtypeStruct((n_tok, 2 * kvw), F32),
            jax.ShapeDtypeStruct((n_tok, d), BF16),
            jax.ShapeDtypeStruct((n_tok, d), BF16),
        ],
        scratch_shapes=[pltpu.VMEM((tm, d), BF16), pltpu.VMEM((tm, d), BF16)],
        compiler_params=params,
        name="attn_sample_proj",
    )(h2, gkv, gpre, w_kv, w_in)

    ckt_in = cache_k.transpose(0, 2, 3, 1)
    cvt_in = cache_v.transpose(0, 2, 3, 1)
    rows = n_seq_tile * t_new
    cache_blk = pl.BlockSpec((n_seq_tile, n_kv, HEAD_DIM, w_buf), lambda i: (i, 0, 0, 0))
    act, ckt, cvt = pl.pallas_call(
        functools.partial(_attn_sample_core_kernel, n_seq=n_seq_tile, t_new=t_new, n_kv=n_kv, n_par=4),
        grid=(n_seq // n_seq_tile,),
        in_specs=[
            pl.BlockSpec(memory_space=pltpu.SMEM),
            pl.BlockSpec((rows, d), lambda i: (i, 0)),
            pl.BlockSpec((rows, d), lambda i: (i, 0)),
            pl.BlockSpec((2 * kvw, rows), lambda i: (0, i)),
            cache_blk, cache_blk,
        ],
        out_specs=[pl.BlockSpec((rows, d), lambda i: (i, 0)), cache_blk, cache_blk],
        out_shape=[
            jax.ShapeDtypeStruct((n_tok, d), BF16),
            jax.ShapeDtypeStruct(ckt_in.shape, F32),
            jax.ShapeDtypeStruct(cvt_in.shape, F32),
        ],
        compiler_params=params,
        name="attn_sample_core",
    )(sinks, q, sg, kv.T, ckt_in, cvt_in)
    ck = ckt.transpose(0, 3, 1, 2)
    cv = cvt.transpose(0, 3, 1, 2)

    y = pl.pallas_call(
        functools.partial(_attn_sample_out_kernel, tm=tm),
        grid=(n_tok // tm,),
        in_specs=[
            pl.BlockSpec((tm, d), lambda i: (i, 0)),
            pl.BlockSpec((tm, d), lambda i: (i, 0)),
            vec(),
            _const_spec(w_out.shape, 1),
        ],
        out_specs=pl.BlockSpec((tm, d), lambda i: (i, 0)),
        out_shape=jax.ShapeDtypeStruct((n_tok, d), F32),
        compiler_params=params,
        name="attn_sample_out",
    )(act, h2, gpost, w_out)
    return y.reshape(n_seq, t_new, d), ck, cv


def kernel(x_prompt, x_sample, state_conv, cache_k, cache_v, norm_pre, norm_post, w_in_a, conv_w, conv_b, ln_g, ln_b, w_out_a, kv_norm, w_kv, w_in_b, sinks, w_out_b):
    n_a = w_in_a.shape[0]
    assert n_a == 1 and w_in_b.shape[0] == 1 and norm_pre.shape[0] == 2
    d = x_prompt.shape[-1]
    n_seq, w_buf, n_kv, hd = cache_k.shape
    assert hd == HEAD_DIM and w_buf == WINDOW

    row = lambda v: v.reshape(1, -1)
    w_in_a_bf = w_in_a[0].astype(BF16)
    w_out_a_bf = w_out_a[0].astype(BF16)
    w_kv_bf = w_kv.astype(BF16)
    w_in_b_bf = w_in_b[0].astype(BF16)
    w_out_b_bf = w_out_b[0].astype(BF16)
    conv_args = (w_in_a_bf, conv_w[0], row(conv_b[0]), row(ln_g[0]), row(ln_b[0]), row(norm_pre[0]),
                 row(norm_post[0]), w_out_a_bf)
    attn_args = (sinks[0], row(kv_norm), row(norm_pre[1]), row(norm_post[1]), w_kv_bf, w_in_b_bf,
                 w_out_b_bf)

    h_p, st_p = _conv_layer_prompt(x_prompt, *conv_args, tm=512, cn=256)
    y_p, ck_p, cv_p = _attn_layer_prompt(h_p, *attn_args, tm=256)

    h_s, st_s = _conv_layer_sample(x_sample, state_conv[0], *conv_args, n_seq_tile=64, cn=256)
    y_s, ck_s, cv_s = _attn_layer_sample(
        h_s, cache_k, cache_v,
        *attn_args, tm=256, n_seq_tile=16)

    b = x_prompt.shape[0]
    return (y_p, y_s, st_p[None], ck_p.reshape(b, w_buf, n_kv, hd), cv_p.reshape(b, w_buf, n_kv, hd),
            st_s[None], ck_s, cv_s)
```

```python
import functools

import jax
import jax.numpy as jnp
from jax import lax
from jax.experimental import pallas as pl
from jax.experimental.pallas import tpu as pltpu

RMS_EPS = 1e-6
LN_EPS = 1e-5
HEAD_DIM = 64
GROUP = 8
WINDOW = 128
SUBLANES = 8
LANES = 128
LOG2E = 1.4426950408889634
HALO = 32
VMEM_LIMIT = 56 * 1024 * 1024
ROW_BLK = 64

BF16 = jnp.bfloat16
F32 = jnp.float32


def _rows(i, n):
    return pl.ds(pl.multiple_of(i * n, n), n)


def _rms_scale(x):
    return lax.rsqrt(jnp.mean(x * x, axis=-1, keepdims=True) + RMS_EPS)


def _dot(a, b):
    return jnp.dot(a, b, preferred_element_type=F32)


def _dot_t(a, b):
    return lax.dot_general(a, b, (((1,), (1,)), ((), ())), preferred_element_type=F32)


def _pre_norm_to_bf16(x_ref, g_ref, u_ref, tm):
    def body(r, c):
        rows = _rows(r, ROW_BLK)
        x = x_ref[rows, :]
        u_ref[rows, :] = (x * _rms_scale(x) * g_ref[...]).astype(BF16)
        return c
    lax.fori_loop(0, tm // ROW_BLK, body, 0, unroll=2)


def _glu_chunk(u_ref, wa_ref, wb_ref, wg_ref):
    u = u_ref[...]
    c = _dot(u, wa_ref[...]) * jax.nn.sigmoid(_dot(u, wb_ref[...]))
    sg = jax.nn.silu(_dot(u, wg_ref[...])).astype(BF16)
    return c, sg


def _broadcast_taps(cw_ref, wbc_ref, taps):
    for k in range(taps):
        wbc_ref[k] = jnp.broadcast_to(cw_ref[k:k + 1, :], wbc_ref.shape[1:])


def _conv_finalize(j_chunks, tm, cn, y_ref, sg_ref, act_ref, x_ref, h_ref, cb_ref, lng_ref, lnb_ref,
                   gpost_ref, wout_ref):
    d = j_chunks * cn

    ln_rows = ROW_BLK

    def ln_body(r, c):
        rows = _rows(r, ln_rows)
        ys = [y_ref[jj, rows, :] + cb_ref[:, jj * cn:(jj + 1) * cn] for jj in range(j_chunks)]
        mu = jnp.sum(sum(ys), axis=-1, keepdims=True) * (1.0 / d)
        yc = [y - mu for y in ys]
        var = jnp.sum(sum(y * y for y in yc), axis=-1, keepdims=True) * (1.0 / d)
        rstd = lax.rsqrt(var + LN_EPS)
        for jj in range(j_chunks):
            cols = slice(jj * cn, (jj + 1) * cn)
            t = jax.nn.silu(yc[jj] * rstd * lng_ref[:, cols] + lnb_ref[:, cols])
            act_ref[rows, cols] = (t * sg_ref[jj, rows, :].astype(F32)).astype(BF16)
        return c
    lax.fori_loop(0, tm // ln_rows, ln_body, 0, unroll=2)

    h_ref[...] = _dot(act_ref[...], wout_ref[...])

    def post_body(r, c):
        rows = [_rows(2 * r + e, ROW_BLK) for e in range(2)]
        os = [h_ref[rw, :] for rw in rows]
        res = [x_ref[rw, :] + o * _rms_scale(o) * gpost_ref[...] for rw, o in zip(rows, os)]
        for rw, v in zip(rows, res):
            h_ref[rw, :] = v
        return c
    lax.fori_loop(0, tm // (2 * ROW_BLK), post_body, 0)


def _conv_chunk(cext_ref, wbc_ref, y_ref, jc, *, tm, cn, taps, grp):
    lbs = cn // LANES
    assert lbs >= 2, "the row-interleaved layout needs at least two lane blocks per chunk"
    lead = HALO - (taps - 1)
    for g in range(tm // (grp * SUBLANES)):
        base = g * grp * SUBLANES
        for lb in range(lbs):
            ls = slice(lb * LANES, (lb + 1) * LANES)
            acc = [None] * grp
            for k in range(taps):
                wv = wbc_ref[jc, k, :, ls]
                for gi in range(grp):
                    r0 = base + SUBLANES * gi + k + lead
                    t = wv * cext_ref[jc, pl.ds(lbs * r0 + lb, SUBLANES, stride=lbs), :]
                    acc[gi] = t if acc[gi] is None else acc[gi] + t
            for gi in range(grp):
                y_ref[jc, base + SUBLANES * gi:base + SUBLANES * (gi + 1), ls] = acc[gi]


def _conv_prompt_kernel(x_ref, w_hbm, cw_ref, cb_ref, lng_ref, lnb_ref, gpre_ref, gpost_ref, wout_ref,
                        h_ref, st_ref,
                        u_ref, wbuf_ref, wsem, cext_ref, wbc_ref, y_ref, sg_ref, act_ref,
                        *, tm, cn, taps, grp):
    bb = pl.program_id(0)
    i = pl.program_id(1)
    n_chunks = cext_ref.shape[0]
    lbs = cn // LANES
    first_step = (bb == 0) & (i == 0)
    last_step = (bb == pl.num_programs(0) - 1) & (i == pl.num_programs(1) - 1)

    def weight_copy(jc, slot, part):
        col0 = pl.multiple_of((part * n_chunks + jc) * cn, cn)
        return pltpu.make_async_copy(w_hbm.at[:, pl.ds(col0, cn)], wbuf_ref.at[slot, part],
                                     wsem.at[slot, part])

    @pl.when(first_step)
    def _():
        for part in range(3):
            weight_copy(0, 0, part).start()
        for jj in range(n_chunks):
            for k in range(taps):
                wbc_ref[jj, k] = jnp.broadcast_to(cw_ref[k:k + 1, jj * cn:(jj + 1) * cn], (SUBLANES, cn))

    _pre_norm_to_bf16(x_ref, gpre_ref, u_ref, tm)

    def chunk_body(j, carry):
        slot = j % 2
        for part in range(3):
            weight_copy(j, slot, part).wait()

        @pl.when(jnp.logical_not(last_step & (j == n_chunks - 1)))
        def _():
            nxt = jnp.where(j == n_chunks - 1, 0, j + 1)
            for part in range(3):
                weight_copy(nxt, 1 - slot, part).start()

        @pl.when(i == 0)
        def _():
            cext_ref[j, 0:lbs * HALO, :] = jnp.zeros((lbs * HALO, LANES), F32)

        @pl.when(i > 0)
        def _():
            cext_ref[j, 0:lbs * HALO, :] = cext_ref[j, lbs * tm:lbs * (tm + HALO), :]

        u = u_ref[...]
        c = _dot(u, wbuf_ref[slot, 0]) * jax.nn.sigmoid(_dot(u, wbuf_ref[slot, 1]))
        sg_ref[j] = jax.nn.silu(_dot(u, wbuf_ref[slot, 2])).astype(BF16)
        for lb in range(lbs):
            cext_ref[j, pl.ds(lbs * HALO + lb, tm, stride=lbs), :] = c[:, lb * LANES:(lb + 1) * LANES]
        _conv_chunk(cext_ref, wbc_ref, y_ref, j, tm=tm, cn=cn, taps=taps, grp=grp)
        return carry
    lax.fori_loop(0, n_chunks, chunk_body, 0)

    _conv_finalize(n_chunks, tm, cn, y_ref, sg_ref, act_ref, x_ref, h_ref, cb_ref,
                   lng_ref, lnb_ref, gpost_ref, wout_ref)

    @pl.when(i == pl.num_programs(1) - 1)
    def _():
        tail = lbs * (HALO + tm - (taps - 1))
        for jj in range(n_chunks):
            for lb in range(lbs):
                c0 = jj * cn + lb * LANES
                st_ref[:, c0:c0 + LANES] = cext_ref[jj, pl.ds(tail + lb, taps - 1, stride=lbs), :]


def _conv_sample_kernel(x_ref, st_in_ref, wa_ref, wb_ref, wg_ref, cw_ref, cb_ref, lng_ref, lnb_ref,
                        gpre_ref, gpost_ref, wout_ref, h_ref, st_ref,
                        u_ref, c_ref, ystg_ref, wbc_ref, y_ref, sg_ref, act_ref, *, tm, cn, taps, t_new):
    j = pl.program_id(1)
    n_j = pl.num_programs(1)
    n_seq = tm // t_new
    hist = taps - 1

    @pl.when(j == 0)
    def _():
        _pre_norm_to_bf16(x_ref, gpre_ref, u_ref, tm)

    c, sg = _glu_chunk(u_ref, wa_ref, wb_ref, wg_ref)
    sg_ref[j] = sg
    _broadcast_taps(cw_ref, wbc_ref, taps)

    for lb in range(cn // LANES):
        ls = slice(lb * LANES, (lb + 1) * LANES)
        c_ref[lb] = c[:, ls]
        new_rows = [c_ref[lb, pl.ds(t, n_seq, stride=t_new), :] for t in range(t_new)]
        for r in range(hist):
            st_ref[r, :, ls] = st_in_ref[r + t_new, :, ls] if r + t_new < hist else new_rows[r + t_new - hist]
        for sb in range(n_seq // SUBLANES):
            seqs = slice(sb * SUBLANES, (sb + 1) * SUBLANES)
            full = [st_in_ref[r, seqs, ls] for r in range(hist)] + [nr[seqs] for nr in new_rows]
            acc = [None] * t_new
            for k in range(taps):
                wv = wbc_ref[k, :, ls]
                for t in range(t_new):
                    term = wv * full[t + k]
                    acc[t] = term if acc[t] is None else acc[t] + term
            for t in range(t_new):
                ystg_ref[lb, pl.ds(sb * SUBLANES * t_new + t, SUBLANES, stride=t_new), :] = acc[t]
        y_ref[j, :, ls] = ystg_ref[lb]

    @pl.when(j == n_j - 1)
    def _():
        _conv_finalize(y_ref.shape[0], tm, cn, y_ref, sg_ref, act_ref, x_ref, h_ref, cb_ref,
                       lng_ref, lnb_ref, gpost_ref, wout_ref)


def _const_spec(shape, n_grid):
    zeros = (0,) * len(shape)
    return pl.BlockSpec(shape, lambda *_: zeros, pipeline_mode=pl.Buffered(1))


def _conv_layer_prompt(x, w_in, cw, cb, lng, lnb, gpre, gpost, w_out, *, tm, cn):
    b, t, d = x.shape
    taps = cw.shape[0]
    n_j = d // cn
    lbs = cn // LANES
    grp = 4
    kern = functools.partial(_conv_prompt_kernel, tm=tm, cn=cn, taps=taps, grp=grp)
    vec = lambda: _const_spec((1, d), 2)
    return pl.pallas_call(
        kern,
        grid=(b, t // tm),
        in_specs=[
            pl.BlockSpec((None, tm, d), lambda bb, i: (bb, i, 0)),
            pl.BlockSpec(memory_space=pl.ANY),
            _const_spec((taps, d), 2),
            vec(), vec(), vec(), vec(), vec(),
            _const_spec((d, d), 2),
        ],
        out_specs=[
            pl.BlockSpec((None, tm, d), lambda bb, i: (bb, i, 0)),
            pl.BlockSpec((None, taps - 1, d), lambda bb, i: (bb, 0, 0)),
        ],
        out_shape=[
            jax.ShapeDtypeStruct((b, t, d), F32),
            jax.ShapeDtypeStruct((b, taps - 1, d), F32),
        ],
        scratch_shapes=[
            pltpu.VMEM((tm, d), BF16),
            pltpu.VMEM((2, 3, d, cn), BF16),
            pltpu.SemaphoreType.DMA((2, 3)),
            pltpu.VMEM((n_j, lbs * (HALO + tm), LANES), F32),
            pltpu.VMEM((n_j, taps, SUBLANES, cn), F32),
            pltpu.VMEM((n_j, tm, cn), F32),
            pltpu.VMEM((n_j, tm, cn), BF16),
            pltpu.VMEM((tm, d), BF16),
        ],
        compiler_params=pltpu.CompilerParams(
            dimension_semantics=("arbitrary", "arbitrary"),
            vmem_limit_bytes=VMEM_LIMIT),
        name="conv_layer_prompt",
    )(x, w_in, cw, cb, lng, lnb, gpre, gpost, w_out)


def _conv_layer_sample(x, state, w_in, cw, cb, lng, lnb, gpre, gpost, w_out, *, n_seq_tile, cn):
    n_seq, t_new, d = x.shape
    taps = cw.shape[0]
    n_j = d // cn
    tm = n_seq_tile * t_new
    x2 = x.reshape(n_seq * t_new, d)
    kern = functools.partial(_conv_sample_kernel, tm=tm, cn=cn, taps=taps, t_new=t_new)
    vec = lambda: _const_spec((1, d), 2)
    h, st = pl.pallas_call(
        kern,
        grid=(n_seq // n_seq_tile, n_j),
        in_specs=[
            pl.BlockSpec((tm, d), lambda i, j: (i, 0)),
            pl.BlockSpec((taps - 1, n_seq_tile, cn), lambda i, j: (0, i, j)),
            pl.BlockSpec((d, cn), lambda i, j: (0, j)),
            pl.BlockSpec((d, cn), lambda i, j: (0, n_j + j)),
            pl.BlockSpec((d, cn), lambda i, j: (0, 2 * n_j + j)),
            pl.BlockSpec((taps, cn), lambda i, j: (0, j)),
            vec(), vec(), vec(), vec(), vec(),
            _const_spec((d, d), 2),
        ],
        out_specs=[
            pl.BlockSpec((tm, d), lambda i, j: (i, 0)),
            pl.BlockSpec((taps - 1, n_seq_tile, cn), lambda i, j: (0, i, j)),
        ],
        out_shape=[
            jax.ShapeDtypeStruct((n_seq * t_new, d), F32),
            jax.ShapeDtypeStruct((taps - 1, n_seq, d), F32),
        ],
        scratch_shapes=[
            pltpu.VMEM((tm, d), BF16),
            pltpu.VMEM((cn // LANES, tm, LANES), F32),
            pltpu.VMEM((cn // LANES, tm, LANES), F32),
            pltpu.VMEM((taps, SUBLANES, cn), F32),
            pltpu.VMEM((n_j, tm, cn), F32),
            pltpu.VMEM((n_j, tm, cn), BF16),
            pltpu.VMEM((tm, d), BF16),
        ],
        compiler_params=pltpu.CompilerParams(
            dimension_semantics=("arbitrary", "arbitrary"),
            vmem_limit_bytes=VMEM_LIMIT),
        name="conv_layer_sample",
    )(x2, state.transpose(1, 0, 2), w_in, w_in, w_in, cw, cb, lng, lnb, gpre, gpost, w_out)
    return h.reshape(n_seq, t_new, d), st.transpose(1, 0, 2)


def _attn_projections(h_ref, gkv_ref, gpre_ref, wkv_ref, win_ref, ukv_ref, u_ref, q_ref, sg_ref, tm,
                      n_chunk):
    d = h_ref.shape[-1]

    n_col = 4

    def body(r, c):
        rows = _rows(r, ROW_BLK)
        rs = _rms_scale(h_ref[rows, :])
        for cc in range(n_col):
            cols = slice(cc * d // n_col, (cc + 1) * d // n_col)
            xn = h_ref[rows, cols] * rs
            ukv_ref[rows, cols] = (xn * gkv_ref[:, cols]).astype(BF16)
            u_ref[rows, cols] = (xn * gpre_ref[:, cols]).astype(BF16)
        return c
    lax.fori_loop(0, tm // ROW_BLK, body, 0, unroll=2)

    kv = _dot(ukv_ref[...], wkv_ref[...])
    u = u_ref[...]
    cw = d // n_chunk
    scale = HEAD_DIM ** -0.5 * LOG2E
    for n in range(n_chunk):
        cols = slice(n * cw, (n + 1) * cw)
        q_ref[:, cols] = (_dot(u, win_ref[:, cols]) * scale).astype(BF16)
    for n in range(n_chunk):
        cols = slice(n * cw, (n + 1) * cw)
        sg_ref[:, cols] = jax.nn.silu(_dot(u, win_ref[:, d + n * cw:d + (n + 1) * cw])).astype(BF16)
    return kv


def _attn_output(act_ref, h_ref, gpost_ref, wout_ref, y_ref, tm):
    y_ref[...] = _dot(act_ref[...], wout_ref[...])

    def post_body(r, c):
        rows = [_rows(2 * r + e, ROW_BLK) for e in range(2)]
        os = [y_ref[rw, :] for rw in rows]
        res = [h_ref[rw, :] + o * _rms_scale(o) * gpost_ref[...] for rw, o in zip(rows, os)]
        for rw, v in zip(rows, res):
            y_ref[rw, :] = v
        return c
    lax.fori_loop(0, tm // (2 * ROW_BLK), post_body, 0)


def _attn_block(sink_ref, q_ref, sg_ref, kpad_ref, vpad_ref, act_ref, r0, *, n_kv, first):
    blk = WINDOW
    pair_w = 2 * HEAD_DIM
    n_pair = GROUP // 2
    qi = lax.broadcasted_iota(jnp.int32, (blk, blk), 0)
    kj = lax.broadcasted_iota(jnp.int32, (blk, blk), 1)
    own = kj <= qi
    low = lax.broadcasted_iota(jnp.int32, (blk, pair_w), 1) < HEAD_DIM
    rows = pl.ds(r0, blk)
    krows = pl.ds(r0, 2 * blk)
    for g in range(n_kv):
        qs = jnp.concatenate(
            [q_ref[rows, (g * n_pair + pp) * pair_w:(g * n_pair + pp + 1) * pair_w]
             for pp in range(n_pair)], axis=0)
        probs, stats = [], []
        for e in range(2):
            s = _dot_t(qs, kpad_ref[krows, (2 * g + e) * pair_w:(2 * g + e + 1) * pair_w])
            p_parts, st = [], []
            for pp in range(n_pair):
                sh = s[pp * blk:(pp + 1) * blk]
                prev = sh[:, :blk] if first is False else sh[:, :blk] + jnp.where(first, -jnp.inf, 0.0)
                sf = jnp.where(own, sh[:, blk:], prev)
                sink = sink_ref[g * GROUP + 2 * pp + e] * LOG2E
                m = jnp.maximum(jnp.max(sf, axis=-1, keepdims=True), sink)
                p = jnp.exp2(sf - m)
                p_parts.append(jnp.concatenate([jnp.where(own, 0.0, p), jnp.where(own, p, 0.0)],
                                               axis=1).astype(BF16))
                st.append(jnp.exp2(sink - m))
            probs.append(jnp.concatenate(p_parts, axis=0))
            stats.append(st)
        o = (_dot(probs[0], vpad_ref[krows, (2 * g) * 2 * pair_w:(2 * g + 1) * 2 * pair_w])
             + _dot(probs[1], vpad_ref[krows, (2 * g + 1) * 2 * pair_w:(2 * g + 2) * 2 * pair_w]))
        for pp in range(n_pair):
            oh = o[pp * blk:(pp + 1) * blk]
            den = oh[:, pair_w:] + jnp.where(low, stats[0][pp], stats[1][pp])
            c0 = (g * n_pair + pp) * pair_w
            act_ref[rows, c0:c0 + pair_w] = (
                oh[:, :pair_w] / den * sg_ref[rows, c0:c0 + pair_w].astype(F32)).astype(BF16)


def _store_padded_kv(kv, kpad_ref, vpad_ref, row0, n_kv):
    n = kv.shape[0]
    kvw = n_kv * HEAD_DIM
    pair_w = 2 * HEAD_DIM
    low = lax.broadcasted_iota(jnp.int32, (n, pair_w), 1) < HEAD_DIM
    halves = (low, jnp.logical_not(low))
    for gam in range(n_kv // 2):
        kc = kv[:, gam * pair_w:(gam + 1) * pair_w]
        vc = kv[:, kvw + gam * pair_w:kvw + (gam + 1) * pair_w]
        for side in range(2):
            g = 2 * gam + side
            k_here = jnp.where(halves[side], kc, 0.0)
            v_here = jnp.where(halves[side], vc, 0.0)
            k_other = pltpu.roll(k_here, HEAD_DIM, axis=1)
            v_other = pltpu.roll(v_here, HEAD_DIM, axis=1)
            for e in range(2):
                kc0 = (2 * g + e) * pair_w
                vc0 = (2 * g + e) * 2 * pair_w
                kpad_ref[row0:row0 + n, kc0:kc0 + pair_w] = (k_here if e == side else k_other).astype(BF16)
                vpad_ref[row0:row0 + n, vc0:vc0 + pair_w] = (v_here if e == side else v_other).astype(BF16)
                vpad_ref[row0:row0 + n, vc0 + pair_w:vc0 + 2 * pair_w] = (
                    jnp.where(halves[e], 1.0, 0.0).astype(BF16))


def _attn_prompt_kernel(sink_ref, h_ref, gkv_ref, gpre_ref, gpost_ref, wkv_ref, win_ref, wout_ref,
                        y_ref, ck_ref, cv_ref,
                        ukv_ref, u_ref, q_ref, sg_ref, kpad_ref, vpad_ref, act_ref, *, tm, n_kv):
    i = pl.program_id(1)
    kvw = n_kv * HEAD_DIM
    blk = WINDOW

    @pl.when((pl.program_id(0) == 0) & (i == 0))
    def _():
        kpad_ref[...] = jnp.zeros(kpad_ref.shape, BF16)
        vpad_ref[...] = jnp.zeros(vpad_ref.shape, BF16)

    kv = _attn_projections(h_ref, gkv_ref, gpre_ref, wkv_ref, win_ref, ukv_ref, u_ref, q_ref, sg_ref,
                           tm, 4)

    keep = i > 0
    kpad_ref[0:blk, :] = jnp.where(keep, kpad_ref[tm:tm + blk, :], jnp.zeros((), BF16))
    vpad_ref[0:blk, :] = jnp.where(keep, vpad_ref[tm:tm + blk, :], jnp.zeros((), BF16))

    _store_padded_kv(kv, kpad_ref, vpad_ref, blk, n_kv)

    ck_ref[...] = kv[tm - blk:, :kvw]
    cv_ref[...] = kv[tm - blk:, kvw:]

    block = functools.partial(_attn_block, sink_ref, q_ref, sg_ref, kpad_ref, vpad_ref, act_ref, n_kv=n_kv)

    for bi in range(tm // blk):
        block(bi * blk, first=(i == 0) if bi == 0 else False)

    _attn_output(act_ref, h_ref, gpost_ref, wout_ref, y_ref, tm)


def _attn_layer_prompt(h, sinks, gkv, gpre, gpost, w_kv, w_in, w_out, *, tm):
    b, t, d = h.shape
    kvw = w_kv.shape[1] // 2
    n_kv = kvw // HEAD_DIM
    kern = functools.partial(_attn_prompt_kernel, tm=tm, n_kv=n_kv)
    vec = lambda: _const_spec((1, d), 2)
    return pl.pallas_call(
        kern,
        grid=(b, t // tm),
        in_specs=[
            pl.BlockSpec(memory_space=pltpu.SMEM),
            pl.BlockSpec((None, tm, d), lambda bb, i: (bb, i, 0)),
            vec(), vec(), vec(),
            _const_spec(w_kv.shape, 2),
            _const_spec(w_in.shape, 2),
            _const_spec(w_out.shape, 2),
        ],
        out_specs=[
            pl.BlockSpec((None, tm, d), lambda bb, i: (bb, i, 0)),
            pl.BlockSpec((None, WINDOW, kvw), lambda bb, i: (bb, 0, 0)),
            pl.BlockSpec((None, WINDOW, kvw), lambda bb, i: (bb, 0, 0)),
        ],
        out_shape=[
            jax.ShapeDtypeStruct((b, t, d), F32),
            jax.ShapeDtypeStruct((b, WINDOW, kvw), F32),
            jax.ShapeDtypeStruct((b, WINDOW, kvw), F32),
        ],
        scratch_shapes=[
            pltpu.VMEM((tm, d), BF16),
            pltpu.VMEM((tm, d), BF16),
            pltpu.VMEM((tm, d), BF16),
            pltpu.VMEM((tm, d), BF16),
            pltpu.VMEM((WINDOW + tm, 4 * kvw), BF16),
            pltpu.VMEM((WINDOW + tm, 8 * kvw), BF16),
            pltpu.VMEM((tm, d), BF16),
        ],
        compiler_params=pltpu.CompilerParams(
            dimension_semantics=("arbitrary", "arbitrary"),
            vmem_limit_bytes=VMEM_LIMIT),
        name="attn_layer_prompt",
    )(sinks, h, gkv, gpre, gpost, w_kv, w_in, w_out)


def _attn_sample_proj_kernel(h_ref, gkv_ref, gpre_ref, wkv_ref, win_ref, kv_ref, q_ref, sg_ref,
                             ukv_ref, u_ref, *, tm):
    kv_ref[...] = _attn_projections(h_ref, gkv_ref, gpre_ref, wkv_ref, win_ref, ukv_ref, u_ref, q_ref,
                                    sg_ref, tm, 4)


def _attn_sample_core_kernel(sink_ref, q_ref, sg_ref, kvt_ref, ckt_in_ref, cvt_in_ref,
                             act_ref, ckt_ref, cvt_ref, *, n_seq, t_new, n_kv, n_par):
    kvw = n_kv * HEAD_DIM
    w_buf = ckt_in_ref.shape[-1]
    first_new = w_buf - t_new
    assert kvt_ref.shape[-1] == w_buf, "the new K/V rows of one grid step must fill one lane block"
    n_rows = GROUP * n_kv * t_new
    tq = lax.broadcasted_iota(jnp.int32, (n_rows, w_buf), 0) % t_new
    kj = lax.broadcasted_iota(jnp.int32, (n_rows, w_buf), 1)
    mask_c = kj > tq
    mask_n = (kj >= first_new) & (kj - first_new <= tq)
    is_new = lax.broadcasted_iota(jnp.int32, (kvw, w_buf), 1) >= first_new
    row_head = lax.broadcasted_iota(jnp.int32, (n_rows, 1), 0) // t_new
    sinkv = jnp.zeros((n_rows, 1), F32)
    for h in range(GROUP * n_kv):
        sinkv = jnp.where(row_head == h, sink_ref[h] * LOG2E, sinkv)
    pair_w = 2 * HEAD_DIM
    low = lax.broadcasted_iota(jnp.int32, (t_new, pair_w), 1) < HEAD_DIM
    halves = (low, jnp.logical_not(low))
    zero_blk = jnp.zeros((t_new, pair_w), F32)

    def to_half(block, src_half, dst_half):
        moved = block if src_half == dst_half else pltpu.roll(block, HEAD_DIM, axis=1)
        return jnp.where(halves[dst_half], moved, 0.0)

    def one_seq(s):
        rows = pl.ds(pl.multiple_of(s * t_new, t_new), t_new)
        kct = ckt_in_ref[s].reshape(kvw, w_buf)
        vct = cvt_in_ref[s].reshape(kvw, w_buf)
        newt = pltpu.roll(kvt_ref[...], first_new - s * t_new, axis=1)
        knt = jnp.where(is_new, newt[:kvw], 0.0)
        vnt = jnp.where(is_new, newt[kvw:], 0.0)
        ckt_ref[s] = jnp.where(is_new, knt, pltpu.roll(kct, first_new, axis=1)).reshape(ckt_ref.shape[1:])
        cvt_ref[s] = jnp.where(is_new, vnt, pltpu.roll(vct, first_new, axis=1)).reshape(cvt_ref.shape[1:])
        qparts = []
        for cb in range(GROUP * n_kv // 2):
            blk = q_ref[rows, cb * pair_w:(cb + 1) * pair_w].astype(F32)
            g = 2 * cb // GROUP
            for e in range(2):
                piece = to_half(blk, e, g % 2)
                qparts.append(jnp.concatenate(
                    [piece if kb == g // 2 else zero_blk for kb in range(kvw // pair_w)], axis=1))
        qbd = jnp.concatenate(qparts, axis=0).astype(BF16)
        s_c = jnp.where(mask_c, _dot(qbd, kct.astype(BF16)), -jnp.inf)
        s_n = jnp.where(mask_n, _dot(qbd, knt.astype(BF16)), -jnp.inf)
        m = jnp.maximum(jnp.max(jnp.maximum(s_c, s_n), axis=-1, keepdims=True), sinkv)
        p_c = jnp.exp2(s_c - m)
        p_n = jnp.exp2(s_n - m)
        den = jnp.sum(p_c + p_n, axis=-1, keepdims=True) + jnp.exp2(sinkv - m)
        o = (_dot_t(p_c.astype(BF16), vct.astype(BF16))
             + _dot_t(p_n.astype(BF16), vnt.astype(BF16))) * (1.0 / den)
        for cb in range(GROUP * n_kv // 2):
            g = 2 * cb // GROUP
            kcols = slice((g // 2) * pair_w, (g // 2 + 1) * pair_w)
            pieces = [to_half(o[(2 * cb + e) * t_new:(2 * cb + e + 1) * t_new, kcols], g % 2, e)
                      for e in range(2)]
            cols = slice(cb * pair_w, (cb + 1) * pair_w)
            act_ref[rows, cols] = ((pieces[0] + pieces[1]) * sg_ref[rows, cols].astype(F32)).astype(BF16)

    def seq_body(sb, carry):
        for q in range(n_par):
            one_seq(sb * n_par + q)
        return carry
    lax.fori_loop(0, n_seq // n_par, seq_body, 0)


def _attn_sample_out_kernel(act_ref, h_ref, gpost_ref, wout_ref, y_ref, *, tm):
    _attn_output(act_ref, h_ref, gpost_ref, wout_ref, y_ref, tm)


def _attn_layer_sample(h, cache_k, cache_v, sinks, gkv, gpre, gpost, w_kv, w_in, w_out, *, tm,
                       n_seq_tile):
    n_seq, t_new, d = h.shape
    kvw = w_kv.shape[1] // 2
    n_kv = kvw // HEAD_DIM
    w_buf = cache_k.shape[1]
    n_tok = n_seq * t_new
    h2 = h.reshape(n_tok, d)
    vec = lambda: _const_spec((1, d), 1)
    params = pltpu.CompilerParams(dimension_semantics=("arbitrary",), vmem_limit_bytes=VMEM_LIMIT)

    kv, q, sg = pl.pallas_call(
        functools.partial(_attn_sample_proj_kernel, tm=tm),
        grid=(n_tok // tm,),
        in_specs=[
            pl.BlockSpec((tm, d), lambda i: (i, 0)),
            vec(), vec(),
            _const_spec(w_kv.shape, 1),
            _const_spec(w_in.shape, 1),
        ],
        out_specs=[
            pl.BlockSpec((tm, 2 * kvw), lambda i: (i, 0)),
            pl.BlockSpec((tm, d), lambda i: (i, 0)),
            pl.BlockSpec((tm, d), lambda i: (i, 0)),
        ],
        out_shape=[
            jax.ShapeDtypeStruct((n_tok, 2 * kvw), F32),
            jax.ShapeDtypeStruct((n_tok, d), BF16),
            jax.ShapeDtypeStruct((n_tok, d), BF16),
        ],
        scratch_shapes=[pltpu.VMEM((tm, d), BF16), pltpu.VMEM((tm, d), BF16)],
        compiler_params=params,
        name="attn_sample_proj",
    )(h2, gkv, gpre, w_kv, w_in)

    ckt_in = cache_k.transpose(0, 2, 3, 1)
    cvt_in = cache_v.transpose(0, 2, 3, 1)
    rows = n_seq_tile * t_new
    cache_blk = pl.BlockSpec((n_seq_tile, n_kv, HEAD_DIM, w_buf), lambda i: (i, 0, 0, 0))
    act, ckt, cvt = pl.pallas_call(
        functools.partial(_attn_sample_core_kernel, n_seq=n_seq_tile, t_new=t_new, n_kv=n_kv, n_par=4),
        grid=(n_seq // n_seq_tile,),
        in_specs=[
            pl.BlockSpec(memory_space=pltpu.SMEM),
            pl.BlockSpec((rows, d), lambda i: (i, 0)),
            pl.BlockSpec((rows, d), lambda i: (i, 0)),
            pl.BlockSpec((2 * kvw, rows), lambda i: (0, i)),
            cache_blk, cache_blk,
        ],
        out_specs=[pl.BlockSpec((rows, d), lambda i: (i, 0)), cache_blk, cache_blk],
        out_shape=[
            jax.ShapeDtypeStruct((n_tok, d), BF16),
            jax.ShapeDtypeStruct(ckt_in.shape, F32),
            jax.ShapeDtypeStruct(cvt_in.shape, F32),
        ],
        compiler_params=params,
        name="attn_sample_core",
    )(sinks, q, sg, kv.T, ckt_in, cvt_in)
    ck = ckt.transpose(0, 3, 1, 2)
    cv = cvt.transpose(0, 3, 1, 2)

    y = pl.pallas_call(
        functools.partial(_attn_sample_out_kernel, tm=tm),
        grid=(n_tok // tm,),
        in_specs=[
            pl.BlockSpec((tm, d), lambda i: (i, 0)),
            pl.BlockSpec((tm, d), lambda i: (i, 0)),
            vec(),
            _const_spec(w_out.shape, 1),
        ],
        out_specs=pl.BlockSpec((tm, d), lambda i: (i, 0)),
        out_shape=jax.ShapeDtypeStruct((n_tok, d), F32),
        compiler_params=params,
        name="attn_sample_out",
    )(act, h2, gpost, w_out)
    return y.reshape(n_seq, t_new, d), ck, cv


def kernel(x_prompt, x_sample, state_conv, cache_k, cache_v, norm_pre, norm_post, w_in_a, conv_w, conv_b, ln_g, ln_b, w_out_a, kv_norm, w_kv, w_in_b, sinks, w_out_b):
    n_a = w_in_a.shape[0]
    assert n_a == 1 and w_in_b.shape[0] == 1 and norm_pre.shape[0] == 2
    d = x_prompt.shape[-1]
    n_seq, w_buf, n_kv, hd = cache_k.shape
    assert hd == HEAD_DIM and w_buf == WINDOW

    row = lambda v: v.reshape(1, -1)
    w_in_a_bf = w_in_a[0].astype(BF16)
    w_out_a_bf = w_out_a[0].astype(BF16)
    w_kv_bf = w_kv.astype(BF16)
    w_in_b_bf = w_in_b[0].astype(BF16)
    w_out_b_bf = w_out_b[0].astype(BF16)
    conv_args = (w_in_a_bf, conv_w[0], row(conv_b[0]), row(ln_g[0]), row(ln_b[0]), row(norm_pre[0]),
                 row(norm_post[0]), w_out_a_bf)
    attn_args = (sinks[0], row(kv_norm), row(norm_pre[1]), row(norm_post[1]), w_kv_bf, w_in_b_bf,
                 w_out_b_bf)

    h_p, st_p = _conv_layer_prompt(x_prompt, *conv_args, tm=512, cn=256)
    y_p, ck_p, cv_p = _attn_layer_prompt(h_p, *attn_args, tm=256)

    h_s, st_s = _conv_layer_sample(x_sample, state_conv[0], *conv_args, n_seq_tile=64, cn=256)
    y_s, ck_s, cv_s = _attn_layer_sample(
        h_s, cache_k, cache_v,
        *attn_args, tm=256, n_seq_tile=16)

    b = x_prompt.shape[0]
    return (y_p, y_s, st_p[None], ck_p.reshape(b, w_buf, n_kv, hd), cv_p.reshape(b, w_buf, n_kv, hd),
            st_s[None], ck_s, cv_s)
```
